```python
import math
import jax, jax.numpy as jnp
from jax import lax
import numpy as np

D_MODEL = 1024
BATCH = 4
SEQ = 8192
DEPTH = 1

CHUNK = 64
NORM_EPS = 1e-5
SSM_WIDTH = D_MODEL // 2
SSM_GROUP = 16
SSM_GROUPS = SSM_WIDTH // SSM_GROUP
SSM_STATE = 64
DT_MIN = 1e-3
DT_MAX = 1e-1
SGU_WIDTH = D_MODEL // 2
SGU_BLOCK = 128
SGU_HEADS = 4
SGU_HEAD_DIM = SGU_WIDTH // SGU_HEADS
IN_WIDTH = SSM_WIDTH + 2 * SGU_WIDTH + 2 * D_MODEL
N_EXPERTS = 32
TOP_K = 4
D_EXPERT = D_MODEL
SWIGLU_ALPHA = 1.702
SWIGLU_LIMIT = 7.0
MOE_BLOCK = 512

kernel_name = "hybrid_s5_gmlp_moe_block"


def rmsnorm(x, g):
    xf = x.astype(jnp.float32)
    y = xf * lax.rsqrt(jnp.mean(xf * xf, axis=-1, keepdims=True) + NORM_EPS)
    return (y * g.astype(jnp.float32)).astype(x.dtype)


def layernorm(x, g, b):
    xf = x.astype(jnp.float32)
    mu = jnp.mean(xf, axis=-1, keepdims=True)
    xc = xf - mu
    y = xc * lax.rsqrt(jnp.mean(xc * xc, axis=-1, keepdims=True) + NORM_EPS)
    return (y * g.astype(jnp.float32) + b.astype(jnp.float32)).astype(x.dtype)


def s5_mixer(xa, lam_re, lam_im, log_dt, b_re, b_im, c_re, c_im, d_skip, w_glu, b_glu):
    bsz, seq, _ = xa.shape
    f32 = jnp.float32
    u = xa.astype(f32).reshape(bsz, seq, SSM_GROUPS, SSM_GROUP)
    lam = lax.complex(lam_re.astype(f32), lam_im.astype(f32))
    dt = jnp.exp(log_dt.astype(f32))[:, None]
    lam_bar = jnp.exp(lam * dt)
    b_bar = ((lam_bar - 1.0) / lam)[..., None] * lax.complex(b_re.astype(f32), b_im.astype(f32))
    bu = jnp.einsum('gnp,bsgp->bsgn', b_bar, u.astype(jnp.complex64))
    a = jnp.broadcast_to(lam_bar, (seq,) + lam_bar.shape)[None]

    def combine(left, right):
        a_l, b_l = left
        a_r, b_r = right
        return a_l * a_r, a_r * b_l + b_r

    _, states = lax.associative_scan(combine, (a, bu), axis=1)
    c = lax.complex(c_re.astype(f32), c_im.astype(f32))
    y = jnp.einsum('gpn,bsgn->bsgp', c, states).real + d_skip.astype(f32) * u
    y = y.reshape(bsz, seq, SSM_WIDTH)
    z = jax.nn.gelu(y)
    z = z * jax.nn.sigmoid(z @ w_glu.astype(f32) + b_glu.astype(f32))
    return z.astype(xa.dtype)


def sgu_mixer(xb, ln_g, ln_b, w_s, b_s):
    bsz, seq, _ = xb.shape
    z = jax.nn.gelu(xb)
    u, v = z[..., :SGU_WIDTH], z[..., SGU_WIDTH:]
    v = layernorm(v, ln_g, ln_b)
    n_blk = seq // SGU_BLOCK
    v = v.reshape(bsz, n_blk, SGU_BLOCK, SGU_HEADS, SGU_HEAD_DIM)
    chunk_id = jnp.arange(SGU_BLOCK) // CHUNK
    mask = chunk_id[:, None] >= chunk_id[None, :]
    w = jnp.where(mask[None], w_s, jnp.zeros_like(w_s))
    s = jnp.einsum('hij,bnjhc->bnihc', w, v) + b_s.T[:, :, None]
    return u * s.reshape(bsz, seq, SGU_WIDTH)


def moe(xn, w_router, b_router, w_gate, b_gate, w_up, b_up, w_down, b_down):
    bsz, seq, d = xn.shape
    t = bsz * seq
    xt = xn.reshape(t, d)
    logits = xt.astype(jnp.float32) @ w_router.astype(jnp.float32) + b_router.astype(jnp.float32)
    top_val, top_idx = lax.top_k(logits, TOP_K)
    gates = jax.nn.softmax(top_val, axis=-1)

    n_slots = t * TOP_K
    n_blocks = -(-n_slots // MOE_BLOCK) + N_EXPERTS
    n_rows = n_blocks * MOE_BLOCK
    flat_e = top_idx.reshape(-1)
    order = jnp.argsort(flat_e)
    sorted_e = flat_e[order]
    slot_token = (order // TOP_K).astype(jnp.int32)
    counts = jnp.bincount(flat_e, length=N_EXPERTS)
    padded = (counts + MOE_BLOCK - 1) // MOE_BLOCK * MOE_BLOCK
    start = jnp.cumsum(counts) - counts
    pstart = jnp.cumsum(padded) - padded
    dest = pstart[sorted_e] + jnp.arange(n_slots) - start[sorted_e]
    row_token = jnp.full((n_rows,), t, jnp.int32).at[dest].set(slot_token)
    row_gate = jnp.zeros((n_rows,), jnp.float32).at[dest].set(gates.reshape(-1)[order])
    block_expert = jnp.minimum(
        jnp.searchsorted(jnp.cumsum(padded), jnp.arange(n_blocks) * MOE_BLOCK, side='right'),
        N_EXPERTS - 1)
    x_pad = jnp.concatenate([xt, jnp.zeros((1, d), xt.dtype)], axis=0)
    x_rows = x_pad[row_token].reshape(n_blocks, MOE_BLOCK, d)

    def expert_block(args):
        xb, e = args
        g = xb @ w_gate[e] + b_gate[e]
        l = xb @ w_up[e] + b_up[e]
        g = jnp.minimum(g, SWIGLU_LIMIT)
        l = jnp.clip(l, -SWIGLU_LIMIT, SWIGLU_LIMIT)
        h = g * jax.nn.sigmoid(SWIGLU_ALPHA * g) * (l + 1.0)
        return h @ w_down[e] + b_down[e]

    y_rows = lax.map(expert_block, (x_rows, block_expert)).reshape(n_rows, d)
    y_rows = y_rows * row_gate[:, None].astype(y_rows.dtype)
    y = jax.ops.segment_sum(y_rows, row_token, num_segments=t + 1)[:t]
    return y.reshape(bsz, seq, d)


def setup_inputs(seed: int = 0) -> dict:
    key = jax.random.key(seed)
    ks = jax.random.split(key, 32)
    f32 = jnp.float32
    L, D, G, N, P = DEPTH, D_MODEL, SSM_GROUPS, SSM_STATE, SSM_GROUP
    nrm = lambda k, shape, s: jax.random.normal(k, shape, f32) * s
    gain = lambda k, shape: 1.0 + 0.02 * jax.random.normal(k, shape, f32)
    n_idx = jnp.arange(N, dtype=f32)
    return {
        "x": jax.random.normal(ks[0], (BATCH, SEQ, D), f32),
        "norm_mix_g": gain(ks[1], (L, D)),
        "w_in": nrm(ks[2], (L, D, IN_WIDTH), D ** -0.5),
        "lam_re": -0.5 + nrm(ks[3], (L, G, N), 0.01),
        "lam_im": math.pi * n_idx + nrm(ks[4], (L, G, N), 0.01),
        "log_dt": jax.random.uniform(ks[5], (L, G), f32, math.log(DT_MIN), math.log(DT_MAX)),
        "b_re": nrm(ks[6], (L, G, N, P), (2 * P) ** -0.5),
        "b_im": nrm(ks[7], (L, G, N, P), (2 * P) ** -0.5),
        "c_re": nrm(ks[8], (L, G, P, N), N ** -0.5),
        "c_im": nrm(ks[9], (L, G, P, N), N ** -0.5),
        "d_skip": nrm(ks[10], (L, G, P), 1.0),
        "w_glu": nrm(ks[11], (L, SSM_WIDTH, SSM_WIDTH), SSM_WIDTH ** -0.5),
        "b_glu": nrm(ks[12], (L, SSM_WIDTH), 0.01),
        "sgu_ln_g": gain(ks[13], (L, SGU_WIDTH)),
        "sgu_ln_b": nrm(ks[14], (L, SGU_WIDTH), 0.01),
        "w_s": nrm(ks[15], (L, SGU_HEADS, SGU_BLOCK, SGU_BLOCK), SGU_BLOCK ** -0.5),
        "b_s": gain(ks[16], (L, SGU_HEADS, SGU_BLOCK)),
        "w_branch_a": nrm(ks[17], (L, SSM_WIDTH, D), SSM_WIDTH ** -0.5),
        "w_branch_b": nrm(ks[18], (L, SGU_WIDTH, D), SGU_WIDTH ** -0.5),
        "w_out": nrm(ks[19], (L, D, D), D ** -0.5),
        "norm_moe_g": gain(ks[20], (L, D)),
        "w_router": nrm(ks[21], (L, D, N_EXPERTS), D ** -0.5),
        "b_router": nrm(ks[22], (L, N_EXPERTS), 0.01),
        "w_gate": nrm(ks[23], (L, N_EXPERTS, D, D_EXPERT), D ** -0.5),
        "b_gate": nrm(ks[24], (L, N_EXPERTS, D_EXPERT), 0.01),
        "w_up": nrm(ks[25], (L, N_EXPERTS, D, D_EXPERT), D ** -0.5),
        "b_up": nrm(ks[26], (L, N_EXPERTS, D_EXPERT), 0.01),
        "w_down": nrm(ks[27], (L, N_EXPERTS, D_EXPERT, D), D_EXPERT ** -0.5),
        "b_down": nrm(ks[28], (L, N_EXPERTS, D), 0.01),
        "norm_final_g": gain(ks[29], (D,)),
    }


def reference(x, norm_mix_g, w_in, lam_re, lam_im, log_dt, b_re, b_im, c_re, c_im, d_skip,
              w_glu, b_glu, sgu_ln_g, sgu_ln_b, w_s, b_s, w_branch_a, w_branch_b, w_out,
              norm_moe_g, w_router, b_router, w_gate, b_gate, w_up, b_up, w_down, b_down,
              norm_final_g):
    h = x
    s_a = SSM_WIDTH
    s_b = s_a + 2 * SGU_WIDTH
    s_g = s_b + D_MODEL
    for layer in range(DEPTH):
        xn = rmsnorm(h, norm_mix_g[layer])
        proj = xn @ w_in[layer]
        xa, xb = proj[..., :s_a], proj[..., s_a:s_b]
        ga, gb = proj[..., s_b:s_g], proj[..., s_g:]
        ya = s5_mixer(xa, lam_re[layer], lam_im[layer], log_dt[layer], b_re[layer], b_im[layer],
                      c_re[layer], c_im[layer], d_skip[layer], w_glu[layer], b_glu[layer])
        yb = sgu_mixer(xb, sgu_ln_g[layer], sgu_ln_b[layer], w_s[layer], b_s[layer])
        merged = (jax.nn.sigmoid(ga) * (ya @ w_branch_a[layer])
                  + jax.nn.sigmoid(gb) * (yb @ w_branch_b[layer]))
        h = h + merged @ w_out[layer]
        h = h + moe(rmsnorm(h, norm_moe_g[layer]), w_router[layer], b_router[layer],
                    w_gate[layer], b_gate[layer], w_up[layer], b_up[layer],
                    w_down[layer], b_down[layer])
    return rmsnorm(h, norm_final_g)
```

```python
import functools
import math

import jax
import jax.numpy as jnp
from jax import lax
from jax.experimental import pallas as pl
from jax.experimental.pallas import tpu as pltpu

F32 = jnp.float32
BF16 = jnp.bfloat16

NORM_EPS = 1e-5
D_MODEL = 1024
SSM_WIDTH = 512
SSM_GROUP = 16
SSM_GROUPS = 32
SSM_STATE = 64
SGU_WIDTH = 512
SGU_BLOCK = 128
SGU_HEADS = 4
SGU_HEAD_DIM = 128
CHUNK = 64
N_EXPERTS = 32
TOP_K = 4
SWIGLU_ALPHA = 1.702
SWIGLU_LIMIT = 7.0

LANES = 128
SUBLANES = 8
TIME_TILE = 128
ROW_BLOCK = 512
VMEM_LIMIT = 56 * 1024 * 1024
NEG_BIG = -1e30


def _dot(a, b):
    return jnp.dot(a, b, preferred_element_type=F32)


def _sigmoid(x):
    return 1.0 / (1.0 + jnp.exp(-x))


def _gelu(x):
    return 0.5 * x * (1.0 + jnp.tanh(0.7978845608028654 * (x + 0.044715 * (x * x * x))))


def _rmsnorm(x, g):
    return x * lax.rsqrt(jnp.mean(x * x, axis=-1, keepdims=True) + NORM_EPS) * g


def _proj_kernel(x_ref, g_ref, w_ref, lng_ref, lnb_ref, ws_ref, bs_ref, wb_ref,
                 xa_ref, sga_ref, pb_ref, yb_scr):
    nb, tl, d = x_ref.shape
    m = nb * tl
    x = x_ref[...].reshape(m, d)
    xn = _rmsnorm(x, g_ref[...]).astype(BF16)

    xa = _dot(xn, w_ref[:, 0:SSM_WIDTH])
    xa_ref[...] = xa.astype(BF16).reshape(nb, tl, SSM_WIDTH)

    s_b = SSM_WIDTH + 2 * SGU_WIDTH
    s_g = s_b + D_MODEL
    ga = _dot(xn, w_ref[:, s_b:s_g])
    sga_ref[...] = _sigmoid(ga).astype(BF16).reshape(nb, tl, D_MODEL)

    z = _gelu(_dot(xn, w_ref[:, SSM_WIDTH:s_b]))
    u = z[:, :SGU_WIDTH]
    v = z[:, SGU_WIDTH:]
    mu = jnp.mean(v, axis=-1, keepdims=True)
    vc = v - mu
    v = vc * lax.rsqrt(jnp.mean(vc * vc, axis=-1, keepdims=True) + NORM_EPS)
    v = (v * lng_ref[...] + lnb_ref[...]).astype(BF16)

    ri = lax.broadcasted_iota(jnp.int32, (SGU_BLOCK, SGU_BLOCK), 0) // CHUNK
    ci = lax.broadcasted_iota(jnp.int32, (SGU_BLOCK, SGU_BLOCK), 1) // CHUNK
    causal = ri >= ci
    for h in range(SGU_HEADS):
        wm = jnp.where(causal, ws_ref[h], 0.0).astype(BF16)
        bias = bs_ref[:, h:h + 1]
        lo = h * SGU_HEAD_DIM
        for blk in range(m // SGU_BLOCK):
            r0 = blk * SGU_BLOCK
            s = _dot(wm, v[r0:r0 + SGU_BLOCK, lo:lo + SGU_HEAD_DIM]) + bias
            yb_scr[r0:r0 + SGU_BLOCK, lo:lo + SGU_HEAD_DIM] = (
                u[r0:r0 + SGU_BLOCK, lo:lo + SGU_HEAD_DIM] * s).astype(BF16)

    gb = _dot(xn, w_ref[:, s_g:])
    pb = _sigmoid(gb) * _dot(yb_scr[...], wb_ref[...])
    pb_ref[...] = pb.astype(BF16).reshape(nb, tl, D_MODEL)


def _proj_call(x, g, w_in, ln_g, ln_b, w_s, b_s_t, w_b):
    nb, seq, d = x.shape
    tl = TIME_TILE
    m = nb * tl
    const = lambda shape: pl.BlockSpec(shape, lambda i: (0,) * len(shape))
    tile = lambda width: pl.BlockSpec((nb, tl, width), lambda i: (0, i, 0))
    return pl.pallas_call(
        _proj_kernel,
        grid=(seq // tl,),
        in_specs=[tile(d), const((1, d)), const(w_in.shape), const((1, SGU_WIDTH)),
                  const((1, SGU_WIDTH)), const(w_s.shape), const(b_s_t.shape), const(w_b.shape)],
        out_specs=[tile(SSM_WIDTH), tile(D_MODEL), tile(D_MODEL)],
        out_shape=[jax.ShapeDtypeStruct((nb, seq, SSM_WIDTH), BF16),
                   jax.ShapeDtypeStruct((nb, seq, D_MODEL), BF16),
                   jax.ShapeDtypeStruct((nb, seq, D_MODEL), BF16)],
        scratch_shapes=[pltpu.VMEM((m, SGU_WIDTH), BF16)],
        compiler_params=pltpu.CompilerParams(
            dimension_semantics=("arbitrary",), vmem_limit_bytes=VMEM_LIMIT),
        name="proj",
    )(x, g, w_in, ln_g, ln_b, w_s, b_s_t, w_b)


def _mix_kernel(xa_ref, sga_ref, pb_ref, x_ref, perm_ref, permt_ref, bd_ref, cd_ref,
                are_ref, aim_ref, dskip_ref, wglu_ref, bglu_ref, wa_ref, wout_ref,
                g2_ref, wr_ref, br_ref,
                h_ref, xn2_ref, route_ref, cnt_ref,
                bu_scr, state_scr, cnt_scr):
    nb, tl, d = x_ref.shape
    m = nb * tl
    rows8 = SUBLANES * tl

    @pl.when(pl.program_id(0) == 0)
    def _():
        state_scr[...] = jnp.zeros_like(state_scr)
        cnt_scr[...] = jnp.zeros_like(cnt_scr)

    xa = xa_ref[...].reshape(m, SSM_WIDTH)
    gathered = _dot(perm_ref[...], xa)
    first_half = (lax.broadcasted_iota(jnp.int32, (rows8, 1), 0) % SUBLANES) < nb

    y_parts = []
    for p in range(2):
        lo_a = LANES * p
        lo_b = SSM_WIDTH // 2 + LANES * p
        x8 = jnp.concatenate(
            [jnp.where(first_half, gathered[:, lo_a:lo_a + LANES], 0.0),
             jnp.where(first_half, 0.0, gathered[:, lo_b:lo_b + LANES])], axis=1).astype(BF16)
        bu_scr[...] = _dot(x8, bd_ref[p])

        a_re = are_ref[p]
        a_im = aim_ref[p]

        def step(t, carry, a_re=a_re, a_im=a_im):
            s_re, s_im = carry
            r = pl.multiple_of(t * SUBLANES, SUBLANES)
            b_re = bu_scr[pl.ds(r, SUBLANES), 0:SSM_WIDTH]
            b_im = bu_scr[pl.ds(r, SUBLANES), SSM_WIDTH:2 * SSM_WIDTH]
            n_re = a_re * s_re - a_im * s_im + b_re
            n_im = a_re * s_im + a_im * s_re + b_im
            bu_scr[pl.ds(r, SUBLANES), 0:SSM_WIDTH] = n_re
            bu_scr[pl.ds(r, SUBLANES), SSM_WIDTH:2 * SSM_WIDTH] = n_im
            return n_re, n_im

        st = state_scr[p]
        s_re, s_im = lax.fori_loop(0, tl, step, (st[:, :SSM_WIDTH], st[:, SSM_WIDTH:]), unroll=8)
        state_scr[p] = jnp.concatenate([s_re, s_im], axis=1)

        y8 = _dot(bu_scr[...].astype(BF16), cd_ref[p])
        y8m = jnp.concatenate(
            [jnp.where(first_half, y8[:, :LANES], 0.0),
             jnp.where(first_half, 0.0, y8[:, LANES:])], axis=1).astype(BF16)
        y_parts.append(_dot(permt_ref[...], y8m))

    y = jnp.concatenate([y_parts[0][:, :LANES], y_parts[1][:, :LANES],
                         y_parts[0][:, LANES:], y_parts[1][:, LANES:]], axis=1)
    y = y + dskip_ref[...] * xa.astype(F32)
    z = _gelu(y)
    ya = z * _sigmoid(_dot(z.astype(BF16), wglu_ref[...]) + bglu_ref[...])
    pa = sga_ref[...].reshape(m, d).astype(F32) * _dot(ya.astype(BF16), wa_ref[...])
    merged = pa + pb_ref[...].reshape(m, d).astype(F32)
    h = x_ref[...].reshape(m, d) + _dot(merged.astype(BF16), wout_ref[...])
    h_ref[...] = h
    xn2 = _rmsnorm(h, g2_ref[...])
    xn2_ref[...] = xn2

    logits = _dot(xn2.astype(BF16), wr_ref[...]) + br_ref[...]
    lane = lax.broadcasted_iota(jnp.int32, (m, LANES), 1).astype(F32)
    work = logits
    vals, idxs = [], []
    member = jnp.zeros((m, LANES), F32)
    for _ in range(TOP_K):
        mx = jnp.max(work, axis=-1, keepdims=True)
        ix = jnp.min(jnp.where(work == mx, lane, float(LANES)), axis=-1, keepdims=True)
        hit = lane == ix
        member = jnp.where(hit, 1.0, member)
        work = jnp.where(hit, NEG_BIG, work)
        vals.append(mx)
        idxs.append(ix)
    exps = [jnp.exp(v - vals[0]) for v in vals]
    denom = exps[0] + exps[1] + exps[2] + exps[3]
    gates = [e / denom for e in exps]

    ri = lax.broadcasted_iota(jnp.int32, (m, m), 0)
    ci = lax.broadcasted_iota(jnp.int32, (m, m), 1)
    lower = jnp.where(ri > ci, 1.0, 0.0).astype(BF16)
    before = _dot(lower, member.astype(BF16)) + cnt_scr[0:1, :]
    new_cnt = cnt_scr[0:1, :] + jnp.sum(member, axis=0, keepdims=True)
    cnt_scr[...] = jnp.broadcast_to(new_cnt, cnt_scr.shape)
    cnt_ref[...] = jnp.broadcast_to(new_cnt, cnt_ref.shape)

    route = jnp.zeros((m, LANES), F32)
    for k in range(TOP_K):
        rank_k = jnp.sum(jnp.where(lane == idxs[k], before, 0.0), axis=-1, keepdims=True)
        route = jnp.where(lane == float(k), idxs[k], route)
        route = jnp.where(lane == float(TOP_K + k), gates[k], route)
        route = jnp.where(lane == float(2 * TOP_K + k), rank_k, route)
    route_ref[...] = route


def _mix_call(xa, sga, pb, x, perm, permt, bd, cd, a_re, a_im, dskip, w_glu, b_glu, w_a, w_out,
              g2, w_r, b_r):
    nb, seq, d = x.shape
    tl = TIME_TILE
    m = nb * tl
    n_tiles = seq // tl
    const = lambda shape: pl.BlockSpec(shape, lambda i: (0,) * len(shape))
    tile = lambda width: pl.BlockSpec((nb, tl, width), lambda i: (0, i, 0))
    rows = lambda width: pl.BlockSpec((m, width), lambda i: (i, 0))
    operands = (xa, sga, pb, x, perm, permt, bd, cd, a_re, a_im, dskip, w_glu, b_glu, w_a, w_out,
                g2, w_r, b_r)
    in_specs = [tile(SSM_WIDTH), tile(d), tile(d), tile(d)] + [const(o.shape) for o in operands[4:]]
    return pl.pallas_call(
        _mix_kernel,
        grid=(n_tiles,),
        in_specs=in_specs,
        out_specs=[rows(d), rows(d), rows(LANES), const((SUBLANES, LANES))],
        out_shape=[jax.ShapeDtypeStruct((n_tiles * m, d), F32),
                   jax.ShapeDtypeStruct((n_tiles * m, d), F32),
                   jax.ShapeDtypeStruct((n_tiles * m, LANES), F32),
                   jax.ShapeDtypeStruct((SUBLANES, LANES), F32)],
        scratch_shapes=[pltpu.VMEM((SUBLANES * tl, 2 * SSM_WIDTH), F32),
                        pltpu.VMEM((2, SUBLANES, 2 * SSM_WIDTH), F32),
                        pltpu.VMEM((SUBLANES, LANES), F32)],
        compiler_params=pltpu.CompilerParams(
            dimension_semantics=("arbitrary",), vmem_limit_bytes=VMEM_LIMIT),
        name="mix",
    )(*operands)


def _scatter_kernel(dest_ref, x_ref, init_ref, rows_ref, sem):
    del init_ref
    m = x_ref.shape[0]

    def copy(r, k):
        return pltpu.make_async_copy(
            x_ref.at[pl.ds(r, 1)], rows_ref.at[pl.ds(dest_ref[r * TOP_K + k], 1)], sem)

    def issue(r, c):
        for k in range(TOP_K):
            copy(r, k).start()
        return c

    def drain(r, c):
        for k in range(TOP_K):
            copy(r, k).wait()
        return c

    lax.fori_loop(0, m, issue, 0)
    lax.fori_loop(0, m, drain, 0)


def _scatter_call(dest, xn2, rows_init):
    t, d = xn2.shape
    m = ROW_BLOCK
    return pl.pallas_call(
        _scatter_kernel,
        grid=(t // m,),
        in_specs=[pl.BlockSpec((m * TOP_K,), lambda i: (i,), memory_space=pltpu.SMEM),
                  pl.BlockSpec((m, d), lambda i: (i, 0)),
                  pl.BlockSpec(memory_space=pl.ANY)],
        out_specs=pl.BlockSpec(memory_space=pl.ANY),
        out_shape=jax.ShapeDtypeStruct(rows_init.shape, rows_init.dtype),
        scratch_shapes=[pltpu.SemaphoreType.DMA],
        input_output_aliases={2: 0},
        compiler_params=pltpu.CompilerParams(
            dimension_semantics=("arbitrary",), vmem_limit_bytes=VMEM_LIMIT),
        name="scatter",
    )(dest, xn2, rows_init)


def _expert_kernel(be_ref, x_ref, wg_ref, bg_ref, wu_ref, bu_ref, wd_ref, bd_ref, y_ref,
                   wg_scr, wu_scr, wd_scr):
    i = pl.program_id(0)
    changed = jnp.logical_or(i == 0, be_ref[i] != be_ref[jnp.maximum(i - 1, 0)])

    @pl.when(changed)
    def _():
        wg_scr[...] = wg_ref[...].astype(BF16)
        wu_scr[...] = wu_ref[...].astype(BF16)
        wd_scr[...] = wd_ref[...].astype(BF16)

    x = x_ref[...].astype(BF16)
    g = _dot(x, wg_scr[...]) + bg_ref[...]
    l = _dot(x, wu_scr[...]) + bu_ref[...]
    g = jnp.minimum(g, SWIGLU_LIMIT)
    l = jnp.clip(l, -SWIGLU_LIMIT, SWIGLU_LIMIT)
    hid = g * _sigmoid(SWIGLU_ALPHA * g) * (l + 1.0)
    y_ref[...] = _dot(hid.astype(BF16), wd_scr[...]) + bd_ref[...]


def _expert_call(block_expert, x_rows, w_gate, b_gate, w_up, b_up, w_down, b_down):
    n_rows, d = x_rows.shape
    f = w_gate.shape[-1]
    wspec = lambda shape: pl.BlockSpec((None,) + shape, lambda i, be: (be[i], 0, 0))
    grid_spec = pltpu.PrefetchScalarGridSpec(
        num_scalar_prefetch=1,
        grid=(n_rows // ROW_BLOCK,),
        in_specs=[pl.BlockSpec((ROW_BLOCK, d), lambda i, be: (i, 0)),
                  wspec((d, f)), wspec((1, f)), wspec((d, f)), wspec((1, f)),
                  wspec((f, d)), wspec((1, d))],
        out_specs=pl.BlockSpec((ROW_BLOCK, d), lambda i, be: (i, 0)),
        scratch_shapes=[pltpu.VMEM((d, f), BF16), pltpu.VMEM((d, f), BF16),
                        pltpu.VMEM((f, d), BF16)],
    )
    return pl.pallas_call(
        _expert_kernel,
        grid_spec=grid_spec,
        out_shape=jax.ShapeDtypeStruct((n_rows, d), F32),
        compiler_params=pltpu.CompilerParams(
            dimension_semantics=("arbitrary",), vmem_limit_bytes=VMEM_LIMIT),
        name="expert",
    )(block_expert, x_rows, w_gate, b_gate, w_up, b_up, w_down, b_down)


def _combine_kernel(dest_ref, h_ref, route_ref, gf_ref, yrows_ref, out_ref, ybuf, sem):
    nb, tl, d = out_ref.shape
    m = nb * tl

    def copy(r, k):
        return pltpu.make_async_copy(
            yrows_ref.at[pl.ds(dest_ref[r * TOP_K + k], 1)], ybuf.at[k, pl.ds(r, 1)], sem)

    def issue(r, c):
        for k in range(TOP_K):
            copy(r, k).start()
        return c

    def drain(r, c):
        for k in range(TOP_K):
            copy(r, k).wait()
        return c

    lax.fori_loop(0, m, issue, 0)
    lax.fori_loop(0, m, drain, 0)

    acc = h_ref[...]
    for k in range(TOP_K):
        acc = acc + route_ref[:, TOP_K + k:TOP_K + k + 1] * ybuf[k]
    out_ref[...] = _rmsnorm(acc, gf_ref[...]).reshape(nb, tl, d)


def _combine_call(dest, h, route, g_final, y_rows, nb, seq):
    t, d = h.shape
    tl = TIME_TILE
    m = nb * tl
    return pl.pallas_call(
        _combine_kernel,
        grid=(t // m,),
        in_specs=[pl.BlockSpec((m * TOP_K,), lambda i: (i,), memory_space=pltpu.SMEM),
                  pl.BlockSpec((m, d), lambda i: (i, 0)),
                  pl.BlockSpec((m, LANES), lambda i: (i, 0)),
                  pl.BlockSpec((1, d), lambda i: (0, 0)),
                  pl.BlockSpec(memory_space=pl.ANY)],
        out_specs=pl.BlockSpec((nb, tl, d), lambda i: (0, i, 0)),
        out_shape=jax.ShapeDtypeStruct((nb, seq, d), F32),
        scratch_shapes=[pltpu.VMEM((TOP_K, m, d), F32), pltpu.SemaphoreType.DMA],
        compiler_params=pltpu.CompilerParams(
            dimension_semantics=("arbitrary",), vmem_limit_bytes=VMEM_LIMIT),
        name="combine",
    )(dest, h, route, g_final, y_rows)


def _s5_operands(lam_re, lam_im, log_dt, b_re, b_im, c_re, c_im, nb, tl):
    gp, ns, pg = SSM_GROUPS, SSM_STATE, SSM_GROUP
    lam = lax.complex(lam_re.astype(F32), lam_im.astype(F32))
    dt = jnp.exp(log_dt.astype(F32))[:, None]
    lam_bar = jnp.exp(lam * dt)
    b_bar = ((lam_bar - 1.0) / lam)[..., None] * lax.complex(b_re.astype(F32), b_im.astype(F32))
    grp = lambda hf, p: slice(16 * hf + 8 * p, 16 * hf + 8 * p + 8)
    eye8 = jnp.eye(8, dtype=F32)

    def b_block(bb):
        return jnp.einsum('gnq,gh->gqhn', bb, eye8).reshape(8 * pg, 8 * ns)

    def c_block(cc):
        return jnp.einsum('gqn,gh->gnhq', cc, eye8).reshape(8 * ns, 8 * pg)

    bd, cd, a_re, a_im = [], [], [], []
    for p in range(2):
        rows = []
        for hf in range(2):
            bb = b_bar[grp(hf, p)]
            rows.append(jnp.concatenate([b_block(bb.real), b_block(bb.imag)], axis=1))
        bd.append(jnp.concatenate(rows, axis=0))
        cols = []
        for hf in range(2):
            cols.append(jnp.concatenate([c_block(c_re[grp(hf, p)].astype(F32)),
                                         -c_block(c_im[grp(hf, p)].astype(F32))], axis=0))
        cd.append(jnp.concatenate(cols, axis=1))
        lam_rows = jnp.stack([lam_bar[grp(j // nb, p)].reshape(8 * ns) for j in range(SUBLANES)])
        a_re.append(lam_rows.real)
        a_im.append(lam_rows.imag)
    bd = jnp.stack(bd).astype(BF16)
    cd = jnp.stack(cd).astype(BF16)
    a_re = jnp.stack(a_re)
    a_im = jnp.stack(a_im)

    r8 = jnp.arange(SUBLANES * tl)
    src = (r8 % nb) * tl + r8 // SUBLANES
    perm = (src[:, None] == jnp.arange(nb * tl)[None, :]).astype(BF16)
    return perm, perm.T, bd, cd, a_re, a_im


def kernel(x, norm_mix_g, w_in, lam_re, lam_im, log_dt, b_re, b_im, c_re, c_im, d_skip, w_glu, b_glu, sgu_ln_g, sgu_ln_b, w_s, b_s, w_branch_a, w_branch_b, w_out, norm_moe_g, w_router, b_router, w_gate, b_gate, w_up, b_up, w_down, b_down, norm_final_g):
    nb, seq, d = x.shape
    assert d == D_MODEL and SUBLANES % nb == 0 and SUBLANES // nb == 2
    assert seq % TIME_TILE == 0 and norm_mix_g.shape[0] == 1
    tl = TIME_TILE
    t = nb * seq
    row = lambda v: v.reshape(1, -1).astype(F32)

    xa, sga, pb = _proj_call(
        x, row(norm_mix_g[0]), w_in[0].astype(BF16), row(sgu_ln_g[0]), row(sgu_ln_b[0]),
        w_s[0].astype(F32), b_s[0].T.astype(F32), w_branch_b[0].astype(BF16))

    perm, permt, bd, cd, a_re, a_im = _s5_operands(
        lam_re[0], lam_im[0], log_dt[0], b_re[0], b_im[0], c_re[0], c_im[0], nb, tl)
    w_r = jnp.zeros((d, LANES), F32).at[:, :N_EXPERTS].set(w_router[0].astype(F32)).astype(BF16)
    b_r = jnp.full((1, LANES), NEG_BIG, F32).at[0, :N_EXPERTS].set(b_router[0].astype(F32))
    h, xn2, route, cnt = _mix_call(
        xa, sga, pb, x, perm, permt, bd, cd, a_re, a_im, row(d_skip[0]),
        w_glu[0].astype(BF16), row(b_glu[0]), w_branch_a[0].astype(BF16), w_out[0].astype(BF16),
        row(norm_moe_g[0]), w_r, b_r)

    idx = route[:, 0:TOP_K].astype(jnp.int32)
    rank = route[:, 2 * TOP_K:3 * TOP_K].astype(jnp.int32)
    counts = cnt[0, :N_EXPERTS].astype(jnp.int32)
    padded = (counts + ROW_BLOCK - 1) // ROW_BLOCK * ROW_BLOCK
    cum = jnp.cumsum(padded)
    pstart = cum - padded
    dest = (pstart[idx] + rank).reshape(-1)
    n_blocks = (t * TOP_K) // ROW_BLOCK + N_EXPERTS
    block_expert = jnp.minimum(
        jnp.searchsorted(cum, jnp.arange(n_blocks, dtype=jnp.int32) * ROW_BLOCK, side='right'),
        N_EXPERTS - 1).astype(jnp.int32)

    x_rows = _scatter_call(dest, xn2, jnp.zeros((n_blocks * ROW_BLOCK, d), F32))
    y_rows = _expert_call(
        block_expert, x_rows,
        w_gate[0], b_gate[0][:, None, :], w_up[0], b_up[0][:, None, :],
        w_down[0], b_down[0][:, None, :])
    return _combine_call(dest, h, route, row(norm_final_g), y_rows, nb, seq)
```

```python
import jax
import jax.numpy as jnp
from jax import lax
from jax.experimental import pallas as pl
from jax.experimental.pallas import tpu as pltpu
from jax.experimental.pallas import tpu_sc as plsc

F32 = jnp.float32
BF16 = jnp.bfloat16

NORM_EPS = 1e-5
D_MODEL = 1024
SSM_WIDTH = 512
SSM_GROUP = 16
SSM_GROUPS = 32
SSM_STATE = 64
SGU_WIDTH = 512
SGU_BLOCK = 128
SGU_HEADS = 4
SGU_HEAD_DIM = 128
CHUNK = 64
N_EXPERTS = 32
TOP_K = 4
SWIGLU_ALPHA = 1.702
SWIGLU_LIMIT = 7.0

LANES = 128
SUBLANES = 8
TIME_TILE = 128
ROW_BLOCK = 512
VMEM_LIMIT = 56 * 1024 * 1024
NEG_BIG = -1e30
SC_CORES = 2
SC_SUBCORES = 16
SC_WORKERS = SC_CORES * SC_SUBCORES
SUBROWS = 4
PACKED = D_MODEL // 2


def _pack_rows(x, out_ref):
    m = x.shape[0]
    lo = lax.bitcast_convert_type(x[:, :PACKED].astype(BF16).astype(F32), jnp.int32)
    hi = lax.bitcast_convert_type(x[:, PACKED:].astype(BF16).astype(F32), jnp.int32)
    words = lax.shift_right_logical(lo, 16) | (hi & jnp.int32(-65536))
    for s in range(SUBROWS):
        out_ref[pl.ds(s, m, stride=SUBROWS), :] = words[:, s * LANES:(s + 1) * LANES]


def _unpack_rows(ref, m):
    lo, hi = [], []
    for s in range(SUBROWS):
        w = ref[pl.ds(s, m, stride=SUBROWS), :]
        lo.append(lax.bitcast_convert_type(lax.shift_left(w, 16), F32))
        hi.append(lax.bitcast_convert_type(w & jnp.int32(-65536), F32))
    return lo, hi


def _dot(a, b):
    return jnp.dot(a, b, preferred_element_type=F32)


def _sigmoid(x):
    return 1.0 / (1.0 + jnp.exp(-x))


def _gelu(x):
    return 0.5 * x * (1.0 + jnp.tanh(0.7978845608028654 * (x + 0.044715 * (x * x * x))))


def _rmsnorm(x, g):
    return x * lax.rsqrt(jnp.mean(x * x, axis=-1, keepdims=True) + NORM_EPS) * g


def _proj_kernel(x_ref, g_ref, w_ref, lng_ref, lnb_ref, ws_ref, bs_ref, wb_ref,
                 xa_ref, sga_ref, pb_ref, yb_scr):
    nb, tl, d = x_ref.shape
    m = nb * tl
    x = x_ref[...].reshape(m, d)
    xn = _rmsnorm(x, g_ref[...]).astype(BF16)

    xa = _dot(xn, w_ref[:, 0:SSM_WIDTH])
    xa_ref[...] = xa.astype(BF16).reshape(nb, tl, SSM_WIDTH)

    s_b = SSM_WIDTH + 2 * SGU_WIDTH
    s_g = s_b + D_MODEL
    ga = _dot(xn, w_ref[:, s_b:s_g])
    sga_ref[...] = _sigmoid(ga).astype(BF16).reshape(nb, tl, D_MODEL)

    z = _gelu(_dot(xn, w_ref[:, SSM_WIDTH:s_b]))
    u = z[:, :SGU_WIDTH]
    v = z[:, SGU_WIDTH:]
    mu = jnp.mean(v, axis=-1, keepdims=True)
    vc = v - mu
    v = vc * lax.rsqrt(jnp.mean(vc * vc, axis=-1, keepdims=True) + NORM_EPS)
    v = (v * lng_ref[...] + lnb_ref[...]).astype(BF16)

    ri = lax.broadcasted_iota(jnp.int32, (SGU_BLOCK, SGU_BLOCK), 0) // CHUNK
    ci = lax.broadcasted_iota(jnp.int32, (SGU_BLOCK, SGU_BLOCK), 1) // CHUNK
    causal = ri >= ci
    for h in range(SGU_HEADS):
        wm = jnp.where(causal, ws_ref[h], 0.0).astype(BF16)
        bias = bs_ref[:, h:h + 1]
        lo = h * SGU_HEAD_DIM
        for blk in range(m // SGU_BLOCK):
            r0 = blk * SGU_BLOCK
            s = _dot(wm, v[r0:r0 + SGU_BLOCK, lo:lo + SGU_HEAD_DIM]) + bias
            yb_scr[r0:r0 + SGU_BLOCK, lo:lo + SGU_HEAD_DIM] = (
                u[r0:r0 + SGU_BLOCK, lo:lo + SGU_HEAD_DIM] * s).astype(BF16)

    gb = _dot(xn, w_ref[:, s_g:])
    pb = _sigmoid(gb) * _dot(yb_scr[...], wb_ref[...])
    pb_ref[...] = pb.astype(BF16).reshape(nb, tl, D_MODEL)


def _proj_call(x, g, w_in, ln_g, ln_b, w_s, b_s_t, w_b):
    nb, seq, d = x.shape
    tl = TIME_TILE
    m = nb * tl
    const = lambda shape: pl.BlockSpec(shape, lambda i: (0,) * len(shape))
    tile = lambda width: pl.BlockSpec((nb, tl, width), lambda i: (0, i, 0))
    return pl.pallas_call(
        _proj_kernel,
        grid=(seq // tl,),
        in_specs=[tile(d), const((1, d)), const(w_in.shape), const((1, SGU_WIDTH)),
                  const((1, SGU_WIDTH)), const(w_s.shape), const(b_s_t.shape), const(w_b.shape)],
        out_specs=[tile(SSM_WIDTH), tile(D_MODEL), tile(D_MODEL)],
        out_shape=[jax.ShapeDtypeStruct((nb, seq, SSM_WIDTH), BF16),
                   jax.ShapeDtypeStruct((nb, seq, D_MODEL), BF16),
                   jax.ShapeDtypeStruct((nb, seq, D_MODEL), BF16)],
        scratch_shapes=[pltpu.VMEM((m, SGU_WIDTH), BF16)],
        compiler_params=pltpu.CompilerParams(
            dimension_semantics=("arbitrary",), vmem_limit_bytes=VMEM_LIMIT),
        name="proj",
    )(x, g, w_in, ln_g, ln_b, w_s, b_s_t, w_b)


def _mix_kernel(xa_ref, sga_ref, pb_ref, x_ref, perm_ref, permt_ref, bd_ref, cd_ref,
                are_ref, aim_ref, dskip_ref, wglu_ref, bglu_ref, wa_ref, wout_ref,
                g2_ref, wr_ref, br_ref,
                h_ref, xn2_ref, route_ref, cnt_ref,
                bu_scr, state_scr, cnt_scr):
    nb, tl, d = x_ref.shape
    m = nb * tl
    rows8 = SUBLANES * tl

    @pl.when(pl.program_id(0) == 0)
    def _():
        state_scr[...] = jnp.zeros_like(state_scr)
        cnt_scr[...] = jnp.zeros_like(cnt_scr)

    xa = xa_ref[...].reshape(m, SSM_WIDTH)
    gathered = _dot(perm_ref[...], xa)
    first_half = (lax.broadcasted_iota(jnp.int32, (rows8, 1), 0) % SUBLANES) < nb

    y_parts = []
    for p in range(2):
        lo_a = LANES * p
        lo_b = SSM_WIDTH // 2 + LANES * p
        x8 = jnp.concatenate(
            [jnp.where(first_half, gathered[:, lo_a:lo_a + LANES], 0.0),
             jnp.where(first_half, 0.0, gathered[:, lo_b:lo_b + LANES])], axis=1).astype(BF16)
        bu_scr[...] = _dot(x8, bd_ref[p])

        a_re = are_ref[p]
        a_im = aim_ref[p]

        def step(t, carry, a_re=a_re, a_im=a_im):
            s_re, s_im = carry
            r = pl.multiple_of(t * SUBLANES, SUBLANES)
            b_re = bu_scr[pl.ds(r, SUBLANES), 0:SSM_WIDTH]
            b_im = bu_scr[pl.ds(r, SUBLANES), SSM_WIDTH:2 * SSM_WIDTH]
            n_re = a_re * s_re - a_im * s_im + b_re
            n_im = a_re * s_im + a_im * s_re + b_im
            bu_scr[pl.ds(r, SUBLANES), 0:SSM_WIDTH] = n_re
            bu_scr[pl.ds(r, SUBLANES), SSM_WIDTH:2 * SSM_WIDTH] = n_im
            return n_re, n_im

        st = state_scr[p]
        s_re, s_im = lax.fori_loop(0, tl, step, (st[:, :SSM_WIDTH], st[:, SSM_WIDTH:]), unroll=8)
        state_scr[p] = jnp.concatenate([s_re, s_im], axis=1)

        y8 = _dot(bu_scr[...].astype(BF16), cd_ref[p])
        y8m = jnp.concatenate(
            [jnp.where(first_half, y8[:, :LANES], 0.0),
             jnp.where(first_half, 0.0, y8[:, LANES:])], axis=1).astype(BF16)
        y_parts.append(_dot(permt_ref[...], y8m))

    y = jnp.concatenate([y_parts[0][:, :LANES], y_parts[1][:, :LANES],
                         y_parts[0][:, LANES:], y_parts[1][:, LANES:]], axis=1)
    y = y + dskip_ref[...] * xa.astype(F32)
    z = _gelu(y)
    ya = z * _sigmoid(_dot(z.astype(BF16), wglu_ref[...]) + bglu_ref[...])
    pa = sga_ref[...].reshape(m, d).astype(F32) * _dot(ya.astype(BF16), wa_ref[...])
    merged = pa + pb_ref[...].reshape(m, d).astype(F32)
    h = x_ref[...].reshape(m, d) + _dot(merged.astype(BF16), wout_ref[...])
    h_ref[...] = h
    xn2 = _rmsnorm(h, g2_ref[...])
    _pack_rows(xn2, xn2_ref)

    logits = _dot(xn2.astype(BF16), wr_ref[...]) + br_ref[...]
    lane = lax.broadcasted_iota(jnp.int32, (m, LANES), 1).astype(F32)
    work = logits
    vals, idxs = [], []
    member = jnp.zeros((m, LANES), F32)
    for _ in range(TOP_K):
        mx = jnp.max(work, axis=-1, keepdims=True)
        ix = jnp.min(jnp.where(work == mx, lane, float(LANES)), axis=-1, keepdims=True)
        hit = lane == ix
        member = jnp.where(hit, 1.0, member)
        work = jnp.where(hit, NEG_BIG, work)
        vals.append(mx)
        idxs.append(ix)
    exps = [jnp.exp(v - vals[0]) for v in vals]
    denom = exps[0] + exps[1] + exps[2] + exps[3]
    gates = [e / denom for e in exps]

    ri = lax.broadcasted_iota(jnp.int32, (m, m), 0)
    ci = lax.broadcasted_iota(jnp.int32, (m, m), 1)
    lower = jnp.where(ri > ci, 1.0, 0.0).astype(BF16)
    before = _dot(lower, member.astype(BF16)) + cnt_scr[0:1, :]
    new_cnt = cnt_scr[0:1, :] + jnp.sum(member, axis=0, keepdims=True)
    cnt_scr[...] = jnp.broadcast_to(new_cnt, cnt_scr.shape)
    cnt_ref[...] = jnp.broadcast_to(new_cnt, cnt_ref.shape)

    route = jnp.zeros((m, LANES), F32)
    for k in range(TOP_K):
        rank_k = jnp.sum(jnp.where(lane == idxs[k], before, 0.0), axis=-1, keepdims=True)
        route = jnp.where(lane == float(k), idxs[k], route)
        route = jnp.where(lane == float(TOP_K + k), gates[k], route)
        route = jnp.where(lane == float(2 * TOP_K + k), rank_k, route)
    route_ref[...] = route


def _mix_call(xa, sga, pb, x, perm, permt, bd, cd, a_re, a_im, dskip, w_glu, b_glu, w_a, w_out,
              g2, w_r, b_r):
    nb, seq, d = x.shape
    tl = TIME_TILE
    m = nb * tl
    n_tiles = seq // tl
    const = lambda shape: pl.BlockSpec(shape, lambda i: (0,) * len(shape))
    tile = lambda width: pl.BlockSpec((nb, tl, width), lambda i: (0, i, 0))
    rows = lambda width: pl.BlockSpec((m, width), lambda i: (i, 0))
    operands = (xa, sga, pb, x, perm, permt, bd, cd, a_re, a_im, dskip, w_glu, b_glu, w_a, w_out,
                g2, w_r, b_r)
    in_specs = [tile(SSM_WIDTH), tile(d), tile(d), tile(d)] + [const(o.shape) for o in operands[4:]]
    return pl.pallas_call(
        _mix_kernel,
        grid=(n_tiles,),
        in_specs=in_specs,
        out_specs=[rows(d), pl.BlockSpec((SUBROWS * m, LANES), lambda i: (i, 0)), rows(LANES),
                   const((SUBLANES, LANES))],
        out_shape=[jax.ShapeDtypeStruct((n_tiles * m, d), F32),
                   jax.ShapeDtypeStruct((n_tiles * m * SUBROWS, LANES), jnp.int32),
                   jax.ShapeDtypeStruct((n_tiles * m, LANES), F32),
                   jax.ShapeDtypeStruct((SUBLANES, LANES), F32)],
        scratch_shapes=[pltpu.VMEM((SUBLANES * tl, 2 * SSM_WIDTH), F32),
                        pltpu.VMEM((2, SUBLANES, 2 * SSM_WIDTH), F32),
                        pltpu.VMEM((SUBLANES, LANES), F32)],
        compiler_params=pltpu.CompilerParams(
            dimension_semantics=("arbitrary",), vmem_limit_bytes=VMEM_LIMIT),
        name="mix",
    )(*operands)


def _sc_mesh():
    return plsc.VectorSubcoreMesh(core_axis_name="c", subcore_axis_name="s")


def _worker_id():
    return lax.axis_index("s") * SC_CORES + lax.axis_index("c")


def _dispatch_body(src_hbm, idx_hbm, out_hbm, idx_v, buf, sem):
    n_chunks = idx_v.shape[0] // TOP_K
    wid = _worker_id()
    base = wid * (n_chunks * LANES)
    pltpu.sync_copy(idx_hbm.at[wid], idx_v)

    @pl.loop(0, n_chunks)
    def _(j):
        pltpu.sync_copy(src_hbm.at[pl.ds(base + j * LANES, LANES)], buf)
        copies = [pltpu.async_copy(buf, out_hbm.at[idx_v.at[j * TOP_K + k]], sem)
                  for k in range(TOP_K)]
        for c in copies:
            c.wait()


def _sc_dispatch(src, idx, n_out):
    n_chunks = src.shape[0] // (SC_WORKERS * LANES)
    return pl.kernel(
        _dispatch_body,
        out_type=jax.ShapeDtypeStruct((n_out, LANES), jnp.int32),
        mesh=_sc_mesh(),
        scratch_types=[pltpu.VMEM((n_chunks * TOP_K, LANES), jnp.int32),
                       pltpu.VMEM((LANES, LANES), jnp.int32),
                       pltpu.SemaphoreType.DMA],
        name="sc_dispatch",
    )(src, idx)


def _gather_body(tab_hbm, idx_hbm, out_hbm, idx_v, buf0, buf1, sem0, sem1):
    n_chunks = idx_v.shape[0]
    wid = _worker_id()
    base = wid * (n_chunks * LANES)
    pltpu.sync_copy(idx_hbm.at[wid], idx_v)

    @pl.loop(0, n_chunks, step=2)
    def _(j):
        c0 = pltpu.async_copy(tab_hbm.at[idx_v.at[j]], buf0, sem0)
        c1 = pltpu.async_copy(tab_hbm.at[idx_v.at[j + 1]], buf1, sem1)
        c0.wait()
        pltpu.sync_copy(buf0, out_hbm.at[pl.ds(base + j * LANES, LANES)])
        c1.wait()
        pltpu.sync_copy(buf1, out_hbm.at[pl.ds(base + (j + 1) * LANES, LANES)])


def _sc_gather(tab, idx):
    n_chunks = idx.shape[1]
    return pl.kernel(
        _gather_body,
        out_type=jax.ShapeDtypeStruct((SC_WORKERS * n_chunks * LANES, LANES), jnp.int32),
        mesh=_sc_mesh(),
        scratch_types=[pltpu.VMEM((n_chunks, LANES), jnp.int32),
                       pltpu.VMEM((LANES, LANES), jnp.int32),
                       pltpu.VMEM((LANES, LANES), jnp.int32),
                       pltpu.SemaphoreType.DMA, pltpu.SemaphoreType.DMA],
        name="sc_gather",
    )(tab, idx)


def _expert_kernel(be_ref, bv_ref, x_ref, wg_ref, bg_ref, wu_ref, bu_ref, wd_ref, bd_ref, y_ref,
                   wg_scr, wu_scr, wd_scr):
    i = pl.program_id(0)
    valid = bv_ref[i]
    changed = jnp.logical_or(i == 0, be_ref[i] != be_ref[jnp.maximum(i - 1, 0)])

    @pl.when(jnp.logical_and(changed, valid > 0))
    def _():
        wg_scr[...] = wg_ref[...].astype(BF16)
        wu_scr[...] = wu_ref[...].astype(BF16)
        wd_scr[...] = wd_ref[...].astype(BF16)

    @pl.when(valid > 0)
    def _():
        lo, hi = _unpack_rows(x_ref, ROW_BLOCK)
        live = lax.broadcasted_iota(jnp.int32, (ROW_BLOCK, 1), 0) < valid
        x = jnp.where(live, jnp.concatenate(lo + hi, axis=1), 0.0).astype(BF16)
        g = _dot(x, wg_scr[...]) + bg_ref[...]
        l = _dot(x, wu_scr[...]) + bu_ref[...]
        g = jnp.minimum(g, SWIGLU_LIMIT)
        l = jnp.clip(l, -SWIGLU_LIMIT, SWIGLU_LIMIT)
        hid = g * _sigmoid(SWIGLU_ALPHA * g) * (l + 1.0)
        _pack_rows(_dot(hid.astype(BF16), wd_scr[...]) + bd_ref[...], y_ref)

    @pl.when(valid <= 0)
    def _():
        y_ref[...] = jnp.zeros_like(y_ref)


def _expert_call(block_expert, block_valid, x_tab, w_gate, b_gate, w_up, b_up, w_down, b_down):
    d, f = w_gate.shape[-2:]
    n_blocks = block_expert.shape[0]
    wspec = lambda shape: pl.BlockSpec((None,) + shape, lambda i, be, bv: (be[i], 0, 0))
    rows = pl.BlockSpec((ROW_BLOCK * SUBROWS, LANES), lambda i, be, bv: (i, 0))
    grid_spec = pltpu.PrefetchScalarGridSpec(
        num_scalar_prefetch=2,
        grid=(n_blocks,),
        in_specs=[rows, wspec((d, f)), wspec((1, f)), wspec((d, f)), wspec((1, f)),
                  wspec((f, d)), wspec((1, d))],
        out_specs=rows,
        scratch_shapes=[pltpu.VMEM((d, f), BF16), pltpu.VMEM((d, f), BF16),
                        pltpu.VMEM((f, d), BF16)],
    )
    return pl.pallas_call(
        _expert_kernel,
        grid_spec=grid_spec,
        out_shape=jax.ShapeDtypeStruct(x_tab.shape, jnp.int32),
        compiler_params=pltpu.CompilerParams(
            dimension_semantics=("arbitrary",), vmem_limit_bytes=VMEM_LIMIT),
        name="expert",
    )(block_expert, block_valid, x_tab, w_gate, b_gate, w_up, b_up, w_down, b_down)


def _combine_kernel(h_ref, route_ref, gf_ref, ys_ref, out_ref):
    nb, tl, d = out_ref.shape
    m = nb * tl
    h = h_ref[...]
    lo_acc = [h[:, s * LANES:(s + 1) * LANES] for s in range(SUBROWS)]
    hi_acc = [h[:, PACKED + s * LANES:PACKED + (s + 1) * LANES] for s in range(SUBROWS)]
    for k in range(TOP_K):
        gate = route_ref[:, TOP_K + k:TOP_K + k + 1]
        lo, hi = _unpack_rows(ys_ref.at[k], m)
        lo_acc = [a + gate * v for a, v in zip(lo_acc, lo)]
        hi_acc = [a + gate * v for a, v in zip(hi_acc, hi)]
    acc = jnp.concatenate(lo_acc + hi_acc, axis=1)
    out_ref[...] = _rmsnorm(acc, gf_ref[...]).reshape(nb, tl, d)


def _combine_call(h, route, g_final, y_slots, nb, seq):
    t, d = h.shape
    tl = TIME_TILE
    m = nb * tl
    return pl.pallas_call(
        _combine_kernel,
        grid=(t // m,),
        in_specs=[pl.BlockSpec((m, d), lambda i: (i, 0)),
                  pl.BlockSpec((m, LANES), lambda i: (i, 0)),
                  pl.BlockSpec((1, d), lambda i: (0, 0)),
                  pl.BlockSpec((TOP_K, m * SUBROWS, LANES), lambda i: (0, i, 0))],
        out_specs=pl.BlockSpec((nb, tl, d), lambda i: (0, i, 0)),
        out_shape=jax.ShapeDtypeStruct((nb, seq, d), F32),
        compiler_params=pltpu.CompilerParams(
            dimension_semantics=("arbitrary",), vmem_limit_bytes=VMEM_LIMIT),
        name="combine",
    )(h, route, g_final, y_slots)


def _s5_operands(lam_re, lam_im, log_dt, b_re, b_im, c_re, c_im, nb, tl):
    gp, ns, pg = SSM_GROUPS, SSM_STATE, SSM_GROUP
    lam = lax.complex(lam_re.astype(F32), lam_im.astype(F32))
    dt = jnp.exp(log_dt.astype(F32))[:, None]
    lam_bar = jnp.exp(lam * dt)
    b_bar = ((lam_bar - 1.0) / lam)[..., None] * lax.complex(b_re.astype(F32), b_im.astype(F32))
    grp = lambda hf, p: slice(16 * hf + 8 * p, 16 * hf + 8 * p + 8)
    eye8 = jnp.eye(8, dtype=F32)

    def b_block(bb):
        return jnp.einsum('gnq,gh->gqhn', bb, eye8).reshape(8 * pg, 8 * ns)

    def c_block(cc):
        return jnp.einsum('gqn,gh->gnhq', cc, eye8).reshape(8 * ns, 8 * pg)

    bd, cd, a_re, a_im = [], [], [], []
    for p in range(2):
        rows = []
        for hf in range(2):
            bb = b_bar[grp(hf, p)]
            rows.append(jnp.concatenate([b_block(bb.real), b_block(bb.imag)], axis=1))
        bd.append(jnp.concatenate(rows, axis=0))
        cols = []
        for hf in range(2):
            cols.append(jnp.concatenate([c_block(c_re[grp(hf, p)].astype(F32)),
                                         -c_block(c_im[grp(hf, p)].astype(F32))], axis=0))
        cd.append(jnp.concatenate(cols, axis=1))
        lam_rows = jnp.stack([lam_bar[grp(j // nb, p)].reshape(8 * ns) for j in range(SUBLANES)])
        a_re.append(lam_rows.real)
        a_im.append(lam_rows.imag)
    bd = jnp.stack(bd).astype(BF16)
    cd = jnp.stack(cd).astype(BF16)
    a_re = jnp.stack(a_re)
    a_im = jnp.stack(a_im)

    r8 = jnp.arange(SUBLANES * tl)
    src = (r8 % nb) * tl + r8 // SUBLANES
    perm = (src[:, None] == jnp.arange(nb * tl)[None, :]).astype(BF16)
    return perm, perm.T, bd, cd, a_re, a_im


def kernel(x, norm_mix_g, w_in, lam_re, lam_im, log_dt, b_re, b_im, c_re, c_im, d_skip, w_glu, b_glu, sgu_ln_g, sgu_ln_b, w_s, b_s, w_branch_a, w_branch_b, w_out, norm_moe_g, w_router, b_router, w_gate, b_gate, w_up, b_up, w_down, b_down, norm_final_g):
    nb, seq, d = x.shape
    assert d == D_MODEL and SUBLANES % nb == 0 and SUBLANES // nb == 2
    assert seq % TIME_TILE == 0 and norm_mix_g.shape[0] == 1
    assert (nb * seq * SUBROWS) % (SC_WORKERS * LANES) == 0
    assert (nb * seq * SUBROWS * TOP_K) % (SC_WORKERS * LANES * 2) == 0
    tl = TIME_TILE
    t = nb * seq
    row = lambda v: v.reshape(1, -1).astype(F32)

    xa, sga, pb = _proj_call(
        x, row(norm_mix_g[0]), w_in[0].astype(BF16), row(sgu_ln_g[0]), row(sgu_ln_b[0]),
        w_s[0].astype(F32), b_s[0].T.astype(F32), w_branch_b[0].astype(BF16))

    perm, permt, bd, cd, a_re, a_im = _s5_operands(
        lam_re[0], lam_im[0], log_dt[0], b_re[0], b_im[0], c_re[0], c_im[0], nb, tl)
    w_r = jnp.zeros((d, LANES), F32).at[:, :N_EXPERTS].set(w_router[0].astype(F32)).astype(BF16)
    b_r = jnp.full((1, LANES), NEG_BIG, F32).at[0, :N_EXPERTS].set(b_router[0].astype(F32))
    h, xn2p, route, cnt = _mix_call(
        xa, sga, pb, x, perm, permt, bd, cd, a_re, a_im, row(d_skip[0]),
        w_glu[0].astype(BF16), row(b_glu[0]), w_branch_a[0].astype(BF16), w_out[0].astype(BF16),
        row(norm_moe_g[0]), w_r, b_r)

    idx = route[:, 0:TOP_K].astype(jnp.int32)
    rank = route[:, 2 * TOP_K:3 * TOP_K].astype(jnp.int32)
    counts = cnt[0, :N_EXPERTS].astype(jnp.int32)
    padded = (counts + ROW_BLOCK - 1) // ROW_BLOCK * ROW_BLOCK
    cum = jnp.cumsum(padded)
    pstart = cum - padded
    experts = jnp.arange(N_EXPERTS, dtype=jnp.int32)
    dest = rank + jnp.sum(jnp.where(idx[..., None] == experts, pstart, 0), axis=-1)
    n_blocks = (t * TOP_K) // ROW_BLOCK + N_EXPERTS
    block_row0 = jnp.arange(n_blocks, dtype=jnp.int32) * ROW_BLOCK
    block_expert = jnp.minimum(
        jnp.sum((cum[None, :] <= block_row0[:, None]).astype(jnp.int32), axis=1), N_EXPERTS - 1)
    block_valid = jnp.clip(counts[block_expert] - (block_row0 - pstart[block_expert]), 0, ROW_BLOCK)

    sub = dest[..., None] * SUBROWS + jnp.arange(SUBROWS, dtype=jnp.int32)
    tok_per_chunk = LANES // SUBROWS
    idx_dispatch = sub.reshape(t // tok_per_chunk, tok_per_chunk, TOP_K, SUBROWS)
    idx_dispatch = idx_dispatch.transpose(0, 2, 1, 3).reshape(SC_WORKERS, -1, LANES)
    idx_gather = sub.transpose(1, 0, 2).reshape(SC_WORKERS, -1, LANES)

    x_tab = _sc_dispatch(xn2p, idx_dispatch, n_blocks * ROW_BLOCK * SUBROWS)
    y_tab = _expert_call(
        block_expert, block_valid, x_tab,
        w_gate[0], b_gate[0][:, None, :], w_up[0], b_up[0][:, None, :],
        w_down[0], b_down[0][:, None, :])
    y_slots = _sc_gather(y_tab, idx_gather).reshape(TOP_K, t * SUBROWS, LANES)
    return _combine_call(h, route, row(norm_final_g), y_slots, nb, seq)
```

```python
import jax
import jax.numpy as jnp
from jax import lax
from jax.experimental import pallas as pl
from jax.experimental.pallas import tpu as pltpu
from jax.experimental.pallas import tpu_sc as plsc

F32 = jnp.float32
BF16 = jnp.bfloat16

NORM_EPS = 1e-5
D_MODEL = 1024
SSM_WIDTH = 512
SSM_GROUP = 16
SSM_GROUPS = 32
SSM_STATE = 64
SGU_WIDTH = 512
SGU_BLOCK = 128
SGU_HEADS = 4
SGU_HEAD_DIM = 128
CHUNK = 64
N_EXPERTS = 32
TOP_K = 4
SWIGLU_ALPHA = 1.702
SWIGLU_LIMIT = 7.0

LANES = 128
SUBLANES = 8
TIME_TILE = 128
ROW_BLOCK = 512
VMEM_LIMIT = 56 * 1024 * 1024
NEG_BIG = -1e30
SC_CORES = 2
SC_SUBCORES = 16
SC_WORKERS = SC_CORES * SC_SUBCORES
SUBROWS = 4
ROUTE_ROWS = 16
PACKED = D_MODEL // 2


def _pack_rows(x, out_ref):
    lo = lax.bitcast_convert_type(x[:, :PACKED].astype(BF16).astype(F32), jnp.int32)
    hi = lax.bitcast_convert_type(x[:, PACKED:].astype(BF16).astype(F32), jnp.int32)
    words = lax.shift_right_logical(lo, 16) | (hi & jnp.int32(-65536))
    for s in range(SUBROWS):
        out_ref[s] = words[:, s * LANES:(s + 1) * LANES]


def _unpack_rows(ref):
    lo, hi = [], []
    for s in range(SUBROWS):
        w = ref[s]
        lo.append(lax.bitcast_convert_type(lax.shift_left(w, 16), F32))
        hi.append(lax.bitcast_convert_type(w & jnp.int32(-65536), F32))
    return lo, hi


def _dot(a, b):
    return jnp.dot(a, b, preferred_element_type=F32)


def _sigmoid(x):
    return 1.0 / (1.0 + jnp.exp(-x))


def _gelu(x):
    return 0.5 * x * (1.0 + jnp.tanh(0.7978845608028654 * (x + 0.044715 * (x * x * x))))


def _rmsnorm(x, g):
    return x * lax.rsqrt(jnp.mean(x * x, axis=-1, keepdims=True) + NORM_EPS) * g


def _proj_kernel(x_ref, g_ref, w_ref, lng_ref, lnb_ref, ws_ref, bs_ref, wb_ref,
                 xa_ref, sga_ref, pb_ref, yb_scr):
    nb, tl, d = x_ref.shape
    m = nb * tl
    x = x_ref[...].reshape(m, d)
    xn = _rmsnorm(x, g_ref[...]).astype(BF16)

    xa = _dot(xn, w_ref[:, 0:SSM_WIDTH])
    xa_ref[...] = xa.astype(BF16).reshape(nb, tl, SSM_WIDTH)

    s_b = SSM_WIDTH + 2 * SGU_WIDTH
    s_g = s_b + D_MODEL
    ga = _dot(xn, w_ref[:, s_b:s_g])
    sga_ref[...] = _sigmoid(ga).astype(BF16).reshape(nb, tl, D_MODEL)

    z = _gelu(_dot(xn, w_ref[:, SSM_WIDTH:s_b]))
    u = z[:, :SGU_WIDTH]
    v = z[:, SGU_WIDTH:]
    mu = jnp.mean(v, axis=-1, keepdims=True)
    vc = v - mu
    v = vc * lax.rsqrt(jnp.mean(vc * vc, axis=-1, keepdims=True) + NORM_EPS)
    v = (v * lng_ref[...] + lnb_ref[...]).astype(BF16)

    ri = lax.broadcasted_iota(jnp.int32, (SGU_BLOCK, SGU_BLOCK), 0) // CHUNK
    ci = lax.broadcasted_iota(jnp.int32, (SGU_BLOCK, SGU_BLOCK), 1) // CHUNK
    causal = ri >= ci
    for h in range(SGU_HEADS):
        wm = jnp.where(causal, ws_ref[h], 0.0).astype(BF16)
        bias = bs_ref[:, h:h + 1]
        lo = h * SGU_HEAD_DIM
        for blk in range(m // SGU_BLOCK):
            r0 = blk * SGU_BLOCK
            s = _dot(wm, v[r0:r0 + SGU_BLOCK, lo:lo + SGU_HEAD_DIM]) + bias
            yb_scr[r0:r0 + SGU_BLOCK, lo:lo + SGU_HEAD_DIM] = (
                u[r0:r0 + SGU_BLOCK, lo:lo + SGU_HEAD_DIM] * s).astype(BF16)

    gb = _dot(xn, w_ref[:, s_g:])
    pb = _sigmoid(gb) * _dot(yb_scr[...], wb_ref[...])
    pb_ref[...] = pb.astype(BF16).reshape(nb, tl, D_MODEL)


def _proj_call(x, g, w_in, ln_g, ln_b, w_s, b_s_t, w_b):
    nb, seq, d = x.shape
    tl = TIME_TILE
    m = nb * tl
    const = lambda shape: pl.BlockSpec(shape, lambda i: (0,) * len(shape))
    tile = lambda width: pl.BlockSpec((nb, tl, width), lambda i: (0, i, 0))
    return pl.pallas_call(
        _proj_kernel,
        grid=(seq // tl,),
        in_specs=[tile(d), const((1, d)), const(w_in.shape), const((1, SGU_WIDTH)),
                  const((1, SGU_WIDTH)), const(w_s.shape), const(b_s_t.shape), const(w_b.shape)],
        out_specs=[tile(SSM_WIDTH), tile(D_MODEL), tile(D_MODEL)],
        out_shape=[jax.ShapeDtypeStruct((nb, seq, SSM_WIDTH), BF16),
                   jax.ShapeDtypeStruct((nb, seq, D_MODEL), BF16),
                   jax.ShapeDtypeStruct((nb, seq, D_MODEL), BF16)],
        scratch_shapes=[pltpu.VMEM((m, SGU_WIDTH), BF16)],
        compiler_params=pltpu.CompilerParams(
            dimension_semantics=("arbitrary",), vmem_limit_bytes=VMEM_LIMIT),
        name="proj",
    )(x, g, w_in, ln_g, ln_b, w_s, b_s_t, w_b)


def _mix_kernel(xa_ref, sga_ref, pb_ref, x_ref, perm_ref, permt_ref, bd_ref, cd_ref,
                are_ref, aim_ref, dskip_ref, wglu_ref, bglu_ref, wa_ref, wout_ref,
                g2_ref, wr_ref, br_ref,
                h_ref, xn2_ref, route_ref, routet_ref, cnt_ref,
                bu_scr, state_scr, cnt_scr):
    nb, tl, d = x_ref.shape
    m = nb * tl
    rows8 = SUBLANES * tl

    @pl.when(pl.program_id(0) == 0)
    def _():
        state_scr[...] = jnp.zeros_like(state_scr)
        cnt_scr[...] = jnp.zeros_like(cnt_scr)

    xa = xa_ref[...].reshape(m, SSM_WIDTH)
    gathered = _dot(perm_ref[...], xa)
    first_half = (lax.broadcasted_iota(jnp.int32, (rows8, 1), 0) % SUBLANES) < nb

    y_parts = []
    for p in range(2):
        lo_a = LANES * p
        lo_b = SSM_WIDTH // 2 + LANES * p
        x8 = jnp.concatenate(
            [jnp.where(first_half, gathered[:, lo_a:lo_a + LANES], 0.0),
             jnp.where(first_half, 0.0, gathered[:, lo_b:lo_b + LANES])], axis=1).astype(BF16)
        bu_scr[...] = _dot(x8, bd_ref[p])

        a_re = are_ref[p]
        a_im = aim_ref[p]

        def step(t, carry, a_re=a_re, a_im=a_im):
            s_re, s_im = carry
            r = pl.multiple_of(t * SUBLANES, SUBLANES)
            b_re = bu_scr[pl.ds(r, SUBLANES), 0:SSM_WIDTH]
            b_im = bu_scr[pl.ds(r, SUBLANES), SSM_WIDTH:2 * SSM_WIDTH]
            n_re = a_re * s_re - a_im * s_im + b_re
            n_im = a_re * s_im + a_im * s_re + b_im
            bu_scr[pl.ds(r, SUBLANES), 0:SSM_WIDTH] = n_re
            bu_scr[pl.ds(r, SUBLANES), SSM_WIDTH:2 * SSM_WIDTH] = n_im
            return n_re, n_im

        st = state_scr[p]
        s_re, s_im = lax.fori_loop(0, tl, step, (st[:, :SSM_WIDTH], st[:, SSM_WIDTH:]), unroll=8)
        state_scr[p] = jnp.concatenate([s_re, s_im], axis=1)

        y8 = _dot(bu_scr[...].astype(BF16), cd_ref[p])
        y8m = jnp.concatenate(
            [jnp.where(first_half, y8[:, :LANES], 0.0),
             jnp.where(first_half, 0.0, y8[:, LANES:])], axis=1).astype(BF16)
        y_parts.append(_dot(permt_ref[...], y8m))

    y = jnp.concatenate([y_parts[0][:, :LANES], y_parts[1][:, :LANES],
                         y_parts[0][:, LANES:], y_parts[1][:, LANES:]], axis=1)
    y = y + dskip_ref[...] * xa.astype(F32)
    z = _gelu(y)
    ya = z * _sigmoid(_dot(z.astype(BF16), wglu_ref[...]) + bglu_ref[...])
    pa = sga_ref[...].reshape(m, d).astype(F32) * _dot(ya.astype(BF16), wa_ref[...])
    merged = pa + pb_ref[...].reshape(m, d).astype(F32)
    h = x_ref[...].reshape(m, d) + _dot(merged.astype(BF16), wout_ref[...])
    h_ref[...] = h
    xn2 = _rmsnorm(h, g2_ref[...])
    _pack_rows(xn2, xn2_ref)

    logits = _dot(xn2.astype(BF16), wr_ref[...]) + br_ref[...]
    lane = lax.broadcasted_iota(jnp.int32, (m, LANES), 1).astype(F32)
    work = logits
    vals, idxs = [], []
    member = jnp.zeros((m, LANES), F32)
    for _ in range(TOP_K):
        mx = jnp.max(work, axis=-1, keepdims=True)
        ix = jnp.min(jnp.where(work == mx, lane, float(LANES)), axis=-1, keepdims=True)
        hit = lane == ix
        member = jnp.where(hit, 1.0, member)
        work = jnp.where(hit, NEG_BIG, work)
        vals.append(mx)
        idxs.append(ix)
    exps = [jnp.exp(v - vals[0]) for v in vals]
    denom = exps[0] + exps[1] + exps[2] + exps[3]
    gates = [e / denom for e in exps]

    ri = lax.broadcasted_iota(jnp.int32, (m, m), 0)
    ci = lax.broadcasted_iota(jnp.int32, (m, m), 1)
    lower = jnp.where(ri > ci, 1.0, 0.0).astype(BF16)
    before = _dot(lower, member.astype(BF16)) + cnt_scr[0:1, :]
    new_cnt = cnt_scr[0:1, :] + jnp.sum(member, axis=0, keepdims=True)
    cnt_scr[...] = jnp.broadcast_to(new_cnt, cnt_scr.shape)
    cnt_ref[...] = jnp.broadcast_to(new_cnt, cnt_ref.shape)

    route = jnp.zeros((m, LANES), F32)
    for k in range(TOP_K):
        rank_k = jnp.sum(jnp.where(lane == idxs[k], before, 0.0), axis=-1, keepdims=True)
        route = jnp.where(lane == float(k), idxs[k], route)
        route = jnp.where(lane == float(TOP_K + k), gates[k], route)
        route = jnp.where(lane == float(2 * TOP_K + k), rank_k, route)
    route_ref[...] = route
    routet_ref[...] = jnp.transpose(route)[:ROUTE_ROWS]


def _mix_call(xa, sga, pb, x, perm, permt, bd, cd, a_re, a_im, dskip, w_glu, b_glu, w_a, w_out,
              g2, w_r, b_r):
    nb, seq, d = x.shape
    tl = TIME_TILE
    m = nb * tl
    n_tiles = seq // tl
    const = lambda shape: pl.BlockSpec(shape, lambda i: (0,) * len(shape))
    tile = lambda width: pl.BlockSpec((nb, tl, width), lambda i: (0, i, 0))
    rows = lambda width: pl.BlockSpec((m, width), lambda i: (i, 0))
    operands = (xa, sga, pb, x, perm, permt, bd, cd, a_re, a_im, dskip, w_glu, b_glu, w_a, w_out,
                g2, w_r, b_r)
    in_specs = [tile(SSM_WIDTH), tile(d), tile(d), tile(d)] + [const(o.shape) for o in operands[4:]]
    return pl.pallas_call(
        _mix_kernel,
        grid=(n_tiles,),
        in_specs=in_specs,
        out_specs=[rows(d), pl.BlockSpec((SUBROWS, m, LANES), lambda i: (0, i, 0)), rows(LANES),
                   pl.BlockSpec((ROUTE_ROWS, m), lambda i: (0, i)), const((SUBLANES, LANES))],
        out_shape=[jax.ShapeDtypeStruct((n_tiles * m, d), F32),
                   jax.ShapeDtypeStruct((SUBROWS, n_tiles * m, LANES), jnp.int32),
                   jax.ShapeDtypeStruct((n_tiles * m, LANES), F32),
                   jax.ShapeDtypeStruct((ROUTE_ROWS, n_tiles * m), F32),
                   jax.ShapeDtypeStruct((SUBLANES, LANES), F32)],
        scratch_shapes=[pltpu.VMEM((SUBLANES * tl, 2 * SSM_WIDTH), F32),
                        pltpu.VMEM((2, SUBLANES, 2 * SSM_WIDTH), F32),
                        pltpu.VMEM((SUBLANES, LANES), F32)],
        compiler_params=pltpu.CompilerParams(
            dimension_semantics=("arbitrary",), vmem_limit_bytes=VMEM_LIMIT),
        name="mix",
    )(*operands)


def _sc_mesh():
    return plsc.VectorSubcoreMesh(core_axis_name="c", subcore_axis_name="s")


def _worker_id():
    return lax.axis_index("s") * SC_CORES + lax.axis_index("c")


def _dispatch_body(src_hbm, idx_hbm, out_hbm, idx_v, buf, sem):
    n_chunks = idx_v.shape[0] // TOP_K
    wid = _worker_id()
    base = wid * (n_chunks * LANES)
    pltpu.sync_copy(idx_hbm.at[wid], idx_v)

    @pl.loop(0, n_chunks)
    def _(j):
        pltpu.sync_copy(src_hbm.at[pl.ds(base + j * LANES, LANES)], buf)
        copies = [pltpu.async_copy(buf, out_hbm.at[idx_v.at[j * TOP_K + k]], sem)
                  for k in range(TOP_K)]
        for c in copies:
            c.wait()


def _sc_dispatch(src, idx, n_out):
    n_chunks = src.shape[0] // (SC_WORKERS * LANES)
    return pl.kernel(
        _dispatch_body,
        out_type=jax.ShapeDtypeStruct((n_out, LANES), jnp.int32),
        mesh=_sc_mesh(),
        scratch_types=[pltpu.VMEM((n_chunks * TOP_K, LANES), jnp.int32),
                       pltpu.VMEM((LANES, LANES), jnp.int32),
                       pltpu.SemaphoreType.DMA],
        name="sc_dispatch",
    )(src, idx)


def _gather_body(tab_hbm, idx_hbm, out_hbm, idx_v, buf0, buf1, sem0, sem1):
    n_chunks = idx_v.shape[0]
    wid = _worker_id()
    base = wid * (n_chunks * LANES)
    pltpu.sync_copy(idx_hbm.at[wid], idx_v)

    @pl.loop(0, n_chunks, step=2)
    def _(j):
        c0 = pltpu.async_copy(tab_hbm.at[idx_v.at[j]], buf0, sem0)
        c1 = pltpu.async_copy(tab_hbm.at[idx_v.at[j + 1]], buf1, sem1)
        c0.wait()
        pltpu.sync_copy(buf0, out_hbm.at[pl.ds(base + j * LANES, LANES)])
        c1.wait()
        pltpu.sync_copy(buf1, out_hbm.at[pl.ds(base + (j + 1) * LANES, LANES)])


def _sc_gather(tab, idx):
    n_chunks = idx.shape[1]
    return pl.kernel(
        _gather_body,
        out_type=jax.ShapeDtypeStruct((SC_WORKERS * n_chunks * LANES, LANES), jnp.int32),
        mesh=_sc_mesh(),
        scratch_types=[pltpu.VMEM((n_chunks, LANES), jnp.int32),
                       pltpu.VMEM((LANES, LANES), jnp.int32),
                       pltpu.VMEM((LANES, LANES), jnp.int32),
                       pltpu.SemaphoreType.DMA, pltpu.SemaphoreType.DMA],
        name="sc_gather",
    )(tab, idx)


def _expert_kernel(be_ref, bv_ref, nx_ref, sl_ref, x_ref, bg_ref, bu_ref, bd_ref,
                   wg_hbm, wu_hbm, wd_hbm, y_ref, stage, wg_scr, wu_scr, wd_scr, sems):
    i = pl.program_id(0)
    valid = bv_ref[i]
    first = jnp.logical_and(
        valid > 0, jnp.logical_or(i == 0, be_ref[i] != be_ref[jnp.maximum(i - 1, 0)]))

    def weight_copies(expert, slot):
        return [pltpu.make_async_copy(w.at[expert], stage.at[slot, j], sems.at[slot, j])
                for j, w in enumerate((wg_hbm, wu_hbm, wd_hbm))]

    @pl.when(i == 0)
    def _():
        for c in weight_copies(be_ref[0], sl_ref[0]):
            c.start()

    @pl.when(first)
    def _():
        slot = sl_ref[i]
        for c in weight_copies(be_ref[i], slot):
            c.wait()
        wg_scr[...] = stage[slot, 0].astype(BF16)
        wu_scr[...] = stage[slot, 1].astype(BF16)
        wd_scr[...] = stage[slot, 2].astype(BF16)

        @pl.when(nx_ref[i] >= 0)
        def _():
            for c in weight_copies(nx_ref[i], 1 - slot):
                c.start()

    @pl.when(valid > 0)
    def _():
        lo, hi = _unpack_rows(x_ref)
        live = lax.broadcasted_iota(jnp.int32, (ROW_BLOCK, 1), 0) < valid
        x = jnp.where(live, jnp.concatenate(lo + hi, axis=1), 0.0).astype(BF16)
        g = _dot(x, wg_scr[...]) + bg_ref[...]
        l = _dot(x, wu_scr[...]) + bu_ref[...]
        g = jnp.minimum(g, SWIGLU_LIMIT)
        l = jnp.clip(l, -SWIGLU_LIMIT, SWIGLU_LIMIT)
        hid = g * _sigmoid(SWIGLU_ALPHA * g) * (l + 1.0)
        _pack_rows(_dot(hid.astype(BF16), wd_scr[...]) + bd_ref[...], y_ref)

    @pl.when(valid <= 0)
    def _():
        y_ref[...] = jnp.zeros_like(y_ref)


def _expert_call(block_expert, block_valid, next_expert, slot, x_tab,
                 w_gate, b_gate, w_up, b_up, w_down, b_down):
    d, f = w_gate.shape[-2:]
    assert d == f
    n_blocks = block_expert.shape[0]
    bspec = lambda width: pl.BlockSpec((None, 1, width), lambda i, be, bv, nx, sl: (be[i], 0, 0))
    rows = pl.BlockSpec((SUBROWS, ROW_BLOCK, LANES), lambda i, be, bv, nx, sl: (0, i, 0))
    hbm = pl.BlockSpec(memory_space=pl.ANY)
    grid_spec = pltpu.PrefetchScalarGridSpec(
        num_scalar_prefetch=4,
        grid=(n_blocks,),
        in_specs=[rows, bspec(f), bspec(f), bspec(d), hbm, hbm, hbm],
        out_specs=rows,
        scratch_shapes=[pltpu.VMEM((2, 3, d, f), F32),
                        pltpu.VMEM((d, f), BF16), pltpu.VMEM((d, f), BF16),
                        pltpu.VMEM((f, d), BF16), pltpu.SemaphoreType.DMA((2, 3))],
    )
    return pl.pallas_call(
        _expert_kernel,
        grid_spec=grid_spec,
        out_shape=jax.ShapeDtypeStruct(x_tab.shape, jnp.int32),
        compiler_params=pltpu.CompilerParams(
            dimension_semantics=("arbitrary",), vmem_limit_bytes=VMEM_LIMIT),
        name="expert",
    )(block_expert, block_valid, next_expert, slot, x_tab, b_gate, b_up, b_down,
      w_gate, w_up, w_down)


def _combine_kernel(h_ref, route_ref, gf_ref, ys_ref, out_ref):
    nb, tl, d = out_ref.shape
    m = nb * tl
    h = h_ref[...]
    lo_acc = [h[:, s * LANES:(s + 1) * LANES] for s in range(SUBROWS)]
    hi_acc = [h[:, PACKED + s * LANES:PACKED + (s + 1) * LANES] for s in range(SUBROWS)]
    for k in range(TOP_K):
        gate = route_ref[:, TOP_K + k:TOP_K + k + 1]
        lo, hi = _unpack_rows(ys_ref.at[k])
        lo_acc = [a + gate * v for a, v in zip(lo_acc, lo)]
        hi_acc = [a + gate * v for a, v in zip(hi_acc, hi)]
    acc = jnp.concatenate(lo_acc + hi_acc, axis=1)
    out_ref[...] = _rmsnorm(acc, gf_ref[...]).reshape(nb, tl, d)


def _combine_call(h, route, g_final, y_slots, nb, seq):
    t, d = h.shape
    tl = TIME_TILE
    m = nb * tl
    return pl.pallas_call(
        _combine_kernel,
        grid=(t // m,),
        in_specs=[pl.BlockSpec((m, d), lambda i: (i, 0)),
                  pl.BlockSpec((m, LANES), lambda i: (i, 0)),
                  pl.BlockSpec((1, d), lambda i: (0, 0)),
                  pl.BlockSpec((TOP_K, SUBROWS, m, LANES), lambda i: (0, 0, i, 0))],
        out_specs=pl.BlockSpec((nb, tl, d), lambda i: (0, i, 0)),
        out_shape=jax.ShapeDtypeStruct((nb, seq, d), F32),
        compiler_params=pltpu.CompilerParams(
            dimension_semantics=("arbitrary",), vmem_limit_bytes=VMEM_LIMIT),
        name="combine",
    )(h, route, g_final, y_slots)


def _s5_operands(lam_re, lam_im, log_dt, b_re, b_im, c_re, c_im, nb, tl):
    gp, ns, pg = SSM_GROUPS, SSM_STATE, SSM_GROUP
    lam = lax.complex(lam_re.astype(F32), lam_im.astype(F32))
    dt = jnp.exp(log_dt.astype(F32))[:, None]
    lam_bar = jnp.exp(lam * dt)
    b_bar = ((lam_bar - 1.0) / lam)[..., None] * lax.complex(b_re.astype(F32), b_im.astype(F32))
    grp = lambda hf, p: slice(16 * hf + 8 * p, 16 * hf + 8 * p + 8)
    eye8 = jnp.eye(8, dtype=F32)

    def b_block(bb):
        return jnp.einsum('gnq,gh->gqhn', bb, eye8).reshape(8 * pg, 8 * ns)

    def c_block(cc):
        return jnp.einsum('gqn,gh->gnhq', cc, eye8).reshape(8 * ns, 8 * pg)

    bd, cd, a_re, a_im = [], [], [], []
    for p in range(2):
        rows = []
        for hf in range(2):
            bb = b_bar[grp(hf, p)]
            rows.append(jnp.concatenate([b_block(bb.real), b_block(bb.imag)], axis=1))
        bd.append(jnp.concatenate(rows, axis=0))
        cols = []
        for hf in range(2):
            cols.append(jnp.concatenate([c_block(c_re[grp(hf, p)].astype(F32)),
                                         -c_block(c_im[grp(hf, p)].astype(F32))], axis=0))
        cd.append(jnp.concatenate(cols, axis=1))
        lam_rows = jnp.stack([lam_bar[grp(j // nb, p)].reshape(8 * ns) for j in range(SUBLANES)])
        a_re.append(lam_rows.real)
        a_im.append(lam_rows.imag)
    bd = jnp.stack(bd).astype(BF16)
    cd = jnp.stack(cd).astype(BF16)
    a_re = jnp.stack(a_re)
    a_im = jnp.stack(a_im)

    r8 = jnp.arange(SUBLANES * tl)
    src = (r8 % nb) * tl + r8 // SUBLANES
    perm = (src[:, None] == jnp.arange(nb * tl)[None, :]).astype(BF16)
    return perm, perm.T, bd, cd, a_re, a_im


def kernel(x, norm_mix_g, w_in, lam_re, lam_im, log_dt, b_re, b_im, c_re, c_im, d_skip, w_glu, b_glu, sgu_ln_g, sgu_ln_b, w_s, b_s, w_branch_a, w_branch_b, w_out, norm_moe_g, w_router, b_router, w_gate, b_gate, w_up, b_up, w_down, b_down, norm_final_g):
    nb, seq, d = x.shape
    assert d == D_MODEL and SUBLANES % nb == 0 and SUBLANES // nb == 2
    assert seq % TIME_TILE == 0 and norm_mix_g.shape[0] == 1
    assert (nb * seq * SUBROWS) % (SC_WORKERS * LANES) == 0
    assert (nb * seq * SUBROWS * TOP_K) % (SC_WORKERS * LANES * 2) == 0
    tl = TIME_TILE
    t = nb * seq
    row = lambda v: v.reshape(1, -1).astype(F32)

    xa, sga, pb = _proj_call(
        x, row(norm_mix_g[0]), w_in[0].astype(BF16), row(sgu_ln_g[0]), row(sgu_ln_b[0]),
        w_s[0].astype(F32), b_s[0].T.astype(F32), w_branch_b[0].astype(BF16))

    perm, permt, bd, cd, a_re, a_im = _s5_operands(
        lam_re[0], lam_im[0], log_dt[0], b_re[0], b_im[0], c_re[0], c_im[0], nb, tl)
    w_r = jnp.zeros((d, LANES), F32).at[:, :N_EXPERTS].set(w_router[0].astype(F32)).astype(BF16)
    b_r = jnp.full((1, LANES), NEG_BIG, F32).at[0, :N_EXPERTS].set(b_router[0].astype(F32))
    h, xn2p, route, route_t, cnt = _mix_call(
        xa, sga, pb, x, perm, permt, bd, cd, a_re, a_im, row(d_skip[0]),
        w_glu[0].astype(BF16), row(b_glu[0]), w_branch_a[0].astype(BF16), w_out[0].astype(BF16),
        row(norm_moe_g[0]), w_r, b_r)

    idx_t = route_t[0:TOP_K].astype(jnp.int32)
    rank_t = route_t[2 * TOP_K:3 * TOP_K].astype(jnp.int32)
    counts = cnt[0, :N_EXPERTS].astype(jnp.int32)
    padded = (counts + ROW_BLOCK - 1) // ROW_BLOCK * ROW_BLOCK
    cum = jnp.cumsum(padded)
    pstart = cum - padded
    experts = jnp.arange(N_EXPERTS, dtype=jnp.int32)
    dest_t = rank_t
    for e in range(N_EXPERTS):
        dest_t = dest_t + jnp.where(idx_t == e, pstart[e], 0)
    n_blocks = (t * TOP_K) // ROW_BLOCK + N_EXPERTS
    n_rows = n_blocks * ROW_BLOCK
    block_row0 = jnp.arange(n_blocks, dtype=jnp.int32) * ROW_BLOCK
    block_expert = jnp.minimum(
        jnp.sum((cum[None, :] <= block_row0[:, None]).astype(jnp.int32), axis=1), N_EXPERTS - 1)
    block_valid = jnp.clip(counts[block_expert] - (block_row0 - pstart[block_expert]), 0, ROW_BLOCK)
    present = counts > 0
    slot_e = (jnp.cumsum(present.astype(jnp.int32)) - 1) % 2
    later = lax.cummin(jnp.where(present, experts, N_EXPERTS)[::-1])[::-1]
    next_e = jnp.concatenate([later[1:], jnp.full((1,), N_EXPERTS, jnp.int32)])
    next_e = jnp.where(next_e == N_EXPERTS, -1, next_e)

    sub = dest_t[:, None, :] + (jnp.arange(SUBROWS, dtype=jnp.int32) * n_rows)[None, :, None]
    idx_gather = sub.reshape(SC_WORKERS, -1, LANES)
    idx_dispatch = sub.reshape(TOP_K, SUBROWS * t // LANES, LANES).transpose(1, 0, 2)
    idx_dispatch = idx_dispatch.reshape(SC_WORKERS, -1, LANES)

    x_tab = _sc_dispatch(xn2p.reshape(SUBROWS * t, LANES), idx_dispatch, SUBROWS * n_rows)
    y_tab = _expert_call(
        block_expert, block_valid, next_e[block_expert], slot_e[block_expert],
        x_tab.reshape(SUBROWS, n_rows, LANES),
        w_gate[0], b_gate[0][:, None, :], w_up[0], b_up[0][:, None, :],
        w_down[0], b_down[0][:, None, :])
    y_slots = _sc_gather(y_tab.reshape(SUBROWS * n_rows, LANES), idx_gather)
    return _combine_call(h, route, row(norm_final_g),
                         y_slots.reshape(TOP_K, SUBROWS, t, LANES), nb, seq)
```

```python
import jax
import jax.numpy as jnp
from jax import lax
from jax.experimental import pallas as pl
from jax.experimental.pallas import tpu as pltpu
from jax.experimental.pallas import tpu_sc as plsc

F32 = jnp.float32
BF16 = jnp.bfloat16

NORM_EPS = 1e-5
D_MODEL = 1024
SSM_WIDTH = 512
SSM_GROUP = 16
SSM_GROUPS = 32
SSM_STATE = 64
SGU_WIDTH = 512
SGU_BLOCK = 128
SGU_HEADS = 4
SGU_HEAD_DIM = 128
CHUNK = 64
N_EXPERTS = 32
TOP_K = 4
SWIGLU_ALPHA = 1.702
SWIGLU_LIMIT = 7.0

LANES = 128
SUBLANES = 8
TIME_TILE = 128
ROW_BLOCK = 512
VMEM_LIMIT = 56 * 1024 * 1024
NEG_BIG = -1e30
SC_CORES = 2
SC_SUBCORES = 16
SC_WORKERS = SC_CORES * SC_SUBCORES
SUBROWS = 4
ROUTE_ROWS = 16
PACKED = D_MODEL // 2


def _pack_rows(x, out_ref):
    lo = lax.bitcast_convert_type(x[:, :PACKED].astype(BF16).astype(F32), jnp.int32)
    hi = lax.bitcast_convert_type(x[:, PACKED:].astype(BF16).astype(F32), jnp.int32)
    words = lax.shift_right_logical(lo, 16) | (hi & jnp.int32(-65536))
    for s in range(SUBROWS):
        out_ref[s] = words[:, s * LANES:(s + 1) * LANES]


def _unpack_rows(ref):
    lo, hi = [], []
    for s in range(SUBROWS):
        w = ref[s]
        lo.append(lax.bitcast_convert_type(lax.shift_left(w, 16), F32))
        hi.append(lax.bitcast_convert_type(w & jnp.int32(-65536), F32))
    return lo, hi


def _dot(a, b):
    return jnp.dot(a, b, preferred_element_type=F32)


def _sigmoid(x):
    return 1.0 / (1.0 + jnp.exp(-x))


def _gelu(x):
    return 0.5 * x * (1.0 + jnp.tanh(0.7978845608028654 * (x + 0.044715 * (x * x * x))))


def _rmsnorm(x, g):
    return x * lax.rsqrt(jnp.mean(x * x, axis=-1, keepdims=True) + NORM_EPS) * g


def _proj_kernel(x_ref, g_ref, w_ref, lng_ref, lnb_ref, ws_ref, bs_ref, wb_ref,
                 xa_ref, sga_ref, pb_ref, yb_scr):
    nb, tl, d = x_ref.shape
    m = nb * tl
    x = x_ref[...].reshape(m, d)
    xn = _rmsnorm(x, g_ref[...]).astype(BF16)

    xa = _dot(xn, w_ref[:, 0:SSM_WIDTH])
    xa_ref[...] = xa.astype(BF16).reshape(nb, tl, SSM_WIDTH)

    s_b = SSM_WIDTH + 2 * SGU_WIDTH
    s_g = s_b + D_MODEL
    ga = _dot(xn, w_ref[:, s_b:s_g])
    sga_ref[...] = _sigmoid(ga).astype(BF16).reshape(nb, tl, D_MODEL)

    z = _gelu(_dot(xn, w_ref[:, SSM_WIDTH:s_b]))
    u = z[:, :SGU_WIDTH]
    v = z[:, SGU_WIDTH:]
    mu = jnp.mean(v, axis=-1, keepdims=True)
    vc = v - mu
    v = vc * lax.rsqrt(jnp.mean(vc * vc, axis=-1, keepdims=True) + NORM_EPS)
    v = (v * lng_ref[...] + lnb_ref[...]).astype(BF16)

    ri = lax.broadcasted_iota(jnp.int32, (SGU_BLOCK, SGU_BLOCK), 0) // CHUNK
    ci = lax.broadcasted_iota(jnp.int32, (SGU_BLOCK, SGU_BLOCK), 1) // CHUNK
    causal = ri >= ci
    for h in range(SGU_HEADS):
        wm = jnp.where(causal, ws_ref[h], 0.0).astype(BF16)
        bias = bs_ref[:, h:h + 1]
        lo = h * SGU_HEAD_DIM
        for blk in range(m // SGU_BLOCK):
            r0 = blk * SGU_BLOCK
            s = _dot(wm, v[r0:r0 + SGU_BLOCK, lo:lo + SGU_HEAD_DIM]) + bias
            yb_scr[r0:r0 + SGU_BLOCK, lo:lo + SGU_HEAD_DIM] = (
                u[r0:r0 + SGU_BLOCK, lo:lo + SGU_HEAD_DIM] * s).astype(BF16)

    gb = _dot(xn, w_ref[:, s_g:])
    pb = _sigmoid(gb) * _dot(yb_scr[...], wb_ref[...])
    pb_ref[...] = pb.astype(BF16).reshape(nb, tl, D_MODEL)


def _proj_call(x, g, w_in, ln_g, ln_b, w_s, b_s_t, w_b):
    nb, seq, d = x.shape
    tl = TIME_TILE
    m = nb * tl
    const = lambda shape: pl.BlockSpec(shape, lambda i: (0,) * len(shape))
    tile = lambda width: pl.BlockSpec((nb, tl, width), lambda i: (0, i, 0))
    return pl.pallas_call(
        _proj_kernel,
        grid=(seq // tl,),
        in_specs=[tile(d), const((1, d)), const(w_in.shape), const((1, SGU_WIDTH)),
                  const((1, SGU_WIDTH)), const(w_s.shape), const(b_s_t.shape), const(w_b.shape)],
        out_specs=[tile(SSM_WIDTH), tile(D_MODEL), tile(D_MODEL)],
        out_shape=[jax.ShapeDtypeStruct((nb, seq, SSM_WIDTH), BF16),
                   jax.ShapeDtypeStruct((nb, seq, D_MODEL), BF16),
                   jax.ShapeDtypeStruct((nb, seq, D_MODEL), BF16)],
        scratch_shapes=[pltpu.VMEM((m, SGU_WIDTH), BF16)],
        compiler_params=pltpu.CompilerParams(
            dimension_semantics=("arbitrary",), vmem_limit_bytes=VMEM_LIMIT),
        name="proj",
    )(x, g, w_in, ln_g, ln_b, w_s, b_s_t, w_b)


def _mix_kernel(xa_ref, sga_ref, pb_ref, x_ref, upper_ref, bd_ref, cd_ref,
                are_ref, aim_ref, dskip_ref, wglu_ref, bglu_ref, wa_ref, wout_ref,
                g2_ref, wr_ref, br_ref,
                h_ref, xn2_ref, routet_ref, cnt_ref,
                bu0_scr, bu1_scr, x8_scr, y8_scr, state_scr, cnt_scr):
    nb, tl, d = x_ref.shape
    m = nb * tl

    @pl.when(pl.program_id(0) == 0)
    def _():
        state_scr[...] = jnp.zeros_like(state_scr)
        cnt_scr[...] = jnp.zeros_like(cnt_scr)
        x8_scr[...] = jnp.zeros_like(x8_scr)

    xa = xa_ref[...].reshape(m, SSM_WIDTH).astype(F32)

    def chunk_lo(p, hf):
        return hf * (SSM_WIDTH // 2) + p * LANES

    scans = (bu0_scr, bu1_scr)
    for p in range(2):
        for hf in range(2):
            lo = chunk_lo(p, hf)
            for b in range(nb):
                x8_scr[2 * p + hf, pl.ds(hf * nb + b, tl, stride=SUBLANES), :] = (
                    xa[b * tl:(b + 1) * tl, lo:lo + LANES])
        x8 = jnp.concatenate([x8_scr[2 * p], x8_scr[2 * p + 1]], axis=1).astype(BF16)
        scans[p][...] = _dot(x8, bd_ref[p])

    a_re = [are_ref[p] for p in range(2)]
    a_im = [aim_ref[p] for p in range(2)]
    s_re = [state_scr[p][:, :SSM_WIDTH] for p in range(2)]
    s_im = [state_scr[p][:, SSM_WIDTH:] for p in range(2)]
    for t in range(tl):
        r = t * SUBLANES
        for p in range(2):
            bu_scr = scans[p]
            n_re = a_re[p] * s_re[p] - a_im[p] * s_im[p] + bu_scr[r:r + SUBLANES, 0:SSM_WIDTH]
            n_im = (a_re[p] * s_im[p] + a_im[p] * s_re[p]
                    + bu_scr[r:r + SUBLANES, SSM_WIDTH:2 * SSM_WIDTH])
            bu_scr[r:r + SUBLANES, 0:SSM_WIDTH] = n_re
            bu_scr[r:r + SUBLANES, SSM_WIDTH:2 * SSM_WIDTH] = n_im
            s_re[p], s_im[p] = n_re, n_im

    y_chunks = {}
    for p in range(2):
        state_scr[p] = jnp.concatenate([s_re[p], s_im[p]], axis=1)
        y8 = _dot(scans[p][...].astype(BF16), cd_ref[p])
        for hf in range(2):
            y8_scr[2 * p + hf] = y8[:, hf * LANES:(hf + 1) * LANES]
            y_chunks[(hf, p)] = jnp.concatenate(
                [y8_scr[2 * p + hf, pl.ds(hf * nb + b, tl, stride=SUBLANES), :]
                 for b in range(nb)], axis=0)
    y = jnp.concatenate([y_chunks[(hf, p)] for hf in range(2) for p in range(2)], axis=1)
    y = y + dskip_ref[...] * xa
    z = _gelu(y)
    ya = z * _sigmoid(_dot(z.astype(BF16), wglu_ref[...]) + bglu_ref[...])
    pa = sga_ref[...].reshape(m, d).astype(F32) * _dot(ya.astype(BF16), wa_ref[...])
    merged = pa + pb_ref[...].reshape(m, d).astype(F32)
    h = x_ref[...].reshape(m, d) + _dot(merged.astype(BF16), wout_ref[...])
    h_ref[...] = h
    xn2 = _rmsnorm(h, g2_ref[...])
    _pack_rows(xn2, xn2_ref)

    logits = _dot(xn2.astype(BF16), wr_ref[...]) + br_ref[...]
    work = jnp.transpose(logits)[:N_EXPERTS]
    expert = lax.broadcasted_iota(jnp.int32, (N_EXPERTS, m), 0).astype(F32)
    vals, idxs = [], []
    member = jnp.zeros((N_EXPERTS, m), F32)
    for _ in range(TOP_K):
        mx = jnp.max(work, axis=0, keepdims=True)
        ix = jnp.min(jnp.where(work == mx, expert, float(N_EXPERTS)), axis=0, keepdims=True)
        hit = expert == ix
        member = jnp.where(hit, 1.0, member)
        work = jnp.where(hit, NEG_BIG, work)
        vals.append(mx)
        idxs.append(ix)
    exps = [jnp.exp(v - vals[0]) for v in vals]
    denom = exps[0] + exps[1] + exps[2] + exps[3]
    gates = [e / denom for e in exps]

    before = _dot(member.astype(BF16), upper_ref[...]) + cnt_scr[:, 0:1]
    new_cnt = cnt_scr[:, 0:1] + jnp.sum(member, axis=1, keepdims=True)
    cnt_scr[...] = jnp.broadcast_to(new_cnt, cnt_scr.shape)
    cnt_ref[...] = jnp.broadcast_to(new_cnt, cnt_ref.shape)
    ranks = [jnp.sum(jnp.where(expert == ix, before, 0.0), axis=0, keepdims=True) for ix in idxs]
    pad = jnp.zeros((ROUTE_ROWS - 3 * TOP_K, m), F32)
    routet_ref[...] = jnp.concatenate(idxs + gates + ranks + [pad], axis=0)


def _mix_call(xa, sga, pb, x, upper, bd, cd, a_re, a_im, dskip, w_glu, b_glu, w_a, w_out,
              g2, w_r, b_r):
    nb, seq, d = x.shape
    tl = TIME_TILE
    m = nb * tl
    n_tiles = seq // tl
    const = lambda shape: pl.BlockSpec(shape, lambda i: (0,) * len(shape))
    tile = lambda width: pl.BlockSpec((nb, tl, width), lambda i: (0, i, 0))
    rows = lambda width: pl.BlockSpec((m, width), lambda i: (i, 0))
    operands = (xa, sga, pb, x, upper, bd, cd, a_re, a_im, dskip, w_glu, b_glu, w_a, w_out,
                g2, w_r, b_r)
    in_specs = [tile(SSM_WIDTH), tile(d), tile(d), tile(d)] + [const(o.shape) for o in operands[4:]]
    return pl.pallas_call(
        _mix_kernel,
        grid=(n_tiles,),
        in_specs=in_specs,
        out_specs=[rows(d), pl.BlockSpec((SUBROWS, m, LANES), lambda i: (0, i, 0)),
                   pl.BlockSpec((ROUTE_ROWS, m), lambda i: (0, i)), const((N_EXPERTS, LANES))],
        out_shape=[jax.ShapeDtypeStruct((n_tiles * m, d), F32),
                   jax.ShapeDtypeStruct((SUBROWS, n_tiles * m, LANES), jnp.int32),
                   jax.ShapeDtypeStruct((ROUTE_ROWS, n_tiles * m), F32),
                   jax.ShapeDtypeStruct((N_EXPERTS, LANES), F32)],
        scratch_shapes=[pltpu.VMEM((SUBLANES * tl, 2 * SSM_WIDTH), F32),
                        pltpu.VMEM((SUBLANES * tl, 2 * SSM_WIDTH), F32),
                        pltpu.VMEM((4, SUBLANES * tl, LANES), F32),
                        pltpu.VMEM((4, SUBLANES * tl, LANES), F32),
                        pltpu.VMEM((2, SUBLANES, 2 * SSM_WIDTH), F32),
                        pltpu.VMEM((N_EXPERTS, LANES), F32)],
        compiler_params=pltpu.CompilerParams(
            dimension_semantics=("arbitrary",), vmem_limit_bytes=VMEM_LIMIT),
        name="mix",
    )(*operands)


def _sc_mesh():
    return plsc.VectorSubcoreMesh(core_axis_name="c", subcore_axis_name="s")


def _worker_id():
    return lax.axis_index("s") * SC_CORES + lax.axis_index("c")


def _dispatch_body(src_hbm, idx_hbm, out_hbm, idx_v, buf, sem):
    n_chunks = idx_v.shape[0] // TOP_K
    wid = _worker_id()
    base = wid * (n_chunks * LANES)
    pltpu.sync_copy(idx_hbm.at[wid], idx_v)

    @pl.loop(0, n_chunks)
    def _(j):
        pltpu.sync_copy(src_hbm.at[pl.ds(base + j * LANES, LANES)], buf)
        copies = [pltpu.async_copy(buf, out_hbm.at[idx_v.at[j * TOP_K + k]], sem)
                  for k in range(TOP_K)]
        for c in copies:
            c.wait()


def _sc_dispatch(src, idx, n_out):
    n_chunks = src.shape[0] // (SC_WORKERS * LANES)
    return pl.kernel(
        _dispatch_body,
        out_type=jax.ShapeDtypeStruct((n_out, LANES), jnp.int32),
        mesh=_sc_mesh(),
        scratch_types=[pltpu.VMEM((n_chunks * TOP_K, LANES), jnp.int32),
                       pltpu.VMEM((LANES, LANES), jnp.int32),
                       pltpu.SemaphoreType.DMA],
        name="sc_dispatch",
    )(src, idx)


def _gather_body(tab_hbm, idx_hbm, out_hbm, idx_v, buf0, buf1, sem0, sem1):
    n_chunks = idx_v.shape[0]
    wid = _worker_id()
    base = wid * (n_chunks * LANES)
    pltpu.sync_copy(idx_hbm.at[wid], idx_v)

    @pl.loop(0, n_chunks, step=2)
    def _(j):
        c0 = pltpu.async_copy(tab_hbm.at[idx_v.at[j]], buf0, sem0)
        c1 = pltpu.async_copy(tab_hbm.at[idx_v.at[j + 1]], buf1, sem1)
        c0.wait()
        pltpu.sync_copy(buf0, out_hbm.at[pl.ds(base + j * LANES, LANES)])
        c1.wait()
        pltpu.sync_copy(buf1, out_hbm.at[pl.ds(base + (j + 1) * LANES, LANES)])


def _sc_gather(tab, idx):
    n_chunks = idx.shape[1]
    return pl.kernel(
        _gather_body,
        out_type=jax.ShapeDtypeStruct((SC_WORKERS * n_chunks * LANES, LANES), jnp.int32),
        mesh=_sc_mesh(),
        scratch_types=[pltpu.VMEM((n_chunks, LANES), jnp.int32),
                       pltpu.VMEM((LANES, LANES), jnp.int32),
                       pltpu.VMEM((LANES, LANES), jnp.int32),
                       pltpu.SemaphoreType.DMA, pltpu.SemaphoreType.DMA],
        name="sc_gather",
    )(tab, idx)


def _expert_kernel(be_ref, bv_ref, nx_ref, sl_ref, x_ref, bg_ref, bu_ref, bd_ref,
                   wg_hbm, wu_hbm, wd_hbm, y_ref, stage, wg_scr, wu_scr, wd_scr, sems):
    i = pl.program_id(0)
    valid = bv_ref[i]
    first = jnp.logical_and(
        valid > 0, jnp.logical_or(i == 0, be_ref[i] != be_ref[jnp.maximum(i - 1, 0)]))

    def weight_copies(expert, slot):
        return [pltpu.make_async_copy(w.at[expert], stage.at[slot, j], sems.at[slot, j])
                for j, w in enumerate((wg_hbm, wu_hbm, wd_hbm))]

    @pl.when(i == 0)
    def _():
        for c in weight_copies(be_ref[0], sl_ref[0]):
            c.start()

    @pl.when(first)
    def _():
        slot = sl_ref[i]
        for c in weight_copies(be_ref[i], slot):
            c.wait()
        wg_scr[...] = stage[slot, 0].astype(BF16)
        wu_scr[...] = stage[slot, 1].astype(BF16)
        wd_scr[...] = stage[slot, 2].astype(BF16)

        @pl.when(nx_ref[i] >= 0)
        def _():
            for c in weight_copies(nx_ref[i], 1 - slot):
                c.start()

    @pl.when(valid > 0)
    def _():
        lo, hi = _unpack_rows(x_ref)
        live = lax.broadcasted_iota(jnp.int32, (ROW_BLOCK, 1), 0) < valid
        x = jnp.where(live, jnp.concatenate(lo + hi, axis=1), 0.0).astype(BF16)
        g = _dot(x, wg_scr[...]) + bg_ref[...]
        l = _dot(x, wu_scr[...]) + bu_ref[...]
        g = jnp.minimum(g, SWIGLU_LIMIT)
        l = jnp.clip(l, -SWIGLU_LIMIT, SWIGLU_LIMIT)
        hid = g * _sigmoid(SWIGLU_ALPHA * g) * (l + 1.0)
        _pack_rows(_dot(hid.astype(BF16), wd_scr[...]) + bd_ref[...], y_ref)

    @pl.when(valid <= 0)
    def _():
        y_ref[...] = jnp.zeros_like(y_ref)


def _expert_call(block_expert, block_valid, next_expert, slot, x_tab,
                 w_gate, b_gate, w_up, b_up, w_down, b_down):
    d, f = w_gate.shape[-2:]
    assert d == f
    n_blocks = block_expert.shape[0]
    bspec = lambda width: pl.BlockSpec((None, 1, width), lambda i, be, bv, nx, sl: (be[i], 0, 0))
    rows = pl.BlockSpec((SUBROWS, ROW_BLOCK, LANES), lambda i, be, bv, nx, sl: (0, i, 0))
    hbm = pl.BlockSpec(memory_space=pl.ANY)
    grid_spec = pltpu.PrefetchScalarGridSpec(
        num_scalar_prefetch=4,
        grid=(n_blocks,),
        in_specs=[rows, bspec(f), bspec(f), bspec(d), hbm, hbm, hbm],
        out_specs=rows,
        scratch_shapes=[pltpu.VMEM((2, 3, d, f), F32),
                        pltpu.VMEM((d, f), BF16), pltpu.VMEM((d, f), BF16),
                        pltpu.VMEM((f, d), BF16), pltpu.SemaphoreType.DMA((2, 3))],
    )
    return pl.pallas_call(
        _expert_kernel,
        grid_spec=grid_spec,
        out_shape=jax.ShapeDtypeStruct(x_tab.shape, jnp.int32),
        compiler_params=pltpu.CompilerParams(
            dimension_semantics=("arbitrary",), vmem_limit_bytes=VMEM_LIMIT),
        name="expert",
    )(block_expert, block_valid, next_expert, slot, x_tab, b_gate, b_up, b_down,
      w_gate, w_up, w_down)


def _combine_kernel(h_ref, routet_ref, gf_ref, ys_ref, out_ref):
    nb, tl, d = out_ref.shape
    m = nb * tl
    h = h_ref[...]
    route = jnp.transpose(jnp.concatenate(
        [routet_ref[...], jnp.zeros((LANES - ROUTE_ROWS, m), F32)], axis=0))
    lo_acc = [h[:, s * LANES:(s + 1) * LANES] for s in range(SUBROWS)]
    hi_acc = [h[:, PACKED + s * LANES:PACKED + (s + 1) * LANES] for s in range(SUBROWS)]
    for k in range(TOP_K):
        gate = route[:, TOP_K + k:TOP_K + k + 1]
        lo, hi = _unpack_rows(ys_ref.at[k])
        lo_acc = [a + gate * v for a, v in zip(lo_acc, lo)]
        hi_acc = [a + gate * v for a, v in zip(hi_acc, hi)]
    acc = jnp.concatenate(lo_acc + hi_acc, axis=1)
    out_ref[...] = _rmsnorm(acc, gf_ref[...]).reshape(nb, tl, d)


def _combine_call(h, route_t, g_final, y_slots, nb, seq):
    t, d = h.shape
    tl = TIME_TILE
    m = nb * tl
    return pl.pallas_call(
        _combine_kernel,
        grid=(t // m,),
        in_specs=[pl.BlockSpec((m, d), lambda i: (i, 0)),
                  pl.BlockSpec((ROUTE_ROWS, m), lambda i: (0, i)),
                  pl.BlockSpec((1, d), lambda i: (0, 0)),
                  pl.BlockSpec((TOP_K, SUBROWS, m, LANES), lambda i: (0, 0, i, 0))],
        out_specs=pl.BlockSpec((nb, tl, d), lambda i: (0, i, 0)),
        out_shape=jax.ShapeDtypeStruct((nb, seq, d), F32),
        compiler_params=pltpu.CompilerParams(
            dimension_semantics=("arbitrary",), vmem_limit_bytes=VMEM_LIMIT),
        name="combine",
    )(h, route_t, g_final, y_slots)


def _s5_operands(lam_re, lam_im, log_dt, b_re, b_im, c_re, c_im, nb):
    gp, ns, pg = SSM_GROUPS, SSM_STATE, SSM_GROUP
    lam = lax.complex(lam_re.astype(F32), lam_im.astype(F32))
    dt = jnp.exp(log_dt.astype(F32))[:, None]
    lam_bar = jnp.exp(lam * dt)
    b_bar = ((lam_bar - 1.0) / lam)[..., None] * lax.complex(b_re.astype(F32), b_im.astype(F32))
    grp = lambda hf, p: slice(16 * hf + 8 * p, 16 * hf + 8 * p + 8)
    eye8 = jnp.eye(8, dtype=F32)

    def b_block(bb):
        return jnp.einsum('gnq,gh->gqhn', bb, eye8).reshape(8 * pg, 8 * ns)

    def c_block(cc):
        return jnp.einsum('gqn,gh->gnhq', cc, eye8).reshape(8 * ns, 8 * pg)

    bd, cd, a_re, a_im = [], [], [], []
    for p in range(2):
        rows = []
        for hf in range(2):
            bb = b_bar[grp(hf, p)]
            rows.append(jnp.concatenate([b_block(bb.real), b_block(bb.imag)], axis=1))
        bd.append(jnp.concatenate(rows, axis=0))
        cols = []
        for hf in range(2):
            cols.append(jnp.concatenate([c_block(c_re[grp(hf, p)].astype(F32)),
                                         -c_block(c_im[grp(hf, p)].astype(F32))], axis=0))
        cd.append(jnp.concatenate(cols, axis=1))
        lam_rows = jnp.stack([lam_bar[grp(j // nb, p)].reshape(8 * ns) for j in range(SUBLANES)])
        a_re.append(lam_rows.real)
        a_im.append(lam_rows.imag)
    bd = jnp.stack(bd).astype(BF16)
    cd = jnp.stack(cd).astype(BF16)
    a_re = jnp.stack(a_re)
    a_im = jnp.stack(a_im)

    return bd, cd, a_re, a_im


def kernel(x, norm_mix_g, w_in, lam_re, lam_im, log_dt, b_re, b_im, c_re, c_im, d_skip, w_glu, b_glu, sgu_ln_g, sgu_ln_b, w_s, b_s, w_branch_a, w_branch_b, w_out, norm_moe_g, w_router, b_router, w_gate, b_gate, w_up, b_up, w_down, b_down, norm_final_g):
    nb, seq, d = x.shape
    assert d == D_MODEL and SUBLANES % nb == 0 and SUBLANES // nb == 2
    assert seq % TIME_TILE == 0 and norm_mix_g.shape[0] == 1
    assert (nb * seq * SUBROWS) % (SC_WORKERS * LANES) == 0
    assert (nb * seq * SUBROWS * TOP_K) % (SC_WORKERS * LANES * 2) == 0
    tl = TIME_TILE
    t = nb * seq
    row = lambda v: v.reshape(1, -1).astype(F32)

    xa, sga, pb = _proj_call(
        x, row(norm_mix_g[0]), w_in[0].astype(BF16), row(sgu_ln_g[0]), row(sgu_ln_b[0]),
        w_s[0].astype(F32), b_s[0].T.astype(F32), w_branch_b[0].astype(BF16))

    bd, cd, a_re, a_im = _s5_operands(
        lam_re[0], lam_im[0], log_dt[0], b_re[0], b_im[0], c_re[0], c_im[0], nb)
    w_r = jnp.zeros((d, LANES), F32).at[:, :N_EXPERTS].set(w_router[0].astype(F32)).astype(BF16)
    b_r = jnp.full((1, LANES), NEG_BIG, F32).at[0, :N_EXPERTS].set(b_router[0].astype(F32))
    m = nb * tl
    upper = (jnp.arange(m)[:, None] < jnp.arange(m)[None, :]).astype(BF16)
    h, xn2p, route_t, cnt = _mix_call(
        xa, sga, pb, x, upper, bd, cd, a_re, a_im, row(d_skip[0]),
        w_glu[0].astype(BF16), row(b_glu[0]), w_branch_a[0].astype(BF16), w_out[0].astype(BF16),
        row(norm_moe_g[0]), w_r, b_r)

    idx_t = route_t[0:TOP_K].astype(jnp.int32)
    rank_t = route_t[2 * TOP_K:3 * TOP_K].astype(jnp.int32)
    counts = cnt[:, 0].astype(jnp.int32)
    padded = (counts + ROW_BLOCK - 1) // ROW_BLOCK * ROW_BLOCK
    cum = jnp.cumsum(padded)
    pstart = cum - padded
    experts = jnp.arange(N_EXPERTS, dtype=jnp.int32)
    dest_t = rank_t
    for e in range(N_EXPERTS):
        dest_t = dest_t + jnp.where(idx_t == e, pstart[e], 0)
    n_blocks = (t * TOP_K) // ROW_BLOCK + N_EXPERTS
    n_rows = n_blocks * ROW_BLOCK
    block_row0 = jnp.arange(n_blocks, dtype=jnp.int32) * ROW_BLOCK
    block_expert = jnp.minimum(
        jnp.sum((cum[None, :] <= block_row0[:, None]).astype(jnp.int32), axis=1), N_EXPERTS - 1)
    block_valid = jnp.clip(counts[block_expert] - (block_row0 - pstart[block_expert]), 0, ROW_BLOCK)
    present = counts > 0
    slot_e = (jnp.cumsum(present.astype(jnp.int32)) - 1) % 2
    later = lax.cummin(jnp.where(present, experts, N_EXPERTS)[::-1])[::-1]
    next_e = jnp.concatenate([later[1:], jnp.full((1,), N_EXPERTS, jnp.int32)])
    next_e = jnp.where(next_e == N_EXPERTS, -1, next_e)

    sub = dest_t[:, None, :] + (jnp.arange(SUBROWS, dtype=jnp.int32) * n_rows)[None, :, None]
    idx_gather = sub.reshape(SC_WORKERS, -1, LANES)
    idx_dispatch = sub.reshape(TOP_K, SUBROWS * t // LANES, LANES).transpose(1, 0, 2)
    idx_dispatch = idx_dispatch.reshape(SC_WORKERS, -1, LANES)

    x_tab = _sc_dispatch(xn2p.reshape(SUBROWS * t, LANES), idx_dispatch, SUBROWS * n_rows)
    y_tab = _expert_call(
        block_expert, block_valid, next_e[block_expert], slot_e[block_expert],
        x_tab.reshape(SUBROWS, n_rows, LANES),
        w_gate[0], b_gate[0][:, None, :], w_up[0], b_up[0][:, None, :],
        w_down[0], b_down[0][:, None, :])
    y_slots = _sc_gather(y_tab.reshape(SUBROWS * n_rows, LANES), idx_gather)
    return _combine_call(h, route_t, row(norm_final_g),
                         y_slots.reshape(TOP_K, SUBROWS, t, LANES), nb, seq)
```

```python
import math

import jax
import jax.numpy as jnp
from jax import lax
from jax.experimental import pallas as pl
from jax.experimental.pallas import tpu as pltpu
from jax.experimental.pallas import tpu_sc as plsc

F32 = jnp.float32
BF16 = jnp.bfloat16
F8 = jnp.float8_e4m3fn
F8_TARGET = 240.0
TINY = 1e-30

NORM_EPS = 1e-5
D_MODEL = 1024
SSM_WIDTH = 512
SSM_GROUP = 16
SSM_GROUPS = 32
SSM_STATE = 64
SGU_WIDTH = 512
SGU_BLOCK = 128
SGU_HEADS = 4
SGU_HEAD_DIM = 128
CHUNK = 64
N_EXPERTS = 32
TOP_K = 4
SWIGLU_ALPHA = 1.702
SWIGLU_LIMIT = 7.0
LOG2_E = 1.4426950408889634
HID_SCALE = 4.0

LANES = 128
SUBLANES = 8
TIME_TILE = 128
ROW_BLOCK = 512
VMEM_LIMIT = 56 * 1024 * 1024
NEG_BIG = -1e30
SC_CORES = 2
SC_SUBCORES = 16
SC_WORKERS = SC_CORES * SC_SUBCORES
SUBROWS = 4
ROUTE_ROWS = 16
PACKED = D_MODEL // 2


def _pack_rows(x, out_ref):
    lo = lax.bitcast_convert_type(x[:, :PACKED].astype(BF16).astype(F32), jnp.int32)
    hi = lax.bitcast_convert_type(x[:, PACKED:].astype(BF16).astype(F32), jnp.int32)
    words = lax.shift_right_logical(lo, 16) | (hi & jnp.int32(-65536))
    for s in range(SUBROWS):
        out_ref[s] = words[:, s * LANES:(s + 1) * LANES]


def _unpack_rows(ref):
    lo, hi = [], []
    for s in range(SUBROWS):
        w = ref[s]
        lo.append(lax.bitcast_convert_type(lax.shift_left(w, 16), F32))
        hi.append(lax.bitcast_convert_type(w & jnp.int32(-65536), F32))
    return lo, hi


def _dot(a, b):
    return jnp.dot(a, b, preferred_element_type=F32)


def _sigmoid(x):
    return 1.0 / (1.0 + jnp.exp(-x))


def _gelu(x):
    return 0.5 * x * (1.0 + jnp.tanh(0.7978845608028654 * (x + 0.044715 * (x * x * x))))


def _rmsnorm(x, g):
    return x * lax.rsqrt(jnp.mean(x * x, axis=-1, keepdims=True) + NORM_EPS) * g


def _proj_kernel(x_ref, g_ref, w_ref, lng_ref, lnb_ref, ws_ref, bs_ref, wb_ref,
                 xa_ref, sga_ref, pb_ref, yb_scr):
    nb, tl, d = x_ref.shape
    m = nb * tl
    x = x_ref[...].reshape(m, d)
    xn = _rmsnorm(x, g_ref[...]).astype(BF16)

    xa = _dot(xn, w_ref[:, 0:SSM_WIDTH])
    xa_ref[...] = xa.astype(BF16).reshape(nb, tl, SSM_WIDTH)

    s_b = SSM_WIDTH + 2 * SGU_WIDTH
    s_g = s_b + D_MODEL
    ga = _dot(xn, w_ref[:, s_b:s_g])
    sga_ref[...] = _sigmoid(ga).astype(BF16).reshape(nb, tl, D_MODEL)

    z = _gelu(_dot(xn, w_ref[:, SSM_WIDTH:s_b]))
    u = z[:, :SGU_WIDTH]
    v = z[:, SGU_WIDTH:]
    mu = jnp.mean(v, axis=-1, keepdims=True)
    vc = v - mu
    v = vc * lax.rsqrt(jnp.mean(vc * vc, axis=-1, keepdims=True) + NORM_EPS)
    v = (v * lng_ref[...] + lnb_ref[...]).astype(BF16)

    ri = lax.broadcasted_iota(jnp.int32, (SGU_BLOCK, SGU_BLOCK), 0) // CHUNK
    ci = lax.broadcasted_iota(jnp.int32, (SGU_BLOCK, SGU_BLOCK), 1) // CHUNK
    causal = ri >= ci
    for h in range(SGU_HEADS):
        wm = jnp.where(causal, ws_ref[h], 0.0).astype(BF16)
        bias = bs_ref[:, h:h + 1]
        lo = h * SGU_HEAD_DIM
        for blk in range(m // SGU_BLOCK):
            r0 = blk * SGU_BLOCK
            s = _dot(wm, v[r0:r0 + SGU_BLOCK, lo:lo + SGU_HEAD_DIM]) + bias
            yb_scr[r0:r0 + SGU_BLOCK, lo:lo + SGU_HEAD_DIM] = (
                u[r0:r0 + SGU_BLOCK, lo:lo + SGU_HEAD_DIM] * s).astype(BF16)

    gb = _dot(xn, w_ref[:, s_g:])
    pb = _sigmoid(gb) * _dot(yb_scr[...], wb_ref[...])
    pb_ref[...] = pb.astype(BF16).reshape(nb, tl, D_MODEL)


def _proj_call(x, g, w_in, ln_g, ln_b, w_s, b_s_t, w_b):
    nb, seq, d = x.shape
    tl = TIME_TILE
    m = nb * tl
    const = lambda shape: pl.BlockSpec(shape, lambda i: (0,) * len(shape))
    tile = lambda width: pl.BlockSpec((nb, tl, width), lambda i: (0, i, 0))
    return pl.pallas_call(
        _proj_kernel,
        grid=(seq // tl,),
        in_specs=[tile(d), const((1, d)), const(w_in.shape), const((1, SGU_WIDTH)),
                  const((1, SGU_WIDTH)), const(w_s.shape), const(b_s_t.shape), const(w_b.shape)],
        out_specs=[tile(SSM_WIDTH), tile(D_MODEL), tile(D_MODEL)],
        out_shape=[jax.ShapeDtypeStruct((nb, seq, SSM_WIDTH), BF16),
                   jax.ShapeDtypeStruct((nb, seq, D_MODEL), BF16),
                   jax.ShapeDtypeStruct((nb, seq, D_MODEL), BF16)],
        scratch_shapes=[pltpu.VMEM((m, SGU_WIDTH), BF16)],
        compiler_params=pltpu.CompilerParams(
            dimension_semantics=("arbitrary",), vmem_limit_bytes=VMEM_LIMIT),
        name="proj",
    )(x, g, w_in, ln_g, ln_b, w_s, b_s_t, w_b)


def _mix_kernel(xa_ref, sga_ref, pb_ref, x_ref, upper_ref, bd_ref, cd_ref,
                are_ref, aim_ref, dskip_ref, wglu_ref, bglu_ref, wa_ref, wout_ref,
                g2_ref, xs_ref, wr_ref, br_ref,
                h_ref, xn2_ref, routet_ref, cnt_ref,
                bu0_scr, bu1_scr, x8_scr, y8_scr, state_scr, cnt_scr):
    nb, tl, d = x_ref.shape
    m = nb * tl

    @pl.when(pl.program_id(0) == 0)
    def _():
        state_scr[...] = jnp.zeros_like(state_scr)
        cnt_scr[...] = jnp.zeros_like(cnt_scr)
        x8_scr[...] = jnp.zeros_like(x8_scr)

    xa = xa_ref[...].reshape(m, SSM_WIDTH).astype(F32)

    def chunk_lo(p, hf):
        return hf * (SSM_WIDTH // 2) + p * LANES

    scans = (bu0_scr, bu1_scr)
    for p in range(2):
        for hf in range(2):
            lo = chunk_lo(p, hf)
            for b in range(nb):
                x8_scr[2 * p + hf, pl.ds(hf * nb + b, tl, stride=SUBLANES), :] = (
                    xa[b * tl:(b + 1) * tl, lo:lo + LANES])
        x8 = jnp.concatenate([x8_scr[2 * p], x8_scr[2 * p + 1]], axis=1).astype(BF16)
        scans[p][...] = _dot(x8, bd_ref[p])

    a_re = [are_ref[p] for p in range(2)]
    a_im = [aim_ref[p] for p in range(2)]
    s_re = [state_scr[p][:, :SSM_WIDTH] for p in range(2)]
    s_im = [state_scr[p][:, SSM_WIDTH:] for p in range(2)]
    for t in range(tl):
        r = t * SUBLANES
        for p in range(2):
            bu_scr = scans[p]
            n_re = a_re[p] * s_re[p] - a_im[p] * s_im[p] + bu_scr[r:r + SUBLANES, 0:SSM_WIDTH]
            n_im = (a_re[p] * s_im[p] + a_im[p] * s_re[p]
                    + bu_scr[r:r + SUBLANES, SSM_WIDTH:2 * SSM_WIDTH])
            bu_scr[r:r + SUBLANES, 0:SSM_WIDTH] = n_re
            bu_scr[r:r + SUBLANES, SSM_WIDTH:2 * SSM_WIDTH] = n_im
            s_re[p], s_im[p] = n_re, n_im

    y_chunks = {}
    for p in range(2):
        state_scr[p] = jnp.concatenate([s_re[p], s_im[p]], axis=1)
        y8 = _dot(scans[p][...].astype(BF16), cd_ref[p])
        for hf in range(2):
            y8_scr[2 * p + hf] = y8[:, hf * LANES:(hf + 1) * LANES]
            y_chunks[(hf, p)] = jnp.concatenate(
                [y8_scr[2 * p + hf, pl.ds(hf * nb + b, tl, stride=SUBLANES), :]
                 for b in range(nb)], axis=0)
    y = jnp.concatenate([y_chunks[(hf, p)] for hf in range(2) for p in range(2)], axis=1)
    y = y + dskip_ref[...] * xa
    z = _gelu(y)
    ya = z * _sigmoid(_dot(z.astype(BF16), wglu_ref[...]) + bglu_ref[...])
    pa = sga_ref[...].reshape(m, d).astype(F32) * _dot(ya.astype(BF16), wa_ref[...])
    merged = pa + pb_ref[...].reshape(m, d).astype(F32)
    h = x_ref[...].reshape(m, d) + _dot(merged.astype(BF16), wout_ref[...])
    h_ref[...] = h
    xn2 = _rmsnorm(h, g2_ref[...])
    _pack_rows(xn2 * xs_ref[0:1, 0:1], xn2_ref)

    logits = _dot(xn2.astype(BF16), wr_ref[...]) + br_ref[...]
    work = jnp.transpose(logits)[:N_EXPERTS]
    expert = lax.broadcasted_iota(jnp.int32, (N_EXPERTS, m), 0).astype(F32)
    vals, idxs = [], []
    member = jnp.zeros((N_EXPERTS, m), F32)
    for _ in range(TOP_K):
        mx = jnp.max(work, axis=0, keepdims=True)
        ix = jnp.min(jnp.where(work == mx, expert, float(N_EXPERTS)), axis=0, keepdims=True)
        hit = expert == ix
        member = jnp.where(hit, 1.0, member)
        work = jnp.where(hit, NEG_BIG, work)
        vals.append(mx)
        idxs.append(ix)
    exps = [jnp.exp(v - vals[0]) for v in vals]
    denom = exps[0] + exps[1] + exps[2] + exps[3]
    gates = [e / denom for e in exps]

    before = _dot(member.astype(BF16), upper_ref[...]) + cnt_scr[:, 0:1]
    new_cnt = cnt_scr[:, 0:1] + jnp.sum(member, axis=1, keepdims=True)
    cnt_scr[...] = jnp.broadcast_to(new_cnt, cnt_scr.shape)
    cnt_ref[...] = jnp.broadcast_to(new_cnt, cnt_ref.shape)
    ranks = [jnp.sum(jnp.where(expert == ix, before, 0.0), axis=0, keepdims=True) for ix in idxs]
    pad = jnp.zeros((ROUTE_ROWS - 3 * TOP_K, m), F32)
    routet_ref[...] = jnp.concatenate(idxs + gates + ranks + [pad], axis=0)


def _mix_call(xa, sga, pb, x, upper, bd, cd, a_re, a_im, dskip, w_glu, b_glu, w_a, w_out,
              g2, x_scale, w_r, b_r):
    nb, seq, d = x.shape
    tl = TIME_TILE
    m = nb * tl
    n_tiles = seq // tl
    const = lambda shape: pl.BlockSpec(shape, lambda i: (0,) * len(shape))
    tile = lambda width: pl.BlockSpec((nb, tl, width), lambda i: (0, i, 0))
    rows = lambda width: pl.BlockSpec((m, width), lambda i: (i, 0))
    operands = (xa, sga, pb, x, upper, bd, cd, a_re, a_im, dskip, w_glu, b_glu, w_a, w_out,
                g2, x_scale, w_r, b_r)
    in_specs = [tile(SSM_WIDTH), tile(d), tile(d), tile(d)] + [const(o.shape) for o in operands[4:]]
    return pl.pallas_call(
        _mix_kernel,
        grid=(n_tiles,),
        in_specs=in_specs,
        out_specs=[rows(d), pl.BlockSpec((SUBROWS, m, LANES), lambda i: (0, i, 0)),
                   pl.BlockSpec((ROUTE_ROWS, m), lambda i: (0, i)), const((N_EXPERTS, LANES))],
        out_shape=[jax.ShapeDtypeStruct((n_tiles * m, d), F32),
                   jax.ShapeDtypeStruct((SUBROWS, n_tiles * m, LANES), jnp.int32),
                   jax.ShapeDtypeStruct((ROUTE_ROWS, n_tiles * m), F32),
                   jax.ShapeDtypeStruct((N_EXPERTS, LANES), F32)],
        scratch_shapes=[pltpu.VMEM((SUBLANES * tl, 2 * SSM_WIDTH), F32),
                        pltpu.VMEM((SUBLANES * tl, 2 * SSM_WIDTH), F32),
                        pltpu.VMEM((4, SUBLANES * tl, LANES), F32),
                        pltpu.VMEM((4, SUBLANES * tl, LANES), F32),
                        pltpu.VMEM((2, SUBLANES, 2 * SSM_WIDTH), F32),
                        pltpu.VMEM((N_EXPERTS, LANES), F32)],
        compiler_params=pltpu.CompilerParams(
            dimension_semantics=("arbitrary",), vmem_limit_bytes=VMEM_LIMIT),
        name="mix",
    )(*operands)


def _sc_mesh():
    return plsc.VectorSubcoreMesh(core_axis_name="c", subcore_axis_name="s")


def _worker_id():
    return lax.axis_index("s") * SC_CORES + lax.axis_index("c")


def _dispatch_body(src_hbm, idx_hbm, out_hbm, idx_v, buf, sem):
    n_chunks = idx_v.shape[0] // TOP_K
    wid = _worker_id()
    base = wid * (n_chunks * LANES)
    pltpu.sync_copy(idx_hbm.at[wid], idx_v)

    @pl.loop(0, n_chunks)
    def _(j):
        pltpu.sync_copy(src_hbm.at[pl.ds(base + j * LANES, LANES)], buf)
        copies = [pltpu.async_copy(buf, out_hbm.at[idx_v.at[j * TOP_K + k]], sem)
                  for k in range(TOP_K)]
        for c in copies:
            c.wait()


def _sc_dispatch(src, idx, n_out):
    n_chunks = src.shape[0] // (SC_WORKERS * LANES)
    return pl.kernel(
        _dispatch_body,
        out_type=jax.ShapeDtypeStruct((n_out, LANES), jnp.int32),
        mesh=_sc_mesh(),
        scratch_types=[pltpu.VMEM((n_chunks * TOP_K, LANES), jnp.int32),
                       pltpu.VMEM((LANES, LANES), jnp.int32),
                       pltpu.SemaphoreType.DMA],
        name="sc_dispatch",
    )(src, idx)


def _gather_body(tab_hbm, idx_hbm, out_hbm, idx_v, buf0, buf1, sem0, sem1):
    n_chunks = idx_v.shape[0]
    wid = _worker_id()
    base = wid * (n_chunks * LANES)
    pltpu.sync_copy(idx_hbm.at[wid], idx_v)

    @pl.loop(0, n_chunks, step=2)
    def _(j):
        c0 = pltpu.async_copy(tab_hbm.at[idx_v.at[j]], buf0, sem0)
        c1 = pltpu.async_copy(tab_hbm.at[idx_v.at[j + 1]], buf1, sem1)
        c0.wait()
        pltpu.sync_copy(buf0, out_hbm.at[pl.ds(base + j * LANES, LANES)])
        c1.wait()
        pltpu.sync_copy(buf1, out_hbm.at[pl.ds(base + (j + 1) * LANES, LANES)])


def _sc_gather(tab, idx):
    n_chunks = idx.shape[1]
    return pl.kernel(
        _gather_body,
        out_type=jax.ShapeDtypeStruct((SC_WORKERS * n_chunks * LANES, LANES), jnp.int32),
        mesh=_sc_mesh(),
        scratch_types=[pltpu.VMEM((n_chunks, LANES), jnp.int32),
                       pltpu.VMEM((LANES, LANES), jnp.int32),
                       pltpu.VMEM((LANES, LANES), jnp.int32),
                       pltpu.SemaphoreType.DMA, pltpu.SemaphoreType.DMA],
        name="sc_gather",
    )(tab, idx)


def _expert_kernel(be_ref, bv_ref, nx_ref, sl_ref, x_ref, xs_ref, bg_ref, bu_ref, bd_ref,
                   wg_hbm, wu_hbm, wd_hbm, y_ref, stage, wg_scr, wu_scr, wd_scr, inv_scr, sems):
    i = pl.program_id(0)
    valid = bv_ref[i]
    first = jnp.logical_and(
        valid > 0, jnp.logical_or(i == 0, be_ref[i] != be_ref[jnp.maximum(i - 1, 0)]))

    def weight_copies(expert, slot):
        return [pltpu.make_async_copy(w.at[expert], stage.at[slot, j], sems.at[slot, j])
                for j, w in enumerate((wg_hbm, wu_hbm, wd_hbm))]

    @pl.when(i == 0)
    def _():
        for c in weight_copies(be_ref[0], sl_ref[0]):
            c.start()

    @pl.when(first)
    def _():
        slot = sl_ref[i]
        for c in weight_copies(be_ref[i], slot):
            c.wait()
        for j, scr in enumerate((wg_scr, wu_scr, wd_scr)):
            w = stage[slot, j]
            amax = jnp.max(jnp.max(jnp.abs(w), axis=0, keepdims=True), axis=1, keepdims=True)
            scale = F8_TARGET / jnp.maximum(amax, TINY)
            scr[...] = (w * scale).astype(F8)
            inv_scr[j] = jnp.broadcast_to(1.0 / scale, inv_scr.shape[1:])

        @pl.when(nx_ref[i] >= 0)
        def _():
            for c in weight_copies(nx_ref[i], 1 - slot):
                c.start()

    @pl.when(valid > 0)
    def _():
        lo, hi = _unpack_rows(x_ref)
        live = lax.broadcasted_iota(jnp.int32, (ROW_BLOCK, 1), 0) < valid
        x = jnp.where(live, jnp.concatenate(lo + hi, axis=1), 0.0).astype(F8)
        inv_x = 1.0 / xs_ref[0:1, 0:1]
        g = _dot(x, wg_scr[...]) * (inv_scr[0, 0:1, 0:1] * inv_x) + bg_ref[...]
        g = jnp.minimum(g, SWIGLU_LIMIT)
        l = (_dot(x, wu_scr[...]) * (inv_scr[1, 0:1, 0:1] * inv_x * HID_SCALE)
             + bu_ref[...] * HID_SCALE)
        l = jnp.clip(l, -SWIGLU_LIMIT * HID_SCALE, SWIGLU_LIMIT * HID_SCALE) + HID_SCALE
        hid = g * l / (1.0 + jnp.exp2(g * (-SWIGLU_ALPHA * LOG2_E)))
        y = _dot(hid.astype(F8), wd_scr[...])
        _pack_rows(y * (inv_scr[2, 0:1, 0:1] * (1.0 / HID_SCALE)) + bd_ref[...], y_ref)

    @pl.when(valid <= 0)
    def _():
        y_ref[...] = jnp.zeros_like(y_ref)


def _expert_call(block_expert, block_valid, next_expert, slot, x_tab, x_scale,
                 w_gate, b_gate, w_up, b_up, w_down, b_down):
    d, f = w_gate.shape[-2:]
    assert d == f
    n_blocks = block_expert.shape[0]
    bspec = lambda width: pl.BlockSpec((None, 1, width), lambda i, be, bv, nx, sl: (be[i], 0, 0))
    rows = pl.BlockSpec((SUBROWS, ROW_BLOCK, LANES), lambda i, be, bv, nx, sl: (0, i, 0))
    hbm = pl.BlockSpec(memory_space=pl.ANY)
    grid_spec = pltpu.PrefetchScalarGridSpec(
        num_scalar_prefetch=4,
        grid=(n_blocks,),
        in_specs=[rows, pl.BlockSpec((SUBLANES, LANES), lambda i, be, bv, nx, sl: (0, 0)),
                  bspec(f), bspec(f), bspec(d), hbm, hbm, hbm],
        out_specs=rows,
        scratch_shapes=[pltpu.VMEM((2, 3, d, f), F32),
                        pltpu.VMEM((d, f), F8), pltpu.VMEM((d, f), F8), pltpu.VMEM((f, d), F8),
                        pltpu.VMEM((3, SUBLANES, LANES), F32), pltpu.SemaphoreType.DMA((2, 3))],
    )
    return pl.pallas_call(
        _expert_kernel,
        grid_spec=grid_spec,
        out_shape=jax.ShapeDtypeStruct(x_tab.shape, jnp.int32),
        compiler_params=pltpu.CompilerParams(
            dimension_semantics=("arbitrary",), vmem_limit_bytes=VMEM_LIMIT),
        name="expert",
    )(block_expert, block_valid, next_expert, slot, x_tab, x_scale, b_gate, b_up, b_down,
      w_gate, w_up, w_down)


def _combine_kernel(h_ref, routet_ref, gf_ref, ys_ref, out_ref):
    nb, tl, d = out_ref.shape
    m = nb * tl
    h = h_ref[...]
    route = jnp.transpose(jnp.concatenate(
        [routet_ref[...], jnp.zeros((LANES - ROUTE_ROWS, m), F32)], axis=0))
    lo_acc = [h[:, s * LANES:(s + 1) * LANES] for s in range(SUBROWS)]
    hi_acc = [h[:, PACKED + s * LANES:PACKED + (s + 1) * LANES] for s in range(SUBROWS)]
    for k in range(TOP_K):
        gate = route[:, TOP_K + k:TOP_K + k + 1]
        lo, hi = _unpack_rows(ys_ref.at[k])
        lo_acc = [a + gate * v for a, v in zip(lo_acc, lo)]
        hi_acc = [a + gate * v for a, v in zip(hi_acc, hi)]
    acc = jnp.concatenate(lo_acc + hi_acc, axis=1)
    out_ref[...] = _rmsnorm(acc, gf_ref[...]).reshape(nb, tl, d)


def _combine_call(h, route_t, g_final, y_slots, nb, seq):
    t, d = h.shape
    tl = TIME_TILE
    m = nb * tl
    return pl.pallas_call(
        _combine_kernel,
        grid=(t // m,),
        in_specs=[pl.BlockSpec((m, d), lambda i: (i, 0)),
                  pl.BlockSpec((ROUTE_ROWS, m), lambda i: (0, i)),
                  pl.BlockSpec((1, d), lambda i: (0, 0)),
                  pl.BlockSpec((TOP_K, SUBROWS, m, LANES), lambda i: (0, 0, i, 0))],
        out_specs=pl.BlockSpec((nb, tl, d), lambda i: (0, i, 0)),
        out_shape=jax.ShapeDtypeStruct((nb, seq, d), F32),
        compiler_params=pltpu.CompilerParams(
            dimension_semantics=("arbitrary",), vmem_limit_bytes=VMEM_LIMIT),
        name="combine",
    )(h, route_t, g_final, y_slots)


def _s5_operands(lam_re, lam_im, log_dt, b_re, b_im, c_re, c_im, nb):
    gp, ns, pg = SSM_GROUPS, SSM_STATE, SSM_GROUP
    lam = lax.complex(lam_re.astype(F32), lam_im.astype(F32))
    dt = jnp.exp(log_dt.astype(F32))[:, None]
    lam_bar = jnp.exp(lam * dt)
    b_bar = ((lam_bar - 1.0) / lam)[..., None] * lax.complex(b_re.astype(F32), b_im.astype(F32))
    grp = lambda hf, p: slice(16 * hf + 8 * p, 16 * hf + 8 * p + 8)
    eye8 = jnp.eye(8, dtype=F32)

    def b_block(bb):
        return jnp.einsum('gnq,gh->gqhn', bb, eye8).reshape(8 * pg, 8 * ns)

    def c_block(cc):
        return jnp.einsum('gqn,gh->gnhq', cc, eye8).reshape(8 * ns, 8 * pg)

    bd, cd, a_re, a_im = [], [], [], []
    for p in range(2):
        rows = []
        for hf in range(2):
            bb = b_bar[grp(hf, p)]
            rows.append(jnp.concatenate([b_block(bb.real), b_block(bb.imag)], axis=1))
        bd.append(jnp.concatenate(rows, axis=0))
        cols = []
        for hf in range(2):
            cols.append(jnp.concatenate([c_block(c_re[grp(hf, p)].astype(F32)),
                                         -c_block(c_im[grp(hf, p)].astype(F32))], axis=0))
        cd.append(jnp.concatenate(cols, axis=1))
        lam_rows = jnp.stack([lam_bar[grp(j // nb, p)].reshape(8 * ns) for j in range(SUBLANES)])
        a_re.append(lam_rows.real)
        a_im.append(lam_rows.imag)
    bd = jnp.stack(bd).astype(BF16)
    cd = jnp.stack(cd).astype(BF16)
    a_re = jnp.stack(a_re)
    a_im = jnp.stack(a_im)

    return bd, cd, a_re, a_im


def kernel(x, norm_mix_g, w_in, lam_re, lam_im, log_dt, b_re, b_im, c_re, c_im, d_skip, w_glu, b_glu, sgu_ln_g, sgu_ln_b, w_s, b_s, w_branch_a, w_branch_b, w_out, norm_moe_g, w_router, b_router, w_gate, b_gate, w_up, b_up, w_down, b_down, norm_final_g):
    nb, seq, d = x.shape
    assert d == D_MODEL and SUBLANES % nb == 0 and SUBLANES // nb == 2
    assert seq % TIME_TILE == 0 and norm_mix_g.shape[0] == 1
    assert (nb * seq * SUBROWS) % (SC_WORKERS * LANES) == 0
    assert (nb * seq * SUBROWS * TOP_K) % (SC_WORKERS * LANES * 2) == 0
    tl = TIME_TILE
    t = nb * seq
    row = lambda v: v.reshape(1, -1).astype(F32)

    xa, sga, pb = _proj_call(
        x, row(norm_mix_g[0]), w_in[0].astype(BF16), row(sgu_ln_g[0]), row(sgu_ln_b[0]),
        w_s[0].astype(F32), b_s[0].T.astype(F32), w_branch_b[0].astype(BF16))

    bd, cd, a_re, a_im = _s5_operands(
        lam_re[0], lam_im[0], log_dt[0], b_re[0], b_im[0], c_re[0], c_im[0], nb)
    w_r = jnp.zeros((d, LANES), F32).at[:, :N_EXPERTS].set(w_router[0].astype(F32)).astype(BF16)
    b_r = jnp.full((1, LANES), NEG_BIG, F32).at[0, :N_EXPERTS].set(b_router[0].astype(F32))
    x_bound = math.sqrt(d) * jnp.max(jnp.abs(norm_moe_g[0].astype(F32)))
    x_scale = jnp.exp2(jnp.floor(jnp.log2(F8_TARGET / jnp.maximum(x_bound, TINY))))
    x_scale = jnp.full((SUBLANES, LANES), 1.0, F32) * x_scale
    m = nb * tl
    upper = (jnp.arange(m)[:, None] < jnp.arange(m)[None, :]).astype(BF16)
    h, xn2p, route_t, cnt = _mix_call(
        xa, sga, pb, x, upper, bd, cd, a_re, a_im, row(d_skip[0]),
        w_glu[0].astype(BF16), row(b_glu[0]), w_branch_a[0].astype(BF16), w_out[0].astype(BF16),
        row(norm_moe_g[0]), x_scale, w_r, b_r)

    idx_t = route_t[0:TOP_K].astype(jnp.int32)
    rank_t = route_t[2 * TOP_K:3 * TOP_K].astype(jnp.int32)
    counts = cnt[:, 0].astype(jnp.int32)
    padded = (counts + ROW_BLOCK - 1) // ROW_BLOCK * ROW_BLOCK
    cum = jnp.cumsum(padded)
    pstart = cum - padded
    experts = jnp.arange(N_EXPERTS, dtype=jnp.int32)
    dest_t = rank_t
    for e in range(N_EXPERTS):
        dest_t = dest_t + jnp.where(idx_t == e, pstart[e], 0)
    n_blocks = (t * TOP_K) // ROW_BLOCK + N_EXPERTS
    n_rows = n_blocks * ROW_BLOCK
    block_row0 = jnp.arange(n_blocks, dtype=jnp.int32) * ROW_BLOCK
    block_expert = jnp.minimum(
        jnp.sum((cum[None, :] <= block_row0[:, None]).astype(jnp.int32), axis=1), N_EXPERTS - 1)
    block_valid = jnp.clip(counts[block_expert] - (block_row0 - pstart[block_expert]), 0, ROW_BLOCK)
    present = counts > 0
    slot_e = (jnp.cumsum(present.astype(jnp.int32)) - 1) % 2
    later = lax.cummin(jnp.where(present, experts, N_EXPERTS)[::-1])[::-1]
    next_e = jnp.concatenate([later[1:], jnp.full((1,), N_EXPERTS, jnp.int32)])
    next_e = jnp.where(next_e == N_EXPERTS, -1, next_e)

    sub = dest_t[:, None, :] + (jnp.arange(SUBROWS, dtype=jnp.int32) * n_rows)[None, :, None]
    idx_gather = sub.reshape(SC_WORKERS, -1, LANES)
    idx_dispatch = sub.reshape(TOP_K, SUBROWS * t // LANES, LANES).transpose(1, 0, 2)
    idx_dispatch = idx_dispatch.reshape(SC_WORKERS, -1, LANES)

    x_tab = _sc_dispatch(xn2p.reshape(SUBROWS * t, LANES), idx_dispatch, SUBROWS * n_rows)
    y_tab = _expert_call(
        block_expert, block_valid, next_e[block_expert], slot_e[block_expert],
        x_tab.reshape(SUBROWS, n_rows, LANES), x_scale,
        w_gate[0], b_gate[0][:, None, :], w_up[0], b_up[0][:, None, :],
        w_down[0], b_down[0][:, None, :])
    y_slots = _sc_gather(y_tab.reshape(SUBROWS * n_rows, LANES), idx_gather)
    return _combine_call(h, route_t, row(norm_final_g),
                         y_slots.reshape(TOP_K, SUBROWS, t, LANES), nb, seq)
```

```python
import math

import jax
import jax.numpy as jnp
from jax import lax
from jax.experimental import pallas as pl
from jax.experimental.pallas import tpu as pltpu
from jax.experimental.pallas import tpu_sc as plsc

F32 = jnp.float32
BF16 = jnp.bfloat16
F8 = jnp.float8_e4m3fn
F8_TARGET = 240.0
TINY = 1e-30

NORM_EPS = 1e-5
D_MODEL = 1024
SSM_WIDTH = 512
SSM_GROUP = 16
SSM_GROUPS = 32
SSM_STATE = 64
SGU_WIDTH = 512
SGU_BLOCK = 128
SGU_HEADS = 4
SGU_HEAD_DIM = 128
CHUNK = 64
N_EXPERTS = 32
TOP_K = 4
SWIGLU_ALPHA = 1.702
SWIGLU_LIMIT = 7.0
LOG2_E = 1.4426950408889634
HID_SCALE = 4.0

LANES = 128
SUBLANES = 8
TIME_TILE = 128
ROW_BLOCK = 512
VMEM_LIMIT = 56 * 1024 * 1024
NEG_BIG = -1e30
SC_CORES = 2
SC_SUBCORES = 16
SC_WORKERS = SC_CORES * SC_SUBCORES
SUBROWS = 4
COMBINE_PARTS = 4
ROUTE_ROWS = 16
PACKED = D_MODEL // 2


def _pack_rows(x, out_ref):
    lo = lax.bitcast_convert_type(x[:, :PACKED].astype(BF16).astype(F32), jnp.int32)
    hi = lax.bitcast_convert_type(x[:, PACKED:].astype(BF16).astype(F32), jnp.int32)
    words = lax.shift_right_logical(lo, 16) | (hi & jnp.int32(-65536))
    for s in range(SUBROWS):
        out_ref[s] = words[:, s * LANES:(s + 1) * LANES]


def _unpack_rows(ref):
    lo, hi = [], []
    for s in range(SUBROWS):
        w = ref[s]
        lo.append(lax.bitcast_convert_type(lax.shift_left(w, 16), F32))
        hi.append(lax.bitcast_convert_type(w & jnp.int32(-65536), F32))
    return lo, hi


def _dot(a, b):
    return jnp.dot(a, b, preferred_element_type=F32)


def _sigmoid(x):
    return 1.0 / (1.0 + jnp.exp(-x))


def _gelu(x):
    return 0.5 * x * (1.0 + jnp.tanh(0.7978845608028654 * (x + 0.044715 * (x * x * x))))


def _rmsnorm(x, g):
    return x * lax.rsqrt(jnp.mean(x * x, axis=-1, keepdims=True) + NORM_EPS) * g


def _proj_kernel(x_ref, g_ref, w_ref, lng_ref, lnb_ref, ws_ref, bs_ref, wb_ref,
                 xa_ref, sga_ref, pb_ref, yb_scr):
    nb, tl, d = x_ref.shape
    m = nb * tl
    x = x_ref[...].reshape(m, d)
    xn = _rmsnorm(x, g_ref[...]).astype(BF16)

    xa = _dot(xn, w_ref[:, 0:SSM_WIDTH])
    xa_ref[...] = xa.astype(BF16).reshape(nb, tl, SSM_WIDTH)

    s_b = SSM_WIDTH + 2 * SGU_WIDTH
    s_g = s_b + D_MODEL
    ga = _dot(xn, w_ref[:, s_b:s_g])
    sga_ref[...] = _sigmoid(ga).astype(BF16).reshape(nb, tl, D_MODEL)

    z = _gelu(_dot(xn, w_ref[:, SSM_WIDTH:s_b]))
    u = z[:, :SGU_WIDTH]
    v = z[:, SGU_WIDTH:]
    mu = jnp.mean(v, axis=-1, keepdims=True)
    vc = v - mu
    v = vc * lax.rsqrt(jnp.mean(vc * vc, axis=-1, keepdims=True) + NORM_EPS)
    v = (v * lng_ref[...] + lnb_ref[...]).astype(BF16)

    ri = lax.broadcasted_iota(jnp.int32, (SGU_BLOCK, SGU_BLOCK), 0) // CHUNK
    ci = lax.broadcasted_iota(jnp.int32, (SGU_BLOCK, SGU_BLOCK), 1) // CHUNK
    causal = ri >= ci
    for h in range(SGU_HEADS):
        wm = jnp.where(causal, ws_ref[h], 0.0).astype(BF16)
        bias = bs_ref[:, h:h + 1]
        lo = h * SGU_HEAD_DIM
        for blk in range(m // SGU_BLOCK):
            r0 = blk * SGU_BLOCK
            s = _dot(wm, v[r0:r0 + SGU_BLOCK, lo:lo + SGU_HEAD_DIM]) + bias
            yb_scr[r0:r0 + SGU_BLOCK, lo:lo + SGU_HEAD_DIM] = (
                u[r0:r0 + SGU_BLOCK, lo:lo + SGU_HEAD_DIM] * s).astype(BF16)

    gb = _dot(xn, w_ref[:, s_g:])
    pb = _sigmoid(gb) * _dot(yb_scr[...], wb_ref[...])
    pb_ref[...] = pb.astype(BF16).reshape(nb, tl, D_MODEL)


def _proj_call(x, g, w_in, ln_g, ln_b, w_s, b_s_t, w_b):
    nb, seq, d = x.shape
    tl = TIME_TILE
    m = nb * tl
    const = lambda shape: pl.BlockSpec(shape, lambda i: (0,) * len(shape))
    tile = lambda width: pl.BlockSpec((nb, tl, width), lambda i: (0, i, 0))
    return pl.pallas_call(
        _proj_kernel,
        grid=(seq // tl,),
        in_specs=[tile(d), const((1, d)), const(w_in.shape), const((1, SGU_WIDTH)),
                  const((1, SGU_WIDTH)), const(w_s.shape), const(b_s_t.shape), const(w_b.shape)],
        out_specs=[tile(SSM_WIDTH), tile(D_MODEL), tile(D_MODEL)],
        out_shape=[jax.ShapeDtypeStruct((nb, seq, SSM_WIDTH), BF16),
                   jax.ShapeDtypeStruct((nb, seq, D_MODEL), BF16),
                   jax.ShapeDtypeStruct((nb, seq, D_MODEL), BF16)],
        scratch_shapes=[pltpu.VMEM((m, SGU_WIDTH), BF16)],
        compiler_params=pltpu.CompilerParams(
            dimension_semantics=("arbitrary",), vmem_limit_bytes=VMEM_LIMIT),
        name="proj",
    )(x, g, w_in, ln_g, ln_b, w_s, b_s_t, w_b)


def _mix_kernel(xa_ref, sga_ref, pb_ref, x_ref, upper_ref, bd_ref, cd_ref,
                are_ref, aim_ref, dskip_ref, wglu_ref, bglu_ref, wa_ref, wout_ref,
                g2_ref, xs_ref, wr_ref, br_ref,
                h_ref, xn2_ref, routet_ref, cnt_ref,
                bu0_scr, bu1_scr, x8_scr, y8_scr, state_scr, cnt_scr):
    nb, tl, d = x_ref.shape
    m = nb * tl

    @pl.when(pl.program_id(0) == 0)
    def _():
        state_scr[...] = jnp.zeros_like(state_scr)
        cnt_scr[...] = jnp.zeros_like(cnt_scr)
        x8_scr[...] = jnp.zeros_like(x8_scr)

    xa = xa_ref[...].reshape(m, SSM_WIDTH).astype(F32)

    def chunk_lo(p, hf):
        return hf * (SSM_WIDTH // 2) + p * LANES

    scans = (bu0_scr, bu1_scr)
    for p in range(2):
        for hf in range(2):
            lo = chunk_lo(p, hf)
            for b in range(nb):
                x8_scr[2 * p + hf, pl.ds(hf * nb + b, tl, stride=SUBLANES), :] = (
                    xa[b * tl:(b + 1) * tl, lo:lo + LANES])
        x8 = jnp.concatenate([x8_scr[2 * p], x8_scr[2 * p + 1]], axis=1).astype(BF16)
        scans[p][...] = _dot(x8, bd_ref[p])

    a_re = [are_ref[p] for p in range(2)]
    a_im = [aim_ref[p] for p in range(2)]
    s_re = [state_scr[p][:, :SSM_WIDTH] for p in range(2)]
    s_im = [state_scr[p][:, SSM_WIDTH:] for p in range(2)]
    for t in range(tl):
        r = t * SUBLANES
        for p in range(2):
            bu_scr = scans[p]
            n_re = a_re[p] * s_re[p] - a_im[p] * s_im[p] + bu_scr[r:r + SUBLANES, 0:SSM_WIDTH]
            n_im = (a_re[p] * s_im[p] + a_im[p] * s_re[p]
                    + bu_scr[r:r + SUBLANES, SSM_WIDTH:2 * SSM_WIDTH])
            bu_scr[r:r + SUBLANES, 0:SSM_WIDTH] = n_re
            bu_scr[r:r + SUBLANES, SSM_WIDTH:2 * SSM_WIDTH] = n_im
            s_re[p], s_im[p] = n_re, n_im

    y_chunks = {}
    for p in range(2):
        state_scr[p] = jnp.concatenate([s_re[p], s_im[p]], axis=1)
        y8 = _dot(scans[p][...].astype(BF16), cd_ref[p])
        for hf in range(2):
            y8_scr[2 * p + hf] = y8[:, hf * LANES:(hf + 1) * LANES]
            y_chunks[(hf, p)] = jnp.concatenate(
                [y8_scr[2 * p + hf, pl.ds(hf * nb + b, tl, stride=SUBLANES), :]
                 for b in range(nb)], axis=0)
    y = jnp.concatenate([y_chunks[(hf, p)] for hf in range(2) for p in range(2)], axis=1)
    y = y + dskip_ref[...] * xa
    z = _gelu(y)
    ya = z * _sigmoid(_dot(z.astype(BF16), wglu_ref[...]) + bglu_ref[...])
    pa = sga_ref[...].reshape(m, d).astype(F32) * _dot(ya.astype(BF16), wa_ref[...])
    merged = pa + pb_ref[...].reshape(m, d).astype(F32)
    h = x_ref[...].reshape(m, d) + _dot(merged.astype(BF16), wout_ref[...])
    h_ref[...] = h
    xn2 = _rmsnorm(h, g2_ref[...])
    _pack_rows(xn2 * xs_ref[0:1, 0:1], xn2_ref)

    logits = _dot(xn2.astype(BF16), wr_ref[...]) + br_ref[...]
    work = jnp.transpose(logits)[:N_EXPERTS]
    expert = lax.broadcasted_iota(jnp.int32, (N_EXPERTS, m), 0).astype(F32)
    vals, idxs = [], []
    member = jnp.zeros((N_EXPERTS, m), F32)
    for _ in range(TOP_K):
        mx = jnp.max(work, axis=0, keepdims=True)
        ix = jnp.min(jnp.where(work == mx, expert, float(N_EXPERTS)), axis=0, keepdims=True)
        hit = expert == ix
        member = jnp.where(hit, 1.0, member)
        work = jnp.where(hit, NEG_BIG, work)
        vals.append(mx)
        idxs.append(ix)
    exps = [jnp.exp(v - vals[0]) for v in vals]
    denom = exps[0] + exps[1] + exps[2] + exps[3]
    gates = [e / denom for e in exps]

    before = _dot(member.astype(BF16), upper_ref[...]) + cnt_scr[:, 0:1]
    new_cnt = cnt_scr[:, 0:1] + jnp.sum(member, axis=1, keepdims=True)
    cnt_scr[...] = jnp.broadcast_to(new_cnt, cnt_scr.shape)
    cnt_ref[...] = jnp.broadcast_to(new_cnt, cnt_ref.shape)
    ranks = [jnp.sum(jnp.where(expert == ix, before, 0.0), axis=0, keepdims=True) for ix in idxs]
    pad = jnp.zeros((ROUTE_ROWS - 3 * TOP_K, m), F32)
    routet_ref[...] = jnp.concatenate(idxs + gates + ranks + [pad], axis=0)


def _mix_call(xa, sga, pb, x, upper, bd, cd, a_re, a_im, dskip, w_glu, b_glu, w_a, w_out,
              g2, x_scale, w_r, b_r):
    nb, seq, d = x.shape
    tl = TIME_TILE
    m = nb * tl
    n_tiles = seq // tl
    const = lambda shape: pl.BlockSpec(shape, lambda i: (0,) * len(shape))
    tile = lambda width: pl.BlockSpec((nb, tl, width), lambda i: (0, i, 0))
    rows = lambda width: pl.BlockSpec((m, width), lambda i: (i, 0))
    operands = (xa, sga, pb, x, upper, bd, cd, a_re, a_im, dskip, w_glu, b_glu, w_a, w_out,
                g2, x_scale, w_r, b_r)
    in_specs = [tile(SSM_WIDTH), tile(d), tile(d), tile(d)] + [const(o.shape) for o in operands[4:]]
    return pl.pallas_call(
        _mix_kernel,
        grid=(n_tiles,),
        in_specs=in_specs,
        out_specs=[rows(d), pl.BlockSpec((SUBROWS, m, LANES), lambda i: (0, i, 0)),
                   pl.BlockSpec((ROUTE_ROWS, m), lambda i: (0, i)), const((N_EXPERTS, LANES))],
        out_shape=[jax.ShapeDtypeStruct((n_tiles * m, d), F32),
                   jax.ShapeDtypeStruct((SUBROWS, n_tiles * m, LANES), jnp.int32),
                   jax.ShapeDtypeStruct((ROUTE_ROWS, n_tiles * m), F32),
                   jax.ShapeDtypeStruct((N_EXPERTS, LANES), F32)],
        scratch_shapes=[pltpu.VMEM((SUBLANES * tl, 2 * SSM_WIDTH), F32),
                        pltpu.VMEM((SUBLANES * tl, 2 * SSM_WIDTH), F32),
                        pltpu.VMEM((4, SUBLANES * tl, LANES), F32),
                        pltpu.VMEM((4, SUBLANES * tl, LANES), F32),
                        pltpu.VMEM((2, SUBLANES, 2 * SSM_WIDTH), F32),
                        pltpu.VMEM((N_EXPERTS, LANES), F32)],
        compiler_params=pltpu.CompilerParams(
            dimension_semantics=("arbitrary",), vmem_limit_bytes=VMEM_LIMIT),
        name="mix",
    )(*operands)


def _sc_mesh():
    return plsc.VectorSubcoreMesh(core_axis_name="c", subcore_axis_name="s")


def _worker_id():
    return lax.axis_index("s") * SC_CORES + lax.axis_index("c")


def _dispatch_body(src_hbm, idx_hbm, out_hbm, idx_v, buf, sem):
    n_chunks = idx_v.shape[0] // TOP_K
    wid = _worker_id()
    base = wid * (n_chunks * LANES)
    pltpu.sync_copy(idx_hbm.at[wid], idx_v)

    @pl.loop(0, n_chunks)
    def _(j):
        pltpu.sync_copy(src_hbm.at[pl.ds(base + j * LANES, LANES)], buf)
        copies = [pltpu.async_copy(buf, out_hbm.at[idx_v.at[j * TOP_K + k]], sem)
                  for k in range(TOP_K)]
        for c in copies:
            c.wait()


def _sc_dispatch(src, idx, n_out):
    n_chunks = src.shape[0] // (SC_WORKERS * LANES)
    return pl.kernel(
        _dispatch_body,
        out_type=jax.ShapeDtypeStruct((n_out, LANES), jnp.int32),
        mesh=_sc_mesh(),
        scratch_types=[pltpu.VMEM((n_chunks * TOP_K, LANES), jnp.int32),
                       pltpu.VMEM((LANES, LANES), jnp.int32),
                       pltpu.SemaphoreType.DMA],
        name="sc_dispatch",
    )(src, idx)


def _gather_body(tab_hbm, idx_hbm, out_hbm, idx_v, buf0, buf1, sem0, sem1):
    n_chunks = idx_v.shape[0]
    wid = _worker_id()
    base = wid * (n_chunks * LANES)
    pltpu.sync_copy(idx_hbm.at[wid], idx_v)

    @pl.loop(0, n_chunks, step=2)
    def _(j):
        c0 = pltpu.async_copy(tab_hbm.at[idx_v.at[j]], buf0, sem0)
        c1 = pltpu.async_copy(tab_hbm.at[idx_v.at[j + 1]], buf1, sem1)
        c0.wait()
        pltpu.sync_copy(buf0, out_hbm.at[pl.ds(base + j * LANES, LANES)])
        c1.wait()
        pltpu.sync_copy(buf1, out_hbm.at[pl.ds(base + (j + 1) * LANES, LANES)])


def _sc_gather(tab, idx):
    n_chunks = idx.shape[1]
    return pl.kernel(
        _gather_body,
        out_type=jax.ShapeDtypeStruct((SC_WORKERS * n_chunks * LANES, LANES), jnp.int32),
        mesh=_sc_mesh(),
        scratch_types=[pltpu.VMEM((n_chunks, LANES), jnp.int32),
                       pltpu.VMEM((LANES, LANES), jnp.int32),
                       pltpu.VMEM((LANES, LANES), jnp.int32),
                       pltpu.SemaphoreType.DMA, pltpu.SemaphoreType.DMA],
        name="sc_gather",
    )(tab, idx)


def _expert_kernel(be_ref, bv_ref, nx_ref, sl_ref, x_ref, xs_ref, bg_ref, bu_ref, bd_ref,
                   wg_hbm, wu_hbm, wd_hbm, y_ref, stage, wg_scr, wu_scr, wd_scr, inv_scr, sems):
    i = pl.program_id(0)
    valid = bv_ref[i]
    first = jnp.logical_and(
        valid > 0, jnp.logical_or(i == 0, be_ref[i] != be_ref[jnp.maximum(i - 1, 0)]))

    def weight_copies(expert, slot):
        return [pltpu.make_async_copy(w.at[expert], stage.at[slot, j], sems.at[slot, j])
                for j, w in enumerate((wg_hbm, wu_hbm, wd_hbm))]

    @pl.when(i == 0)
    def _():
        for c in weight_copies(be_ref[0], sl_ref[0]):
            c.start()

    @pl.when(first)
    def _():
        slot = sl_ref[i]
        for c in weight_copies(be_ref[i], slot):
            c.wait()
        for j, scr in enumerate((wg_scr, wu_scr, wd_scr)):
            w = stage[slot, j]
            amax = jnp.max(jnp.max(jnp.abs(w), axis=0, keepdims=True), axis=1, keepdims=True)
            scale = F8_TARGET / jnp.maximum(amax, TINY)
            scr[...] = (w * scale).astype(F8)
            inv_scr[j] = jnp.broadcast_to(1.0 / scale, inv_scr.shape[1:])

        @pl.when(nx_ref[i] >= 0)
        def _():
            for c in weight_copies(nx_ref[i], 1 - slot):
                c.start()

    @pl.when(valid > 0)
    def _():
        lo, hi = _unpack_rows(x_ref)
        live = lax.broadcasted_iota(jnp.int32, (ROW_BLOCK, 1), 0) < valid
        x = jnp.where(live, jnp.concatenate(lo + hi, axis=1), 0.0).astype(F8)
        inv_x = 1.0 / xs_ref[0:1, 0:1]
        cg = (inv_scr[0, 0:1, 0:1] * inv_x).astype(BF16)
        cl = (inv_scr[1, 0:1, 0:1] * inv_x * HID_SCALE).astype(BF16)
        g = _dot(x, wg_scr[...]).astype(BF16) * cg + bg_ref[...].astype(BF16)
        g = jnp.minimum(g, SWIGLU_LIMIT)
        l = _dot(x, wu_scr[...]).astype(BF16) * cl + (bu_ref[...] * HID_SCALE).astype(BF16)
        l = jnp.clip(l, -SWIGLU_LIMIT * HID_SCALE, SWIGLU_LIMIT * HID_SCALE) + HID_SCALE
        hid = g * l / (1.0 + jnp.exp2(g * (-SWIGLU_ALPHA * LOG2_E)))
        y = _dot(hid.astype(F8), wd_scr[...])
        _pack_rows(y * (inv_scr[2, 0:1, 0:1] * (1.0 / HID_SCALE)) + bd_ref[...], y_ref)

    @pl.when(valid <= 0)
    def _():
        y_ref[...] = jnp.zeros_like(y_ref)


def _expert_call(block_expert, block_valid, next_expert, slot, x_tab, x_scale,
                 w_gate, b_gate, w_up, b_up, w_down, b_down):
    d, f = w_gate.shape[-2:]
    assert d == f
    n_blocks = block_expert.shape[0]
    bspec = lambda width: pl.BlockSpec((None, 1, width), lambda i, be, bv, nx, sl: (be[i], 0, 0))
    rows = pl.BlockSpec((SUBROWS, ROW_BLOCK, LANES), lambda i, be, bv, nx, sl: (0, i, 0))
    hbm = pl.BlockSpec(memory_space=pl.ANY)
    grid_spec = pltpu.PrefetchScalarGridSpec(
        num_scalar_prefetch=4,
        grid=(n_blocks,),
        in_specs=[rows, pl.BlockSpec((SUBLANES, LANES), lambda i, be, bv, nx, sl: (0, 0)),
                  bspec(f), bspec(f), bspec(d), hbm, hbm, hbm],
        out_specs=rows,
        scratch_shapes=[pltpu.VMEM((2, 3, d, f), F32),
                        pltpu.VMEM((d, f), F8), pltpu.VMEM((d, f), F8), pltpu.VMEM((f, d), F8),
                        pltpu.VMEM((3, SUBLANES, LANES), F32), pltpu.SemaphoreType.DMA((2, 3))],
    )
    return pl.pallas_call(
        _expert_kernel,
        grid_spec=grid_spec,
        out_shape=jax.ShapeDtypeStruct(x_tab.shape, jnp.int32),
        compiler_params=pltpu.CompilerParams(
            dimension_semantics=("arbitrary",), vmem_limit_bytes=VMEM_LIMIT),
        name="expert",
    )(block_expert, block_valid, next_expert, slot, x_tab, x_scale, b_gate, b_up, b_down,
      w_gate, w_up, w_down)


def _combine_kernel(h_ref, routet_ref, gf_ref, ys_ref, *rest):
    out_ref = rest[-1]
    nb, tl, d = out_ref.shape
    m = nb * tl
    h = h_ref[...]
    route = jnp.transpose(jnp.concatenate(
        [routet_ref[...], jnp.zeros((LANES - ROUTE_ROWS, m), F32)], axis=0))
    lo_acc = [h[:, s * LANES:(s + 1) * LANES] for s in range(SUBROWS)]
    hi_acc = [h[:, PACKED + s * LANES:PACKED + (s + 1) * LANES] for s in range(SUBROWS)]
    for k in range(TOP_K):
        gate = route[:, TOP_K + k:TOP_K + k + 1]
        lo, hi = _unpack_rows(ys_ref.at[k])
        lo_acc = [a + gate * v for a, v in zip(lo_acc, lo)]
        hi_acc = [a + gate * v for a, v in zip(hi_acc, hi)]
    acc = jnp.concatenate(lo_acc + hi_acc, axis=1)
    out_ref[...] = _rmsnorm(acc, gf_ref[...]).reshape(nb, tl, d)


def _combine_call(h, route_t, g_final, y_slots, nb, seq, part, n_parts, prev):
    t, d = h.shape
    tl = TIME_TILE
    m = nb * tl
    n = t // m // n_parts
    first = part * n
    in_specs = [pl.BlockSpec((m, d), lambda i: (first + i, 0)),
                pl.BlockSpec((ROUTE_ROWS, m), lambda i: (0, first + i)),
                pl.BlockSpec((1, d), lambda i: (0, 0)),
                pl.BlockSpec((TOP_K, SUBROWS, m, LANES), lambda i: (0, 0, i, 0))]
    operands = [h, route_t, g_final, y_slots]
    aliases = {}
    if prev is not None:
        in_specs.append(pl.BlockSpec(memory_space=pl.ANY))
        operands.append(prev)
        aliases = {4: 0}
    return pl.pallas_call(
        _combine_kernel,
        grid=(n,),
        in_specs=in_specs,
        out_specs=pl.BlockSpec((nb, tl, d), lambda i: (0, first + i, 0)),
        out_shape=jax.ShapeDtypeStruct((nb, seq, d), F32),
        input_output_aliases=aliases,
        compiler_params=pltpu.CompilerParams(
            dimension_semantics=("arbitrary",), vmem_limit_bytes=VMEM_LIMIT),
        name="combine",
    )(*operands)


def _s5_operands(lam_re, lam_im, log_dt, b_re, b_im, c_re, c_im, nb):
    gp, ns, pg = SSM_GROUPS, SSM_STATE, SSM_GROUP
    lam = lax.complex(lam_re.astype(F32), lam_im.astype(F32))
    dt = jnp.exp(log_dt.astype(F32))[:, None]
    lam_bar = jnp.exp(lam * dt)
    b_bar = ((lam_bar - 1.0) / lam)[..., None] * lax.complex(b_re.astype(F32), b_im.astype(F32))
    grp = lambda hf, p: slice(16 * hf + 8 * p, 16 * hf + 8 * p + 8)
    eye8 = jnp.eye(8, dtype=F32)

    def b_block(bb):
        return jnp.einsum('gnq,gh->gqhn', bb, eye8).reshape(8 * pg, 8 * ns)

    def c_block(cc):
        return jnp.einsum('gqn,gh->gnhq', cc, eye8).reshape(8 * ns, 8 * pg)

    bd, cd, a_re, a_im = [], [], [], []
    for p in range(2):
        rows = []
        for hf in range(2):
            bb = b_bar[grp(hf, p)]
            rows.append(jnp.concatenate([b_block(bb.real), b_block(bb.imag)], axis=1))
        bd.append(jnp.concatenate(rows, axis=0))
        cols = []
        for hf in range(2):
            cols.append(jnp.concatenate([c_block(c_re[grp(hf, p)].astype(F32)),
                                         -c_block(c_im[grp(hf, p)].astype(F32))], axis=0))
        cd.append(jnp.concatenate(cols, axis=1))
        lam_rows = jnp.stack([lam_bar[grp(j // nb, p)].reshape(8 * ns) for j in range(SUBLANES)])
        a_re.append(lam_rows.real)
        a_im.append(lam_rows.imag)
    bd = jnp.stack(bd).astype(BF16)
    cd = jnp.stack(cd).astype(BF16)
    a_re = jnp.stack(a_re)
    a_im = jnp.stack(a_im)

    return bd, cd, a_re, a_im


def kernel(x, norm_mix_g, w_in, lam_re, lam_im, log_dt, b_re, b_im, c_re, c_im, d_skip, w_glu, b_glu, sgu_ln_g, sgu_ln_b, w_s, b_s, w_branch_a, w_branch_b, w_out, norm_moe_g, w_router, b_router, w_gate, b_gate, w_up, b_up, w_down, b_down, norm_final_g):
    nb, seq, d = x.shape
    assert d == D_MODEL and SUBLANES % nb == 0 and SUBLANES // nb == 2
    assert seq % TIME_TILE == 0 and norm_mix_g.shape[0] == 1
    assert (nb * seq * SUBROWS) % (SC_WORKERS * LANES) == 0
    assert (nb * seq * SUBROWS * TOP_K) % (SC_WORKERS * LANES * 2 * COMBINE_PARTS) == 0
    assert (seq // TIME_TILE) % COMBINE_PARTS == 0
    tl = TIME_TILE
    t = nb * seq
    row = lambda v: v.reshape(1, -1).astype(F32)

    xa, sga, pb = _proj_call(
        x, row(norm_mix_g[0]), w_in[0].astype(BF16), row(sgu_ln_g[0]), row(sgu_ln_b[0]),
        w_s[0].astype(F32), b_s[0].T.astype(F32), w_branch_b[0].astype(BF16))

    bd, cd, a_re, a_im = _s5_operands(
        lam_re[0], lam_im[0], log_dt[0], b_re[0], b_im[0], c_re[0], c_im[0], nb)
    w_r = jnp.zeros((d, LANES), F32).at[:, :N_EXPERTS].set(w_router[0].astype(F32)).astype(BF16)
    b_r = jnp.full((1, LANES), NEG_BIG, F32).at[0, :N_EXPERTS].set(b_router[0].astype(F32))
    x_bound = math.sqrt(d) * jnp.max(jnp.abs(norm_moe_g[0].astype(F32)))
    x_scale = jnp.exp2(jnp.floor(jnp.log2(F8_TARGET / jnp.maximum(x_bound, TINY))))
    x_scale = jnp.full((SUBLANES, LANES), 1.0, F32) * x_scale
    m = nb * tl
    upper = (jnp.arange(m)[:, None] < jnp.arange(m)[None, :]).astype(BF16)
    h, xn2p, route_t, cnt = _mix_call(
        xa, sga, pb, x, upper, bd, cd, a_re, a_im, row(d_skip[0]),
        w_glu[0].astype(BF16), row(b_glu[0]), w_branch_a[0].astype(BF16), w_out[0].astype(BF16),
        row(norm_moe_g[0]), x_scale, w_r, b_r)

    idx_t = route_t[0:TOP_K].astype(jnp.int32)
    rank_t = route_t[2 * TOP_K:3 * TOP_K].astype(jnp.int32)
    counts = cnt[:, 0].astype(jnp.int32)
    padded = (counts + ROW_BLOCK - 1) // ROW_BLOCK * ROW_BLOCK
    cum = jnp.cumsum(padded)
    pstart = cum - padded
    experts = jnp.arange(N_EXPERTS, dtype=jnp.int32)
    dest_t = rank_t
    for e in range(N_EXPERTS):
        dest_t = dest_t + jnp.where(idx_t == e, pstart[e], 0)
    n_blocks = (t * TOP_K) // ROW_BLOCK + N_EXPERTS
    n_rows = n_blocks * ROW_BLOCK
    block_row0 = jnp.arange(n_blocks, dtype=jnp.int32) * ROW_BLOCK
    block_expert = jnp.minimum(
        jnp.sum((cum[None, :] <= block_row0[:, None]).astype(jnp.int32), axis=1), N_EXPERTS - 1)
    block_valid = jnp.clip(counts[block_expert] - (block_row0 - pstart[block_expert]), 0, ROW_BLOCK)
    present = counts > 0
    slot_e = (jnp.cumsum(present.astype(jnp.int32)) - 1) % 2
    later = lax.cummin(jnp.where(present, experts, N_EXPERTS)[::-1])[::-1]
    next_e = jnp.concatenate([later[1:], jnp.full((1,), N_EXPERTS, jnp.int32)])
    next_e = jnp.where(next_e == N_EXPERTS, -1, next_e)

    sub = dest_t[:, None, :] + (jnp.arange(SUBROWS, dtype=jnp.int32) * n_rows)[None, :, None]
    idx_dispatch = sub.reshape(TOP_K, SUBROWS * t // LANES, LANES).transpose(1, 0, 2)
    idx_dispatch = idx_dispatch.reshape(SC_WORKERS, -1, LANES)

    x_tab = _sc_dispatch(xn2p.reshape(SUBROWS * t, LANES), idx_dispatch, SUBROWS * n_rows)
    y_tab = _expert_call(
        block_expert, block_valid, next_e[block_expert], slot_e[block_expert],
        x_tab.reshape(SUBROWS, n_rows, LANES), x_scale,
        w_gate[0], b_gate[0][:, None, :], w_up[0], b_up[0][:, None, :],
        w_down[0], b_down[0][:, None, :])
    y_flat = y_tab.reshape(SUBROWS * n_rows, LANES)
    tp = t // COMBINE_PARTS
    idx_parts = sub.reshape(TOP_K, SUBROWS, COMBINE_PARTS, tp).transpose(2, 0, 1, 3)
    out = None
    for q in range(COMBINE_PARTS):
        y_slots = _sc_gather(y_flat, idx_parts[q].reshape(SC_WORKERS, -1, LANES))
        out = _combine_call(h, route_t, row(norm_final_g),
                            y_slots.reshape(TOP_K, SUBROWS, tp, LANES), nb, seq,
                            q, COMBINE_PARTS, out)
    return out
```

```python
import math

import jax
import jax.numpy as jnp
from jax import lax
from jax.experimental import pallas as pl
from jax.experimental.pallas import tpu as pltpu
from jax.experimental.pallas import tpu_sc as plsc

F32 = jnp.float32
BF16 = jnp.bfloat16
F8 = jnp.float8_e4m3fn
F8_TARGET = 240.0
TINY = 1e-30

NORM_EPS = 1e-5
D_MODEL = 1024
SSM_WIDTH = 512
SSM_GROUP = 16
SSM_GROUPS = 32
SSM_STATE = 64
SGU_WIDTH = 512
SGU_BLOCK = 128
SGU_HEADS = 4
SGU_HEAD_DIM = 128
CHUNK = 64
N_EXPERTS = 32
TOP_K = 4
SWIGLU_ALPHA = 1.702
SWIGLU_LIMIT = 7.0
LOG2_E = 1.4426950408889634
HID_SCALE = 4.0

LANES = 128
SUBLANES = 8
TIME_TILE = 128
ROW_BLOCK = 512
VMEM_LIMIT = 56 * 1024 * 1024
NEG_BIG = -1e30
SC_CORES = 2
SC_SUBCORES = 16
SC_WORKERS = SC_CORES * SC_SUBCORES
SUBROWS = 4
TIME_PARTS = 2
COMBINE_PARTS = 2
ROUTE_ROWS = 16
PACKED = D_MODEL // 2


def _pack_rows(x, out_ref):
    lo = lax.bitcast_convert_type(x[:, :PACKED].astype(BF16).astype(F32), jnp.int32)
    hi = lax.bitcast_convert_type(x[:, PACKED:].astype(BF16).astype(F32), jnp.int32)
    words = lax.shift_right_logical(lo, 16) | (hi & jnp.int32(-65536))
    for s in range(SUBROWS):
        out_ref[s] = words[:, s * LANES:(s + 1) * LANES]


def _unpack_rows(ref):
    lo, hi = [], []
    for s in range(SUBROWS):
        w = ref[s]
        lo.append(lax.bitcast_convert_type(lax.shift_left(w, 16), F32))
        hi.append(lax.bitcast_convert_type(w & jnp.int32(-65536), F32))
    return lo, hi


def _dot(a, b):
    return jnp.dot(a, b, preferred_element_type=F32)


def _sigmoid(x):
    return 1.0 / (1.0 + jnp.exp(-x))


def _gelu(x):
    return 0.5 * x * (1.0 + jnp.tanh(0.7978845608028654 * (x + 0.044715 * (x * x * x))))


def _rmsnorm(x, g):
    return x * lax.rsqrt(jnp.mean(x * x, axis=-1, keepdims=True) + NORM_EPS) * g


def _proj_kernel(x_ref, g_ref, w_ref, lng_ref, lnb_ref, ws_ref, bs_ref, wb_ref,
                 xa_ref, sga_ref, pb_ref, yb_scr):
    nb, tl, d = x_ref.shape
    m = nb * tl
    x = x_ref[...].reshape(m, d)
    xn = _rmsnorm(x, g_ref[...]).astype(BF16)

    xa = _dot(xn, w_ref[:, 0:SSM_WIDTH])
    xa_ref[...] = xa.astype(BF16).reshape(nb, tl, SSM_WIDTH)

    s_b = SSM_WIDTH + 2 * SGU_WIDTH
    s_g = s_b + D_MODEL
    ga = _dot(xn, w_ref[:, s_b:s_g])
    sga_ref[...] = _sigmoid(ga).astype(BF16).reshape(nb, tl, D_MODEL)

    z = _gelu(_dot(xn, w_ref[:, SSM_WIDTH:s_b]))
    u = z[:, :SGU_WIDTH]
    v = z[:, SGU_WIDTH:]
    mu = jnp.mean(v, axis=-1, keepdims=True)
    vc = v - mu
    v = vc * lax.rsqrt(jnp.mean(vc * vc, axis=-1, keepdims=True) + NORM_EPS)
    v = (v * lng_ref[...] + lnb_ref[...]).astype(BF16)

    ri = lax.broadcasted_iota(jnp.int32, (SGU_BLOCK, SGU_BLOCK), 0) // CHUNK
    ci = lax.broadcasted_iota(jnp.int32, (SGU_BLOCK, SGU_BLOCK), 1) // CHUNK
    causal = ri >= ci
    for h in range(SGU_HEADS):
        wm = jnp.where(causal, ws_ref[h], 0.0).astype(BF16)
        bias = bs_ref[:, h:h + 1]
        lo = h * SGU_HEAD_DIM
        for blk in range(m // SGU_BLOCK):
            r0 = blk * SGU_BLOCK
            s = _dot(wm, v[r0:r0 + SGU_BLOCK, lo:lo + SGU_HEAD_DIM]) + bias
            yb_scr[r0:r0 + SGU_BLOCK, lo:lo + SGU_HEAD_DIM] = (
                u[r0:r0 + SGU_BLOCK, lo:lo + SGU_HEAD_DIM] * s).astype(BF16)

    gb = _dot(xn, w_ref[:, s_g:])
    pb = _sigmoid(gb) * _dot(yb_scr[...], wb_ref[...])
    pb_ref[...] = pb.astype(BF16).reshape(nb, tl, D_MODEL)


def _proj_call(x, g, w_in, ln_g, ln_b, w_s, b_s_t, w_b):
    nb, seq, d = x.shape
    tl = TIME_TILE
    m = nb * tl
    const = lambda shape: pl.BlockSpec(shape, lambda i: (0,) * len(shape))
    tile = lambda width: pl.BlockSpec((nb, tl, width), lambda i: (0, i, 0))
    return pl.pallas_call(
        _proj_kernel,
        grid=(seq // tl,),
        in_specs=[tile(d), const((1, d)), const(w_in.shape), const((1, SGU_WIDTH)),
                  const((1, SGU_WIDTH)), const(w_s.shape), const(b_s_t.shape), const(w_b.shape)],
        out_specs=[tile(SSM_WIDTH), tile(D_MODEL), tile(D_MODEL)],
        out_shape=[jax.ShapeDtypeStruct((nb, seq, SSM_WIDTH), BF16),
                   jax.ShapeDtypeStruct((nb, seq, D_MODEL), BF16),
                   jax.ShapeDtypeStruct((nb, seq, D_MODEL), BF16)],
        scratch_shapes=[pltpu.VMEM((m, SGU_WIDTH), BF16)],
        compiler_params=pltpu.CompilerParams(
            dimension_semantics=("arbitrary",), vmem_limit_bytes=VMEM_LIMIT),
        name="proj",
    )(x, g, w_in, ln_g, ln_b, w_s, b_s_t, w_b)


def _mix_kernel(xa_ref, sga_ref, pb_ref, x_ref, state0_ref, upper_ref, bd_ref, cd_ref,
                are_ref, aim_ref, dskip_ref, wglu_ref, bglu_ref, wa_ref, wout_ref,
                g2_ref, xs_ref, wr_ref, br_ref,
                h_ref, xn2_ref, routet_ref, cnt_ref, state1_ref,
                bu0_scr, bu1_scr, x8_scr, y8_scr, state_scr, cnt_scr):
    nb, tl, d = x_ref.shape
    m = nb * tl

    @pl.when(pl.program_id(0) == 0)
    def _():
        state_scr[...] = state0_ref[...]
        cnt_scr[...] = jnp.zeros_like(cnt_scr)
        x8_scr[...] = jnp.zeros_like(x8_scr)

    xa = xa_ref[...].reshape(m, SSM_WIDTH).astype(F32)

    def chunk_lo(p, hf):
        return hf * (SSM_WIDTH // 2) + p * LANES

    scans = (bu0_scr, bu1_scr)
    for p in range(2):
        for hf in range(2):
            lo = chunk_lo(p, hf)
            for b in range(nb):
                x8_scr[2 * p + hf, pl.ds(hf * nb + b, tl, stride=SUBLANES), :] = (
                    xa[b * tl:(b + 1) * tl, lo:lo + LANES])
        x8 = jnp.concatenate([x8_scr[2 * p], x8_scr[2 * p + 1]], axis=1).astype(BF16)
        scans[p][...] = _dot(x8, bd_ref[p])

    a_re = [are_ref[p] for p in range(2)]
    a_im = [aim_ref[p] for p in range(2)]
    s_re = [state_scr[p][:, :SSM_WIDTH] for p in range(2)]
    s_im = [state_scr[p][:, SSM_WIDTH:] for p in range(2)]
    for t in range(tl):
        r = t * SUBLANES
        for p in range(2):
            bu_scr = scans[p]
            n_re = a_re[p] * s_re[p] - a_im[p] * s_im[p] + bu_scr[r:r + SUBLANES, 0:SSM_WIDTH]
            n_im = (a_re[p] * s_im[p] + a_im[p] * s_re[p]
                    + bu_scr[r:r + SUBLANES, SSM_WIDTH:2 * SSM_WIDTH])
            bu_scr[r:r + SUBLANES, 0:SSM_WIDTH] = n_re
            bu_scr[r:r + SUBLANES, SSM_WIDTH:2 * SSM_WIDTH] = n_im
            s_re[p], s_im[p] = n_re, n_im

    y_chunks = {}
    for p in range(2):
        state_scr[p] = jnp.concatenate([s_re[p], s_im[p]], axis=1)
        state1_ref[p] = state_scr[p]
        y8 = _dot(scans[p][...].astype(BF16), cd_ref[p])
        for hf in range(2):
            y8_scr[2 * p + hf] = y8[:, hf * LANES:(hf + 1) * LANES]
            y_chunks[(hf, p)] = jnp.concatenate(
                [y8_scr[2 * p + hf, pl.ds(hf * nb + b, tl, stride=SUBLANES), :]
                 for b in range(nb)], axis=0)
    y = jnp.concatenate([y_chunks[(hf, p)] for hf in range(2) for p in range(2)], axis=1)
    y = y + dskip_ref[...] * xa
    z = _gelu(y)
    ya = z * _sigmoid(_dot(z.astype(BF16), wglu_ref[...]) + bglu_ref[...])
    pa = sga_ref[...].reshape(m, d).astype(F32) * _dot(ya.astype(BF16), wa_ref[...])
    merged = pa + pb_ref[...].reshape(m, d).astype(F32)
    h = x_ref[...].reshape(m, d) + _dot(merged.astype(BF16), wout_ref[...])
    h_ref[...] = h
    xn2 = _rmsnorm(h, g2_ref[...])
    _pack_rows(xn2 * xs_ref[0:1, 0:1], xn2_ref)

    logits = _dot(xn2.astype(BF16), wr_ref[...]) + br_ref[...]
    work = jnp.transpose(logits)[:N_EXPERTS]
    expert = lax.broadcasted_iota(jnp.int32, (N_EXPERTS, m), 0).astype(F32)
    vals, idxs = [], []
    member = jnp.zeros((N_EXPERTS, m), F32)
    for _ in range(TOP_K):
        mx = jnp.max(work, axis=0, keepdims=True)
        ix = jnp.min(jnp.where(work == mx, expert, float(N_EXPERTS)), axis=0, keepdims=True)
        hit = expert == ix
        member = jnp.where(hit, 1.0, member)
        work = jnp.where(hit, NEG_BIG, work)
        vals.append(mx)
        idxs.append(ix)
    exps = [jnp.exp(v - vals[0]) for v in vals]
    denom = exps[0] + exps[1] + exps[2] + exps[3]
    gates = [e / denom for e in exps]

    before = _dot(member.astype(BF16), upper_ref[...]) + cnt_scr[:, 0:1]
    new_cnt = cnt_scr[:, 0:1] + jnp.sum(member, axis=1, keepdims=True)
    cnt_scr[...] = jnp.broadcast_to(new_cnt, cnt_scr.shape)
    cnt_ref[...] = jnp.broadcast_to(new_cnt, cnt_ref.shape)
    ranks = [jnp.sum(jnp.where(expert == ix, before, 0.0), axis=0, keepdims=True) for ix in idxs]
    pad = jnp.zeros((ROUTE_ROWS - 3 * TOP_K, m), F32)
    routet_ref[...] = jnp.concatenate(idxs + gates + ranks + [pad], axis=0)


def _mix_call(first, n_tiles, xa, sga, pb, x, state0, upper, bd, cd, a_re, a_im, dskip,
              w_glu, b_glu, w_a, w_out, g2, x_scale, w_r, b_r):
    nb, seq, d = x.shape
    tl = TIME_TILE
    m = nb * tl
    const = lambda shape: pl.BlockSpec(shape, lambda i: (0,) * len(shape))
    tile = lambda width: pl.BlockSpec((nb, tl, width), lambda i: (0, first + i, 0))
    rows = lambda width: pl.BlockSpec((m, width), lambda i: (i, 0))
    operands = (xa, sga, pb, x, state0, upper, bd, cd, a_re, a_im, dskip, w_glu, b_glu, w_a, w_out,
                g2, x_scale, w_r, b_r)
    in_specs = [tile(SSM_WIDTH), tile(d), tile(d), tile(d)] + [const(o.shape) for o in operands[4:]]
    return pl.pallas_call(
        _mix_kernel,
        grid=(n_tiles,),
        in_specs=in_specs,
        out_specs=[rows(d), pl.BlockSpec((SUBROWS, m, LANES), lambda i: (0, i, 0)),
                   pl.BlockSpec((ROUTE_ROWS, m), lambda i: (0, i)), const((N_EXPERTS, LANES)),
                   const(state0.shape)],
        out_shape=[jax.ShapeDtypeStruct((n_tiles * m, d), F32),
                   jax.ShapeDtypeStruct((SUBROWS, n_tiles * m, LANES), jnp.int32),
                   jax.ShapeDtypeStruct((ROUTE_ROWS, n_tiles * m), F32),
                   jax.ShapeDtypeStruct((N_EXPERTS, LANES), F32),
                   jax.ShapeDtypeStruct(state0.shape, F32)],
        scratch_shapes=[pltpu.VMEM((SUBLANES * tl, 2 * SSM_WIDTH), F32),
                        pltpu.VMEM((SUBLANES * tl, 2 * SSM_WIDTH), F32),
                        pltpu.VMEM((4, SUBLANES * tl, LANES), F32),
                        pltpu.VMEM((4, SUBLANES * tl, LANES), F32),
                        pltpu.VMEM((2, SUBLANES, 2 * SSM_WIDTH), F32),
                        pltpu.VMEM((N_EXPERTS, LANES), F32)],
        compiler_params=pltpu.CompilerParams(
            dimension_semantics=("arbitrary",), vmem_limit_bytes=VMEM_LIMIT),
        name="mix",
    )(*operands)


def _sc_mesh():
    return plsc.VectorSubcoreMesh(core_axis_name="c", subcore_axis_name="s")


def _worker_id():
    return lax.axis_index("s") * SC_CORES + lax.axis_index("c")


def _dispatch_body(src_hbm, idx_hbm, out_hbm, idx_v, buf, sem):
    n_chunks = idx_v.shape[0] // TOP_K
    wid = _worker_id()
    base = wid * (n_chunks * LANES)
    pltpu.sync_copy(idx_hbm.at[wid], idx_v)

    @pl.loop(0, n_chunks)
    def _(j):
        pltpu.sync_copy(src_hbm.at[pl.ds(base + j * LANES, LANES)], buf)
        copies = [pltpu.async_copy(buf, out_hbm.at[idx_v.at[j * TOP_K + k]], sem)
                  for k in range(TOP_K)]
        for c in copies:
            c.wait()


def _sc_dispatch(src, idx, n_out):
    n_chunks = src.shape[0] // (SC_WORKERS * LANES)
    return pl.kernel(
        _dispatch_body,
        out_type=jax.ShapeDtypeStruct((n_out, LANES), jnp.int32),
        mesh=_sc_mesh(),
        scratch_types=[pltpu.VMEM((n_chunks * TOP_K, LANES), jnp.int32),
                       pltpu.VMEM((LANES, LANES), jnp.int32),
                       pltpu.SemaphoreType.DMA],
        name="sc_dispatch",
    )(src, idx)


def _gather_body(tab_hbm, idx_hbm, out_hbm, idx_v, buf0, buf1, sem0, sem1):
    n_chunks = idx_v.shape[0]
    wid = _worker_id()
    base = wid * (n_chunks * LANES)
    pltpu.sync_copy(idx_hbm.at[wid], idx_v)

    @pl.loop(0, n_chunks, step=2)
    def _(j):
        c0 = pltpu.async_copy(tab_hbm.at[idx_v.at[j]], buf0, sem0)
        c1 = pltpu.async_copy(tab_hbm.at[idx_v.at[j + 1]], buf1, sem1)
        c0.wait()
        pltpu.sync_copy(buf0, out_hbm.at[pl.ds(base + j * LANES, LANES)])
        c1.wait()
        pltpu.sync_copy(buf1, out_hbm.at[pl.ds(base + (j + 1) * LANES, LANES)])


def _sc_gather(tab, idx):
    n_chunks = idx.shape[1]
    return pl.kernel(
        _gather_body,
        out_type=jax.ShapeDtypeStruct((SC_WORKERS * n_chunks * LANES, LANES), jnp.int32),
        mesh=_sc_mesh(),
        scratch_types=[pltpu.VMEM((n_chunks, LANES), jnp.int32),
                       pltpu.VMEM((LANES, LANES), jnp.int32),
                       pltpu.VMEM((LANES, LANES), jnp.int32),
                       pltpu.SemaphoreType.DMA, pltpu.SemaphoreType.DMA],
        name="sc_gather",
    )(tab, idx)


def _expert_kernel(be_ref, bv_ref, nx_ref, sl_ref, x_ref, xs_ref, bg_ref, bu_ref, bd_ref,
                   wg_hbm, wu_hbm, wd_hbm, y_ref, stage, wg_scr, wu_scr, wd_scr, inv_scr, sems):
    i = pl.program_id(0)
    valid = bv_ref[i]
    first = jnp.logical_and(
        valid > 0, jnp.logical_or(i == 0, be_ref[i] != be_ref[jnp.maximum(i - 1, 0)]))

    def weight_copies(expert, slot):
        return [pltpu.make_async_copy(w.at[expert], stage.at[slot, j], sems.at[slot, j])
                for j, w in enumerate((wg_hbm, wu_hbm, wd_hbm))]

    @pl.when(i == 0)
    def _():
        for c in weight_copies(be_ref[0], sl_ref[0]):
            c.start()

    @pl.when(first)
    def _():
        slot = sl_ref[i]
        for c in weight_copies(be_ref[i], slot):
            c.wait()
        for j, scr in enumerate((wg_scr, wu_scr, wd_scr)):
            w = stage[slot, j]
            amax = jnp.max(jnp.max(jnp.abs(w), axis=0, keepdims=True), axis=1, keepdims=True)
            scale = F8_TARGET / jnp.maximum(amax, TINY)
            scr[...] = (w * scale).astype(F8)
            inv_scr[j] = jnp.broadcast_to(1.0 / scale, inv_scr.shape[1:])

        @pl.when(nx_ref[i] >= 0)
        def _():
            for c in weight_copies(nx_ref[i], 1 - slot):
                c.start()

    @pl.when(valid > 0)
    def _():
        lo, hi = _unpack_rows(x_ref)
        live = lax.broadcasted_iota(jnp.int32, (ROW_BLOCK, 1), 0) < valid
        x = jnp.where(live, jnp.concatenate(lo + hi, axis=1), 0.0).astype(F8)
        inv_x = 1.0 / xs_ref[0:1, 0:1]
        cg = (inv_scr[0, 0:1, 0:1] * inv_x).astype(BF16)
        cl = (inv_scr[1, 0:1, 0:1] * inv_x * HID_SCALE).astype(BF16)
        g = _dot(x, wg_scr[...]).astype(BF16) * cg + bg_ref[...].astype(BF16)
        g = jnp.minimum(g, SWIGLU_LIMIT)
        l = _dot(x, wu_scr[...]).astype(BF16) * cl + (bu_ref[...] * HID_SCALE).astype(BF16)
        l = jnp.clip(l, -SWIGLU_LIMIT * HID_SCALE, SWIGLU_LIMIT * HID_SCALE) + HID_SCALE
        hid = g * l / (1.0 + jnp.exp2(g * (-SWIGLU_ALPHA * LOG2_E)))
        y = _dot(hid.astype(F8), wd_scr[...])
        _pack_rows(y * (inv_scr[2, 0:1, 0:1] * (1.0 / HID_SCALE)) + bd_ref[...], y_ref)

    @pl.when(valid <= 0)
    def _():
        y_ref[...] = jnp.zeros_like(y_ref)


def _expert_call(block_expert, block_valid, next_expert, slot, x_tab, x_scale,
                 w_gate, b_gate, w_up, b_up, w_down, b_down):
    d, f = w_gate.shape[-2:]
    assert d == f
    n_blocks = block_expert.shape[0]
    bspec = lambda width: pl.BlockSpec((None, 1, width), lambda i, be, bv, nx, sl: (be[i], 0, 0))
    rows = pl.BlockSpec((SUBROWS, ROW_BLOCK, LANES), lambda i, be, bv, nx, sl: (0, i, 0))
    hbm = pl.BlockSpec(memory_space=pl.ANY)
    grid_spec = pltpu.PrefetchScalarGridSpec(
        num_scalar_prefetch=4,
        grid=(n_blocks,),
        in_specs=[rows, pl.BlockSpec((SUBLANES, LANES), lambda i, be, bv, nx, sl: (0, 0)),
                  bspec(f), bspec(f), bspec(d), hbm, hbm, hbm],
        out_specs=rows,
        scratch_shapes=[pltpu.VMEM((2, 3, d, f), F32),
                        pltpu.VMEM((d, f), F8), pltpu.VMEM((d, f), F8), pltpu.VMEM((f, d), F8),
                        pltpu.VMEM((3, SUBLANES, LANES), F32), pltpu.SemaphoreType.DMA((2, 3))],
    )
    return pl.pallas_call(
        _expert_kernel,
        grid_spec=grid_spec,
        out_shape=jax.ShapeDtypeStruct(x_tab.shape, jnp.int32),
        compiler_params=pltpu.CompilerParams(
            dimension_semantics=("arbitrary",), vmem_limit_bytes=VMEM_LIMIT),
        name="expert",
    )(block_expert, block_valid, next_expert, slot, x_tab, x_scale, b_gate, b_up, b_down,
      w_gate, w_up, w_down)


def _combine_kernel(h_ref, routet_ref, gf_ref, ys_ref, *rest):
    out_ref = rest[-1]
    nb, tl, d = out_ref.shape
    m = nb * tl
    h = h_ref[...]
    route = jnp.transpose(jnp.concatenate(
        [routet_ref[...], jnp.zeros((LANES - ROUTE_ROWS, m), F32)], axis=0))
    lo_acc = [h[:, s * LANES:(s + 1) * LANES] for s in range(SUBROWS)]
    hi_acc = [h[:, PACKED + s * LANES:PACKED + (s + 1) * LANES] for s in range(SUBROWS)]
    for k in range(TOP_K):
        gate = route[:, TOP_K + k:TOP_K + k + 1]
        lo, hi = _unpack_rows(ys_ref.at[k])
        lo_acc = [a + gate * v for a, v in zip(lo_acc, lo)]
        hi_acc = [a + gate * v for a, v in zip(hi_acc, hi)]
    acc = jnp.concatenate(lo_acc + hi_acc, axis=1)
    out_ref[...] = _rmsnorm(acc, gf_ref[...]).reshape(nb, tl, d)


def _combine_call(h, route_t, g_final, y_slots, nb, seq, in_first, out_first, n, prev):
    d = h.shape[1]
    tl = TIME_TILE
    m = nb * tl
    first = out_first
    in_specs = [pl.BlockSpec((m, d), lambda i: (in_first + i, 0)),
                pl.BlockSpec((ROUTE_ROWS, m), lambda i: (0, in_first + i)),
                pl.BlockSpec((1, d), lambda i: (0, 0)),
                pl.BlockSpec((TOP_K, SUBROWS, m, LANES), lambda i: (0, 0, i, 0))]
    operands = [h, route_t, g_final, y_slots]
    aliases = {}
    if prev is not None:
        in_specs.append(pl.BlockSpec(memory_space=pl.ANY))
        operands.append(prev)
        aliases = {4: 0}
    return pl.pallas_call(
        _combine_kernel,
        grid=(n,),
        in_specs=in_specs,
        out_specs=pl.BlockSpec((nb, tl, d), lambda i: (0, first + i, 0)),
        out_shape=jax.ShapeDtypeStruct((nb, seq, d), F32),
        input_output_aliases=aliases,
        compiler_params=pltpu.CompilerParams(
            dimension_semantics=("arbitrary",), vmem_limit_bytes=VMEM_LIMIT),
        name="combine",
    )(*operands)


def _s5_operands(lam_re, lam_im, log_dt, b_re, b_im, c_re, c_im, nb):
    gp, ns, pg = SSM_GROUPS, SSM_STATE, SSM_GROUP
    lam = lax.complex(lam_re.astype(F32), lam_im.astype(F32))
    dt = jnp.exp(log_dt.astype(F32))[:, None]
    lam_bar = jnp.exp(lam * dt)
    b_bar = ((lam_bar - 1.0) / lam)[..., None] * lax.complex(b_re.astype(F32), b_im.astype(F32))
    grp = lambda hf, p: slice(16 * hf + 8 * p, 16 * hf + 8 * p + 8)
    eye8 = jnp.eye(8, dtype=F32)

    def b_block(bb):
        return jnp.einsum('gnq,gh->gqhn', bb, eye8).reshape(8 * pg, 8 * ns)

    def c_block(cc):
        return jnp.einsum('gqn,gh->gnhq', cc, eye8).reshape(8 * ns, 8 * pg)

    bd, cd, a_re, a_im = [], [], [], []
    for p in range(2):
        rows = []
        for hf in range(2):
            bb = b_bar[grp(hf, p)]
            rows.append(jnp.concatenate([b_block(bb.real), b_block(bb.imag)], axis=1))
        bd.append(jnp.concatenate(rows, axis=0))
        cols = []
        for hf in range(2):
            cols.append(jnp.concatenate([c_block(c_re[grp(hf, p)].astype(F32)),
                                         -c_block(c_im[grp(hf, p)].astype(F32))], axis=0))
        cd.append(jnp.concatenate(cols, axis=1))
        lam_rows = jnp.stack([lam_bar[grp(j // nb, p)].reshape(8 * ns) for j in range(SUBLANES)])
        a_re.append(lam_rows.real)
        a_im.append(lam_rows.imag)
    bd = jnp.stack(bd).astype(BF16)
    cd = jnp.stack(cd).astype(BF16)
    a_re = jnp.stack(a_re)
    a_im = jnp.stack(a_im)

    return bd, cd, a_re, a_im


def kernel(x, norm_mix_g, w_in, lam_re, lam_im, log_dt, b_re, b_im, c_re, c_im, d_skip, w_glu, b_glu, sgu_ln_g, sgu_ln_b, w_s, b_s, w_branch_a, w_branch_b, w_out, norm_moe_g, w_router, b_router, w_gate, b_gate, w_up, b_up, w_down, b_down, norm_final_g):
    nb, seq, d = x.shape
    assert d == D_MODEL and SUBLANES % nb == 0 and SUBLANES // nb == 2
    assert seq % TIME_TILE == 0 and norm_mix_g.shape[0] == 1
    n_sub = TIME_PARTS * COMBINE_PARTS
    assert (nb * seq * SUBROWS) % (SC_WORKERS * LANES * TIME_PARTS) == 0
    assert (nb * seq * SUBROWS * TOP_K) % (SC_WORKERS * LANES * 2 * n_sub) == 0
    assert (seq // TIME_TILE) % n_sub == 0
    tl = TIME_TILE
    row = lambda v: v.reshape(1, -1).astype(F32)

    xa, sga, pb = _proj_call(
        x, row(norm_mix_g[0]), w_in[0].astype(BF16), row(sgu_ln_g[0]), row(sgu_ln_b[0]),
        w_s[0].astype(F32), b_s[0].T.astype(F32), w_branch_b[0].astype(BF16))

    bd, cd, a_re, a_im = _s5_operands(
        lam_re[0], lam_im[0], log_dt[0], b_re[0], b_im[0], c_re[0], c_im[0], nb)
    w_r = jnp.zeros((d, LANES), F32).at[:, :N_EXPERTS].set(w_router[0].astype(F32)).astype(BF16)
    b_r = jnp.full((1, LANES), NEG_BIG, F32).at[0, :N_EXPERTS].set(b_router[0].astype(F32))
    x_bound = math.sqrt(d) * jnp.max(jnp.abs(norm_moe_g[0].astype(F32)))
    x_scale = jnp.exp2(jnp.floor(jnp.log2(F8_TARGET / jnp.maximum(x_bound, TINY))))
    x_scale = jnp.full((SUBLANES, LANES), 1.0, F32) * x_scale
    m = nb * tl
    upper = (jnp.arange(m)[:, None] < jnp.arange(m)[None, :]).astype(BF16)
    experts = jnp.arange(N_EXPERTS, dtype=jnp.int32)
    tiles_part = seq // tl // TIME_PARTS
    tp = tiles_part * m
    n_blocks = (tp * TOP_K) // ROW_BLOCK + N_EXPERTS
    n_rows = n_blocks * ROW_BLOCK
    block_row0 = jnp.arange(n_blocks, dtype=jnp.int32) * ROW_BLOCK

    state = jnp.zeros((2, SUBLANES, 2 * SSM_WIDTH), F32)
    parts = []
    for part in range(TIME_PARTS):
        h, xn2p, route_t, cnt, state = _mix_call(
            part * tiles_part, tiles_part, xa, sga, pb, x, state, upper, bd, cd, a_re, a_im,
            row(d_skip[0]), w_glu[0].astype(BF16), row(b_glu[0]), w_branch_a[0].astype(BF16),
            w_out[0].astype(BF16), row(norm_moe_g[0]), x_scale, w_r, b_r)

        idx_t = route_t[0:TOP_K].astype(jnp.int32)
        rank_t = route_t[2 * TOP_K:3 * TOP_K].astype(jnp.int32)
        counts = cnt[:, 0].astype(jnp.int32)
        padded = (counts + ROW_BLOCK - 1) // ROW_BLOCK * ROW_BLOCK
        cum = jnp.cumsum(padded)
        pstart = cum - padded
        dest_t = rank_t
        for e in range(N_EXPERTS):
            dest_t = dest_t + jnp.where(idx_t == e, pstart[e], 0)
        block_expert = jnp.minimum(
            jnp.sum((cum[None, :] <= block_row0[:, None]).astype(jnp.int32), axis=1),
            N_EXPERTS - 1)
        block_valid = jnp.clip(
            counts[block_expert] - (block_row0 - pstart[block_expert]), 0, ROW_BLOCK)
        present = counts > 0
        slot_e = (jnp.cumsum(present.astype(jnp.int32)) - 1) % 2
        later = lax.cummin(jnp.where(present, experts, N_EXPERTS)[::-1])[::-1]
        next_e = jnp.concatenate([later[1:], jnp.full((1,), N_EXPERTS, jnp.int32)])
        next_e = jnp.where(next_e == N_EXPERTS, -1, next_e)

        sub = dest_t[:, None, :] + (jnp.arange(SUBROWS, dtype=jnp.int32) * n_rows)[None, :, None]
        idx_dispatch = sub.reshape(TOP_K, SUBROWS * tp // LANES, LANES).transpose(1, 0, 2)
        idx_dispatch = idx_dispatch.reshape(SC_WORKERS, -1, LANES)

        x_tab = _sc_dispatch(xn2p.reshape(SUBROWS * tp, LANES), idx_dispatch, SUBROWS * n_rows)
        y_tab = _expert_call(
            block_expert, block_valid, next_e[block_expert], slot_e[block_expert],
            x_tab.reshape(SUBROWS, n_rows, LANES), x_scale,
            w_gate[0], b_gate[0][:, None, :], w_up[0], b_up[0][:, None, :],
            w_down[0], b_down[0][:, None, :])
        parts.append((h, route_t, sub, y_tab.reshape(SUBROWS * n_rows, LANES)))

    out = None
    tiles_sub = tiles_part // COMBINE_PARTS
    ts = tiles_sub * m
    for part, (h, route_t, sub, y_flat) in enumerate(parts):
        idx_parts = sub.reshape(TOP_K, SUBROWS, COMBINE_PARTS, ts).transpose(2, 0, 1, 3)
        for q in range(COMBINE_PARTS):
            y_slots = _sc_gather(y_flat, idx_parts[q].reshape(SC_WORKERS, -1, LANES))
            out = _combine_call(h, route_t, row(norm_final_g),
                                y_slots.reshape(TOP_K, SUBROWS, ts, LANES), nb, seq,
                                q * tiles_sub, part * tiles_part + q * tiles_sub, tiles_sub, out)
    return out
```

```python
import math

import jax
import jax.numpy as jnp
from jax import lax
from jax.experimental import pallas as pl
from jax.experimental.pallas import tpu as pltpu
from jax.experimental.pallas import tpu_sc as plsc

F32 = jnp.float32
BF16 = jnp.bfloat16
F8 = jnp.float8_e4m3fn
F8_TARGET = 240.0
TINY = 1e-30

NORM_EPS = 1e-5
D_MODEL = 1024
SSM_WIDTH = 512
SSM_GROUP = 16
SSM_GROUPS = 32
SSM_STATE = 64
SGU_WIDTH = 512
SGU_BLOCK = 128
SGU_HEADS = 4
SGU_HEAD_DIM = 128
CHUNK = 64
N_EXPERTS = 32
TOP_K = 4
SWIGLU_ALPHA = 1.702
SWIGLU_LIMIT = 7.0
LOG2_E = 1.4426950408889634
HID_SCALE = 4.0

LANES = 128
SUBLANES = 8
TIME_TILE = 128
SCAN_BLOCK = 64
ROW_BLOCK = 512
VMEM_LIMIT = 56 * 1024 * 1024
NEG_BIG = -1e30
SC_CORES = 2
SC_SUBCORES = 16
SC_WORKERS = SC_CORES * SC_SUBCORES
SUBROWS = 4
COMBINE_PARTS = 4
ROUTE_ROWS = 16
PACKED = D_MODEL // 2


def _pack_rows(x, out_ref):
    lo = lax.bitcast_convert_type(x[:, :PACKED].astype(BF16).astype(F32), jnp.int32)
    hi = lax.bitcast_convert_type(x[:, PACKED:].astype(BF16).astype(F32), jnp.int32)
    words = lax.shift_right_logical(lo, 16) | (hi & jnp.int32(-65536))
    for s in range(SUBROWS):
        out_ref[s] = words[:, s * LANES:(s + 1) * LANES]


def _unpack_rows(ref):
    lo, hi = [], []
    for s in range(SUBROWS):
        w = ref[s]
        lo.append(lax.bitcast_convert_type(lax.shift_left(w, 16), F32))
        hi.append(lax.bitcast_convert_type(w & jnp.int32(-65536), F32))
    return lo, hi


def _dot(a, b):
    return jnp.dot(a, b, preferred_element_type=F32)


def _sigmoid(x):
    return 1.0 / (1.0 + jnp.exp(-x))


def _gelu(x):
    return 0.5 * x * (1.0 + jnp.tanh(0.7978845608028654 * (x + 0.044715 * (x * x * x))))


def _rmsnorm(x, g):
    return x * lax.rsqrt(jnp.mean(x * x, axis=-1, keepdims=True) + NORM_EPS) * g


def _proj_body(x, g_ref, w_ref, lng_ref, lnb_ref, ws_ref, bs_ref, wb_ref, yb_scr):
    m = x.shape[0]
    xn = _rmsnorm(x, g_ref[...]).astype(BF16)

    xa = _dot(xn, w_ref[:, 0:SSM_WIDTH]).astype(BF16)

    s_b = SSM_WIDTH + 2 * SGU_WIDTH
    s_g = s_b + D_MODEL
    sga = _sigmoid(_dot(xn, w_ref[:, s_b:s_g])).astype(BF16)

    z = _gelu(_dot(xn, w_ref[:, SSM_WIDTH:s_b]))
    u = z[:, :SGU_WIDTH]
    v = z[:, SGU_WIDTH:]
    mu = jnp.mean(v, axis=-1, keepdims=True)
    vc = v - mu
    v = vc * lax.rsqrt(jnp.mean(vc * vc, axis=-1, keepdims=True) + NORM_EPS)
    v = (v * lng_ref[...] + lnb_ref[...]).astype(BF16)

    ri = lax.broadcasted_iota(jnp.int32, (SGU_BLOCK, SGU_BLOCK), 0) // CHUNK
    ci = lax.broadcasted_iota(jnp.int32, (SGU_BLOCK, SGU_BLOCK), 1) // CHUNK
    causal = ri >= ci
    for h in range(SGU_HEADS):
        wm = jnp.where(causal, ws_ref[h], 0.0).astype(BF16)
        bias = bs_ref[:, h:h + 1]
        lo = h * SGU_HEAD_DIM
        for blk in range(m // SGU_BLOCK):
            r0 = blk * SGU_BLOCK
            s = _dot(wm, v[r0:r0 + SGU_BLOCK, lo:lo + SGU_HEAD_DIM]) + bias
            yb_scr[r0:r0 + SGU_BLOCK, lo:lo + SGU_HEAD_DIM] = (
                u[r0:r0 + SGU_BLOCK, lo:lo + SGU_HEAD_DIM] * s).astype(BF16)

    gb = _dot(xn, w_ref[:, s_g:])
    pb = (_sigmoid(gb) * _dot(yb_scr[...], wb_ref[...])).astype(BF16)
    return xa, sga, pb


def _mix_kernel(x_ref, g1_ref, win_ref, lng_ref, lnb_ref, ws_ref, bs_ref, wb_ref,
                upper_ref, bd_ref, cd_ref,
                are_ref, aim_ref, dskip_ref, wglu_ref, bglu_ref, wa_ref, wout_ref,
                g2_ref, xs_ref, wr_ref, br_ref,
                h_ref, xn2_ref, routet_ref, cnt_ref,
                bu0_scr, bu1_scr, x8_scr, y8_scr, yb_scr, state_scr, cnt_scr):
    nb, tl, d = x_ref.shape
    m = nb * tl
    x = x_ref[...].reshape(m, d)
    xa_bf, sga, pb = _proj_body(x, g1_ref, win_ref, lng_ref, lnb_ref, ws_ref, bs_ref, wb_ref,
                                yb_scr)

    @pl.when(pl.program_id(0) == 0)
    def _():
        state_scr[...] = jnp.zeros_like(state_scr)
        cnt_scr[...] = jnp.zeros_like(cnt_scr)
        x8_scr[...] = jnp.zeros_like(x8_scr)

    xa = xa_bf.astype(F32)

    def chunk_lo(p, hf):
        return hf * (SSM_WIDTH // 2) + p * LANES

    scans = (bu0_scr, bu1_scr)
    for p in range(2):
        for hf in range(2):
            lo = chunk_lo(p, hf)
            for b in range(nb):
                x8_scr[2 * p + hf, pl.ds(hf * nb + b, tl, stride=SUBLANES), :] = (
                    xa[b * tl:(b + 1) * tl, lo:lo + LANES])
        x8 = jnp.concatenate([x8_scr[2 * p], x8_scr[2 * p + 1]], axis=1).astype(BF16)
        scans[p][...] = _dot(x8, bd_ref[p])

    a_re = [are_ref[p] for p in range(2)]
    a_im = [aim_ref[p] for p in range(2)]
    s_re = [state_scr[p][:, :SSM_WIDTH] for p in range(2)]
    s_im = [state_scr[p][:, SSM_WIDTH:] for p in range(2)]
    for t0 in range(0, tl, SCAN_BLOCK):
        rows_p = ([], [])
        for t in range(t0, t0 + SCAN_BLOCK):
            r = t * SUBLANES
            for p in range(2):
                bu_scr = scans[p]
                n_re = a_re[p] * s_re[p] - a_im[p] * s_im[p] + bu_scr[r:r + SUBLANES, 0:SSM_WIDTH]
                n_im = (a_re[p] * s_im[p] + a_im[p] * s_re[p]
                        + bu_scr[r:r + SUBLANES, SSM_WIDTH:2 * SSM_WIDTH])
                s_re[p], s_im[p] = n_re, n_im
                rows_p[p].append(jnp.concatenate([n_re, n_im], axis=1))
        r0 = t0 * SUBLANES
        for p in range(2):
            y8 = _dot(jnp.concatenate(rows_p[p], axis=0).astype(BF16), cd_ref[p])
            for hf in range(2):
                y8_scr[2 * p + hf, r0:r0 + SCAN_BLOCK * SUBLANES, :] = (
                    y8[:, hf * LANES:(hf + 1) * LANES])

    y_chunks = {}
    for p in range(2):
        state_scr[p] = jnp.concatenate([s_re[p], s_im[p]], axis=1)
        for hf in range(2):
            y_chunks[(hf, p)] = jnp.concatenate(
                [y8_scr[2 * p + hf, pl.ds(hf * nb + b, tl, stride=SUBLANES), :]
                 for b in range(nb)], axis=0)
    y = jnp.concatenate([y_chunks[(hf, p)] for hf in range(2) for p in range(2)], axis=1)
    y = y + dskip_ref[...] * xa
    z = _gelu(y)
    ya = z * _sigmoid(_dot(z.astype(BF16), wglu_ref[...]) + bglu_ref[...])
    pa = sga.astype(F32) * _dot(ya.astype(BF16), wa_ref[...])
    merged = pa + pb.astype(F32)
    h = x + _dot(merged.astype(BF16), wout_ref[...])
    h_ref[...] = h
    xn2 = _rmsnorm(h, g2_ref[...])
    _pack_rows(xn2 * xs_ref[0:1, 0:1], xn2_ref)

    logits = _dot(xn2.astype(BF16), wr_ref[...]) + br_ref[...]
    work = jnp.transpose(logits)[:N_EXPERTS]
    expert = lax.broadcasted_iota(jnp.int32, (N_EXPERTS, m), 0).astype(F32)
    vals, idxs = [], []
    member = jnp.zeros((N_EXPERTS, m), F32)
    for _ in range(TOP_K):
        mx = jnp.max(work, axis=0, keepdims=True)
        ix = jnp.min(jnp.where(work == mx, expert, float(N_EXPERTS)), axis=0, keepdims=True)
        hit = expert == ix
        member = jnp.where(hit, 1.0, member)
        work = jnp.where(hit, NEG_BIG, work)
        vals.append(mx)
        idxs.append(ix)
    exps = [jnp.exp(v - vals[0]) for v in vals]
    denom = exps[0] + exps[1] + exps[2] + exps[3]
    gates = [e / denom for e in exps]

    before = _dot(member.astype(BF16), upper_ref[...]) + cnt_scr[:, 0:1]
    new_cnt = cnt_scr[:, 0:1] + jnp.sum(member, axis=1, keepdims=True)
    cnt_scr[...] = jnp.broadcast_to(new_cnt, cnt_scr.shape)
    cnt_ref[...] = jnp.broadcast_to(new_cnt, cnt_ref.shape)
    ranks = [jnp.sum(jnp.where(expert == ix, before, 0.0), axis=0, keepdims=True) for ix in idxs]
    pad = jnp.zeros((ROUTE_ROWS - 3 * TOP_K, m), F32)
    routet_ref[...] = jnp.concatenate(idxs + gates + ranks + [pad], axis=0)


def _mix_call(x, *params):
    nb, seq, d = x.shape
    tl = TIME_TILE
    m = nb * tl
    n_tiles = seq // tl
    const = lambda shape: pl.BlockSpec(shape, lambda i: (0,) * len(shape))
    resident = lambda shape: pl.BlockSpec(shape, lambda i: (0,) * len(shape),
                                          pipeline_mode=pl.Buffered(1))
    tile = lambda width: pl.BlockSpec((nb, tl, width), lambda i: (0, i, 0))
    rows = lambda width: pl.BlockSpec((m, width), lambda i: (i, 0))
    operands = (x,) + params
    in_specs = [tile(d)] + [resident(o.shape) for o in params]
    return pl.pallas_call(
        _mix_kernel,
        grid=(n_tiles,),
        in_specs=in_specs,
        out_specs=[rows(d), pl.BlockSpec((SUBROWS, m, LANES), lambda i: (0, i, 0)),
                   pl.BlockSpec((ROUTE_ROWS, m), lambda i: (0, i)), const((N_EXPERTS, LANES))],
        out_shape=[jax.ShapeDtypeStruct((n_tiles * m, d), F32),
                   jax.ShapeDtypeStruct((SUBROWS, n_tiles * m, LANES), jnp.int32),
                   jax.ShapeDtypeStruct((ROUTE_ROWS, n_tiles * m), F32),
                   jax.ShapeDtypeStruct((N_EXPERTS, LANES), F32)],
        scratch_shapes=[pltpu.VMEM((SUBLANES * tl, 2 * SSM_WIDTH), F32),
                        pltpu.VMEM((SUBLANES * tl, 2 * SSM_WIDTH), F32),
                        pltpu.VMEM((4, SUBLANES * tl, LANES), F32),
                        pltpu.VMEM((4, SUBLANES * tl, LANES), F32),
                        pltpu.VMEM((m, SGU_WIDTH), BF16),
                        pltpu.VMEM((2, SUBLANES, 2 * SSM_WIDTH), F32),
                        pltpu.VMEM((N_EXPERTS, LANES), F32)],
        compiler_params=pltpu.CompilerParams(
            dimension_semantics=("arbitrary",), vmem_limit_bytes=VMEM_LIMIT),
        name="mix",
    )(*operands)


def _sc_mesh():
    return plsc.VectorSubcoreMesh(core_axis_name="c", subcore_axis_name="s")


def _worker_id():
    return lax.axis_index("s") * SC_CORES + lax.axis_index("c")


def _dispatch_body(src_hbm, idx_hbm, out_hbm, idx_v, buf, sem):
    n_chunks = idx_v.shape[0] // TOP_K
    wid = _worker_id()
    base = wid * (n_chunks * LANES)
    pltpu.sync_copy(idx_hbm.at[wid], idx_v)

    @pl.loop(0, n_chunks)
    def _(j):
        pltpu.sync_copy(src_hbm.at[pl.ds(base + j * LANES, LANES)], buf)
        copies = [pltpu.async_copy(buf, out_hbm.at[idx_v.at[j * TOP_K + k]], sem)
                  for k in range(TOP_K)]
        for c in copies:
            c.wait()


def _sc_dispatch(src, idx, n_out):
    n_chunks = src.shape[0] // (SC_WORKERS * LANES)
    return pl.kernel(
        _dispatch_body,
        out_type=jax.ShapeDtypeStruct((n_out, LANES), jnp.int32),
        mesh=_sc_mesh(),
        scratch_types=[pltpu.VMEM((n_chunks * TOP_K, LANES), jnp.int32),
                       pltpu.VMEM((LANES, LANES), jnp.int32),
                       pltpu.SemaphoreType.DMA],
        name="sc_dispatch",
    )(src, idx)


def _gather_body(tab_hbm, idx_hbm, out_hbm, idx_v, buf0, buf1, sem0, sem1):
    n_chunks = idx_v.shape[0]
    wid = _worker_id()
    base = wid * (n_chunks * LANES)
    pltpu.sync_copy(idx_hbm.at[wid], idx_v)

    @pl.loop(0, n_chunks, step=2)
    def _(j):
        c0 = pltpu.async_copy(tab_hbm.at[idx_v.at[j]], buf0, sem0)
        c1 = pltpu.async_copy(tab_hbm.at[idx_v.at[j + 1]], buf1, sem1)
        c0.wait()
        pltpu.sync_copy(buf0, out_hbm.at[pl.ds(base + j * LANES, LANES)])
        c1.wait()
        pltpu.sync_copy(buf1, out_hbm.at[pl.ds(base + (j + 1) * LANES, LANES)])


def _sc_gather(tab, idx):
    n_chunks = idx.shape[1]
    return pl.kernel(
        _gather_body,
        out_type=jax.ShapeDtypeStruct((SC_WORKERS * n_chunks * LANES, LANES), jnp.int32),
        mesh=_sc_mesh(),
        scratch_types=[pltpu.VMEM((n_chunks, LANES), jnp.int32),
                       pltpu.VMEM((LANES, LANES), jnp.int32),
                       pltpu.VMEM((LANES, LANES), jnp.int32),
                       pltpu.SemaphoreType.DMA, pltpu.SemaphoreType.DMA],
        name="sc_gather",
    )(tab, idx)


def _expert_kernel(be_ref, bv_ref, nx_ref, sl_ref, x_ref, xs_ref, bg_ref, bu_ref, bd_ref,
                   wg_hbm, wu_hbm, wd_hbm, y_ref, stage, wg_scr, wu_scr, wd_scr, inv_scr, sems):
    i = pl.program_id(0)
    valid = bv_ref[i]
    first = jnp.logical_and(
        valid > 0, jnp.logical_or(i == 0, be_ref[i] != be_ref[jnp.maximum(i - 1, 0)]))

    def weight_copies(expert, slot):
        return [pltpu.make_async_copy(w.at[expert], stage.at[slot, j], sems.at[slot, j])
                for j, w in enumerate((wg_hbm, wu_hbm, wd_hbm))]

    @pl.when(i == 0)
    def _():
        for c in weight_copies(be_ref[0], sl_ref[0]):
            c.start()

    @pl.when(first)
    def _():
        slot = sl_ref[i]
        for c in weight_copies(be_ref[i], slot):
            c.wait()
        for j, scr in enumerate((wg_scr, wu_scr, wd_scr)):
            w = stage[slot, j].astype(BF16)
            amax = jnp.max(jnp.max(jnp.abs(w), axis=0, keepdims=True), axis=1, keepdims=True)
            scale = jnp.exp2(jnp.floor(jnp.log2(
                F8_TARGET / jnp.maximum(amax.astype(F32), TINY))))
            scr[...] = (w * scale.astype(BF16)).astype(F8)
            inv_scr[j] = jnp.broadcast_to(1.0 / scale, inv_scr.shape[1:])

        @pl.when(nx_ref[i] >= 0)
        def _():
            for c in weight_copies(nx_ref[i], 1 - slot):
                c.start()

    @pl.when(valid > 0)
    def _():
        lo, hi = _unpack_rows(x_ref)
        live = lax.broadcasted_iota(jnp.int32, (ROW_BLOCK, 1), 0) < valid
        x = jnp.where(live, jnp.concatenate(lo + hi, axis=1), 0.0).astype(BF16).astype(F8)
        inv_x = 1.0 / xs_ref[0:1, 0:1]
        cg = (inv_scr[0, 0:1, 0:1] * inv_x).astype(BF16)
        cl = (inv_scr[1, 0:1, 0:1] * inv_x * HID_SCALE).astype(BF16)
        g = _dot(x, wg_scr[...]).astype(BF16) * cg + bg_ref[...].astype(BF16)
        g = jnp.minimum(g, SWIGLU_LIMIT)
        l = _dot(x, wu_scr[...]).astype(BF16) * cl + (bu_ref[...] * HID_SCALE).astype(BF16)
        l = jnp.clip(l, -SWIGLU_LIMIT * HID_SCALE, SWIGLU_LIMIT * HID_SCALE) + HID_SCALE
        hid = g * l / (1.0 + jnp.exp2(g * (-SWIGLU_ALPHA * LOG2_E)))
        y = _dot(hid.astype(F8), wd_scr[...])
        _pack_rows(y * (inv_scr[2, 0:1, 0:1] * (1.0 / HID_SCALE)) + bd_ref[...], y_ref)

    @pl.when(valid <= 0)
    def _():
        y_ref[...] = jnp.zeros_like(y_ref)


def _expert_call(block_expert, block_valid, next_expert, slot, x_tab, x_scale,
                 w_gate, b_gate, w_up, b_up, w_down, b_down):
    d, f = w_gate.shape[-2:]
    assert d == f
    n_blocks = block_expert.shape[0]
    bspec = lambda width: pl.BlockSpec((None, 1, width), lambda i, be, bv, nx, sl: (be[i], 0, 0))
    rows = pl.BlockSpec((SUBROWS, ROW_BLOCK, LANES), lambda i, be, bv, nx, sl: (0, i, 0))
    hbm = pl.BlockSpec(memory_space=pl.ANY)
    grid_spec = pltpu.PrefetchScalarGridSpec(
        num_scalar_prefetch=4,
        grid=(n_blocks,),
        in_specs=[rows, pl.BlockSpec((SUBLANES, LANES), lambda i, be, bv, nx, sl: (0, 0)),
                  bspec(f), bspec(f), bspec(d), hbm, hbm, hbm],
        out_specs=rows,
        scratch_shapes=[pltpu.VMEM((2, 3, d, f), F32),
                        pltpu.VMEM((d, f), F8), pltpu.VMEM((d, f), F8), pltpu.VMEM((f, d), F8),
                        pltpu.VMEM((3, SUBLANES, LANES), F32), pltpu.SemaphoreType.DMA((2, 3))],
    )
    return pl.pallas_call(
        _expert_kernel,
        grid_spec=grid_spec,
        out_shape=jax.ShapeDtypeStruct(x_tab.shape, jnp.int32),
        compiler_params=pltpu.CompilerParams(
            dimension_semantics=("arbitrary",), vmem_limit_bytes=VMEM_LIMIT),
        name="expert",
    )(block_expert, block_valid, next_expert, slot, x_tab, x_scale, b_gate, b_up, b_down,
      w_gate, w_up, w_down)


def _combine_kernel(h_ref, routet_ref, gf_ref, ys_ref, *rest):
    out_ref = rest[-1]
    nb, tl, d = out_ref.shape
    m = nb * tl
    h = h_ref[...]
    route = jnp.transpose(jnp.concatenate(
        [routet_ref[...], jnp.zeros((LANES - ROUTE_ROWS, m), F32)], axis=0))
    lo_acc = [h[:, s * LANES:(s + 1) * LANES] for s in range(SUBROWS)]
    hi_acc = [h[:, PACKED + s * LANES:PACKED + (s + 1) * LANES] for s in range(SUBROWS)]
    for k in range(TOP_K):
        gate = route[:, TOP_K + k:TOP_K + k + 1]
        lo, hi = _unpack_rows(ys_ref.at[k])
        lo_acc = [a + gate * v for a, v in zip(lo_acc, lo)]
        hi_acc = [a + gate * v for a, v in zip(hi_acc, hi)]
    acc = jnp.concatenate(lo_acc + hi_acc, axis=1)
    out_ref[...] = _rmsnorm(acc, gf_ref[...]).reshape(nb, tl, d)


def _combine_call(h, route_t, g_final, y_slots, nb, seq, part, n_parts, prev):
    t, d = h.shape
    tl = TIME_TILE
    m = nb * tl
    n = t // m // n_parts
    first = part * n
    in_specs = [pl.BlockSpec((m, d), lambda i: (first + i, 0)),
                pl.BlockSpec((ROUTE_ROWS, m), lambda i: (0, first + i)),
                pl.BlockSpec((1, d), lambda i: (0, 0)),
                pl.BlockSpec((TOP_K, SUBROWS, m, LANES), lambda i: (0, 0, i, 0))]
    operands = [h, route_t, g_final, y_slots]
    aliases = {}
    if prev is not None:
        in_specs.append(pl.BlockSpec(memory_space=pl.ANY))
        operands.append(prev)
        aliases = {4: 0}
    return pl.pallas_call(
        _combine_kernel,
        grid=(n,),
        in_specs=in_specs,
        out_specs=pl.BlockSpec((nb, tl, d), lambda i: (0, first + i, 0)),
        out_shape=jax.ShapeDtypeStruct((nb, seq, d), F32),
        input_output_aliases=aliases,
        compiler_params=pltpu.CompilerParams(
            dimension_semantics=("arbitrary",), vmem_limit_bytes=VMEM_LIMIT),
        name="combine",
    )(*operands)


def _s5_operands(lam_re, lam_im, log_dt, b_re, b_im, c_re, c_im, nb):
    gp, ns, pg = SSM_GROUPS, SSM_STATE, SSM_GROUP
    lam = lax.complex(lam_re.astype(F32), lam_im.astype(F32))
    dt = jnp.exp(log_dt.astype(F32))[:, None]
    lam_bar = jnp.exp(lam * dt)
    b_bar = ((lam_bar - 1.0) / lam)[..., None] * lax.complex(b_re.astype(F32), b_im.astype(F32))
    grp = lambda hf, p: slice(16 * hf + 8 * p, 16 * hf + 8 * p + 8)
    eye8 = jnp.eye(8, dtype=F32)

    def b_block(bb):
        return jnp.einsum('gnq,gh->gqhn', bb, eye8).reshape(8 * pg, 8 * ns)

    def c_block(cc):
        return jnp.einsum('gqn,gh->gnhq', cc, eye8).reshape(8 * ns, 8 * pg)

    bd, cd, a_re, a_im = [], [], [], []
    for p in range(2):
        rows = []
        for hf in range(2):
            bb = b_bar[grp(hf, p)]
            rows.append(jnp.concatenate([b_block(bb.real), b_block(bb.imag)], axis=1))
        bd.append(jnp.concatenate(rows, axis=0))
        cols = []
        for hf in range(2):
            cols.append(jnp.concatenate([c_block(c_re[grp(hf, p)].astype(F32)),
                                         -c_block(c_im[grp(hf, p)].astype(F32))], axis=0))
        cd.append(jnp.concatenate(cols, axis=1))
        lam_rows = jnp.stack([lam_bar[grp(j // nb, p)].reshape(8 * ns) for j in range(SUBLANES)])
        a_re.append(lam_rows.real)
        a_im.append(lam_rows.imag)
    bd = jnp.stack(bd).astype(BF16)
    cd = jnp.stack(cd).astype(BF16)
    a_re = jnp.stack(a_re)
    a_im = jnp.stack(a_im)

    return bd, cd, a_re, a_im


def kernel(x, norm_mix_g, w_in, lam_re, lam_im, log_dt, b_re, b_im, c_re, c_im, d_skip, w_glu, b_glu, sgu_ln_g, sgu_ln_b, w_s, b_s, w_branch_a, w_branch_b, w_out, norm_moe_g, w_router, b_router, w_gate, b_gate, w_up, b_up, w_down, b_down, norm_final_g):
    nb, seq, d = x.shape
    assert d == D_MODEL and SUBLANES % nb == 0 and SUBLANES // nb == 2
    assert seq % TIME_TILE == 0 and norm_mix_g.shape[0] == 1
    assert (nb * seq * SUBROWS) % (SC_WORKERS * LANES) == 0
    assert (nb * seq * SUBROWS * TOP_K) % (SC_WORKERS * LANES * 2 * COMBINE_PARTS) == 0
    assert (seq // TIME_TILE) % COMBINE_PARTS == 0
    tl = TIME_TILE
    t = nb * seq
    row = lambda v: v.reshape(1, -1).astype(F32)

    bd, cd, a_re, a_im = _s5_operands(
        lam_re[0], lam_im[0], log_dt[0], b_re[0], b_im[0], c_re[0], c_im[0], nb)
    w_r = jnp.zeros((d, LANES), F32).at[:, :N_EXPERTS].set(w_router[0].astype(F32)).astype(BF16)
    b_r = jnp.full((1, LANES), NEG_BIG, F32).at[0, :N_EXPERTS].set(b_router[0].astype(F32))
    x_bound = math.sqrt(d) * jnp.max(jnp.abs(norm_moe_g[0].astype(F32)))
    x_scale = jnp.exp2(jnp.floor(jnp.log2(F8_TARGET / jnp.maximum(x_bound, TINY))))
    x_scale = jnp.full((SUBLANES, LANES), 1.0, F32) * x_scale
    m = nb * tl
    upper = (jnp.arange(m)[:, None] < jnp.arange(m)[None, :]).astype(BF16)
    h, xn2p, route_t, cnt = _mix_call(
        x, row(norm_mix_g[0]), w_in[0].astype(BF16), row(sgu_ln_g[0]), row(sgu_ln_b[0]),
        w_s[0].astype(F32), b_s[0].T.astype(F32), w_branch_b[0].astype(BF16),
        upper, bd, cd, a_re, a_im, row(d_skip[0]),
        w_glu[0].astype(BF16), row(b_glu[0]), w_branch_a[0].astype(BF16), w_out[0].astype(BF16),
        row(norm_moe_g[0]), x_scale, w_r, b_r)

    idx_t = route_t[0:TOP_K].astype(jnp.int32)
    rank_t = route_t[2 * TOP_K:3 * TOP_K].astype(jnp.int32)
    counts = cnt[:, 0].astype(jnp.int32)
    padded = (counts + ROW_BLOCK - 1) // ROW_BLOCK * ROW_BLOCK
    cum = jnp.cumsum(padded)
    pstart = cum - padded
    experts = jnp.arange(N_EXPERTS, dtype=jnp.int32)
    dest_t = rank_t
    for e in range(N_EXPERTS):
        dest_t = dest_t + jnp.where(idx_t == e, pstart[e], 0)
    n_blocks = (t * TOP_K) // ROW_BLOCK + N_EXPERTS
    n_rows = n_blocks * ROW_BLOCK
    block_row0 = jnp.arange(n_blocks, dtype=jnp.int32) * ROW_BLOCK
    block_expert = jnp.minimum(
        jnp.sum((cum[None, :] <= block_row0[:, None]).astype(jnp.int32), axis=1), N_EXPERTS - 1)
    block_valid = jnp.clip(counts[block_expert] - (block_row0 - pstart[block_expert]), 0, ROW_BLOCK)
    present = counts > 0
    slot_e = (jnp.cumsum(present.astype(jnp.int32)) - 1) % 2
    later = lax.cummin(jnp.where(present, experts, N_EXPERTS)[::-1])[::-1]
    next_e = jnp.concatenate([later[1:], jnp.full((1,), N_EXPERTS, jnp.int32)])
    next_e = jnp.where(next_e == N_EXPERTS, -1, next_e)

    sub = dest_t[:, None, :] + (jnp.arange(SUBROWS, dtype=jnp.int32) * n_rows)[None, :, None]
    idx_dispatch = sub.reshape(TOP_K, SUBROWS * t // LANES, LANES).transpose(1, 0, 2)
    idx_dispatch = idx_dispatch.reshape(SC_WORKERS, -1, LANES)

    x_tab = _sc_dispatch(xn2p.reshape(SUBROWS * t, LANES), idx_dispatch, SUBROWS * n_rows)
    y_tab = _expert_call(
        block_expert, block_valid, next_e[block_expert], slot_e[block_expert],
        x_tab.reshape(SUBROWS, n_rows, LANES), x_scale,
        w_gate[0], b_gate[0][:, None, :], w_up[0], b_up[0][:, None, :],
        w_down[0], b_down[0][:, None, :])
    y_flat = y_tab.reshape(SUBROWS * n_rows, LANES)
    tp = t // COMBINE_PARTS
    idx_parts = sub.reshape(TOP_K, SUBROWS, COMBINE_PARTS, tp).transpose(2, 0, 1, 3)
    out = None
    for q in range(COMBINE_PARTS):
        y_slots = _sc_gather(y_flat, idx_parts[q].reshape(SC_WORKERS, -1, LANES))
        out = _combine_call(h, route_t, row(norm_final_g),
                            y_slots.reshape(TOP_K, SUBROWS, tp, LANES), nb, seq,
                            q, COMBINE_PARTS, out)
    return out
```

```python
import functools
import math

import jax
import jax.numpy as jnp
from jax import lax
from jax.experimental import pallas as pl
from jax.experimental.pallas import tpu as pltpu
from jax.experimental.pallas import tpu_sc as plsc

F32 = jnp.float32
BF16 = jnp.bfloat16
F8 = jnp.float8_e4m3fn
F8_TARGET = 240.0
TINY = 1e-30

NORM_EPS = 1e-5
D_MODEL = 1024
SSM_WIDTH = 512
SSM_GROUP = 16
SSM_GROUPS = 32
SSM_STATE = 64
SGU_WIDTH = 512
SGU_BLOCK = 128
SGU_HEADS = 4
SGU_HEAD_DIM = 128
CHUNK = 64
N_EXPERTS = 32
TOP_K = 4
SWIGLU_ALPHA = 1.702
SWIGLU_LIMIT = 7.0
LOG2_E = 1.4426950408889634
HID_SCALE = 4.0

LANES = 128
SUBLANES = 8
TIME_TILE = 128
SCAN_BLOCK = 64
ROW_BLOCK = 512
VMEM_LIMIT = 56 * 1024 * 1024
NEG_BIG = -1e30
SC_CORES = 2
SC_SUBCORES = 16
SC_WORKERS = SC_CORES * SC_SUBCORES
SUBROWS = 4
COMBINE_PARTS = 4
SLOT_TILE = 4096
ROUTE_ROWS = 16
PACKED = D_MODEL // 2


def _pack_rows(x, out_ref):
    lo = lax.bitcast_convert_type(x[:, :PACKED].astype(BF16).astype(F32), jnp.int32)
    hi = lax.bitcast_convert_type(x[:, PACKED:].astype(BF16).astype(F32), jnp.int32)
    words = lax.shift_right_logical(lo, 16) | (hi & jnp.int32(-65536))
    for s in range(SUBROWS):
        out_ref[s] = words[:, s * LANES:(s + 1) * LANES]


def _unpack_rows(ref):
    lo, hi = [], []
    for s in range(SUBROWS):
        w = ref[s]
        lo.append(lax.bitcast_convert_type(lax.shift_left(w, 16), F32))
        hi.append(lax.bitcast_convert_type(w & jnp.int32(-65536), F32))
    return lo, hi


def _dot(a, b):
    return jnp.dot(a, b, preferred_element_type=F32)


def _sigmoid(x):
    return 1.0 / (1.0 + jnp.exp(-x))


def _gelu(x):
    return 0.5 * x * (1.0 + jnp.tanh(0.7978845608028654 * (x + 0.044715 * (x * x * x))))


def _rmsnorm(x, g):
    return x * lax.rsqrt(jnp.mean(x * x, axis=-1, keepdims=True) + NORM_EPS) * g


def _proj_body(x, g_ref, w_ref, lng_ref, lnb_ref, ws_ref, bs_ref, wb_ref, yb_scr):
    m = x.shape[0]
    xn = _rmsnorm(x, g_ref[...]).astype(BF16)

    xa = _dot(xn, w_ref[:, 0:SSM_WIDTH]).astype(BF16)

    s_b = SSM_WIDTH + 2 * SGU_WIDTH
    s_g = s_b + D_MODEL
    sga = _sigmoid(_dot(xn, w_ref[:, s_b:s_g])).astype(BF16)

    z = _gelu(_dot(xn, w_ref[:, SSM_WIDTH:s_b]))
    u = z[:, :SGU_WIDTH]
    v = z[:, SGU_WIDTH:]
    mu = jnp.mean(v, axis=-1, keepdims=True)
    vc = v - mu
    v = vc * lax.rsqrt(jnp.mean(vc * vc, axis=-1, keepdims=True) + NORM_EPS)
    v = (v * lng_ref[...] + lnb_ref[...]).astype(BF16)

    ri = lax.broadcasted_iota(jnp.int32, (SGU_BLOCK, SGU_BLOCK), 0) // CHUNK
    ci = lax.broadcasted_iota(jnp.int32, (SGU_BLOCK, SGU_BLOCK), 1) // CHUNK
    causal = ri >= ci
    for h in range(SGU_HEADS):
        wm = jnp.where(causal, ws_ref[h], 0.0).astype(BF16)
        bias = bs_ref[:, h:h + 1]
        lo = h * SGU_HEAD_DIM
        for blk in range(m // SGU_BLOCK):
            r0 = blk * SGU_BLOCK
            s = _dot(wm, v[r0:r0 + SGU_BLOCK, lo:lo + SGU_HEAD_DIM]) + bias
            yb_scr[r0:r0 + SGU_BLOCK, lo:lo + SGU_HEAD_DIM] = (
                u[r0:r0 + SGU_BLOCK, lo:lo + SGU_HEAD_DIM] * s).astype(BF16)

    gb = _dot(xn, w_ref[:, s_g:])
    pb = (_sigmoid(gb) * _dot(yb_scr[...], wb_ref[...])).astype(BF16)
    return xa, sga, pb


def _mix_kernel(x_ref, g1_ref, win_ref, lng_ref, lnb_ref, ws_ref, bs_ref, wb_ref,
                upper_ref, bd_ref, cd_ref,
                are_ref, aim_ref, dskip_ref, wglu_ref, bglu_ref, wa_ref, wout_ref,
                g2_ref, xs_ref, wr_ref, br_ref,
                h_ref, xn2_ref, routet_ref, cnt_ref,
                bu0_scr, bu1_scr, x8_scr, y8_scr, yb_scr, state_scr, cnt_scr):
    nb, tl, d = x_ref.shape
    m = nb * tl
    x = x_ref[...].reshape(m, d)
    xa_bf, sga, pb = _proj_body(x, g1_ref, win_ref, lng_ref, lnb_ref, ws_ref, bs_ref, wb_ref,
                                yb_scr)

    @pl.when(pl.program_id(0) == 0)
    def _():
        state_scr[...] = jnp.zeros_like(state_scr)
        cnt_scr[...] = jnp.zeros_like(cnt_scr)
        x8_scr[...] = jnp.zeros_like(x8_scr)

    xa = xa_bf.astype(F32)

    def chunk_lo(p, hf):
        return hf * (SSM_WIDTH // 2) + p * LANES

    scans = (bu0_scr, bu1_scr)
    for p in range(2):
        for hf in range(2):
            lo = chunk_lo(p, hf)
            for b in range(nb):
                x8_scr[2 * p + hf, pl.ds(hf * nb + b, tl, stride=SUBLANES), :] = (
                    xa[b * tl:(b + 1) * tl, lo:lo + LANES])
        x8 = jnp.concatenate([x8_scr[2 * p], x8_scr[2 * p + 1]], axis=1).astype(BF16)
        scans[p][...] = _dot(x8, bd_ref[p])

    a_re = [are_ref[p] for p in range(2)]
    a_im = [aim_ref[p] for p in range(2)]
    s_re = [state_scr[p][:, :SSM_WIDTH] for p in range(2)]
    s_im = [state_scr[p][:, SSM_WIDTH:] for p in range(2)]
    for t0 in range(0, tl, SCAN_BLOCK):
        rows_p = ([], [])
        for t in range(t0, t0 + SCAN_BLOCK):
            r = t * SUBLANES
            for p in range(2):
                bu_scr = scans[p]
                n_re = a_re[p] * s_re[p] - a_im[p] * s_im[p] + bu_scr[r:r + SUBLANES, 0:SSM_WIDTH]
                n_im = (a_re[p] * s_im[p] + a_im[p] * s_re[p]
                        + bu_scr[r:r + SUBLANES, SSM_WIDTH:2 * SSM_WIDTH])
                s_re[p], s_im[p] = n_re, n_im
                rows_p[p].append(jnp.concatenate([n_re, n_im], axis=1))
        r0 = t0 * SUBLANES
        for p in range(2):
            y8 = _dot(jnp.concatenate(rows_p[p], axis=0).astype(BF16), cd_ref[p])
            for hf in range(2):
                y8_scr[2 * p + hf, r0:r0 + SCAN_BLOCK * SUBLANES, :] = (
                    y8[:, hf * LANES:(hf + 1) * LANES])

    y_chunks = {}
    for p in range(2):
        state_scr[p] = jnp.concatenate([s_re[p], s_im[p]], axis=1)
        for hf in range(2):
            y_chunks[(hf, p)] = jnp.concatenate(
                [y8_scr[2 * p + hf, pl.ds(hf * nb + b, tl, stride=SUBLANES), :]
                 for b in range(nb)], axis=0)
    y = jnp.concatenate([y_chunks[(hf, p)] for hf in range(2) for p in range(2)], axis=1)
    y = y + dskip_ref[...] * xa
    z = _gelu(y)
    ya = z * _sigmoid(_dot(z.astype(BF16), wglu_ref[...]) + bglu_ref[...])
    pa = sga.astype(F32) * _dot(ya.astype(BF16), wa_ref[...])
    merged = pa + pb.astype(F32)
    h = x + _dot(merged.astype(BF16), wout_ref[...])
    h_ref[...] = h
    xn2 = _rmsnorm(h, g2_ref[...])
    _pack_rows(xn2 * xs_ref[0:1, 0:1], xn2_ref)

    logits = _dot(xn2.astype(BF16), wr_ref[...]) + br_ref[...]
    work = jnp.transpose(logits)[:N_EXPERTS]
    expert = lax.broadcasted_iota(jnp.int32, (N_EXPERTS, m), 0).astype(F32)
    vals, idxs = [], []
    member = jnp.zeros((N_EXPERTS, m), F32)
    for _ in range(TOP_K):
        mx = jnp.max(work, axis=0, keepdims=True)
        ix = jnp.min(jnp.where(work == mx, expert, float(N_EXPERTS)), axis=0, keepdims=True)
        hit = expert == ix
        member = jnp.where(hit, 1.0, member)
        work = jnp.where(hit, NEG_BIG, work)
        vals.append(mx)
        idxs.append(ix)
    exps = [jnp.exp(v - vals[0]) for v in vals]
    denom = exps[0] + exps[1] + exps[2] + exps[3]
    gates = [e / denom for e in exps]

    before = _dot(member.astype(BF16), upper_ref[...]) + cnt_scr[:, 0:1]
    new_cnt = cnt_scr[:, 0:1] + jnp.sum(member, axis=1, keepdims=True)
    cnt_scr[...] = jnp.broadcast_to(new_cnt, cnt_scr.shape)
    cnt_ref[...] = jnp.broadcast_to(new_cnt, cnt_ref.shape)
    ranks = [jnp.sum(jnp.where(expert == ix, before, 0.0), axis=0, keepdims=True) for ix in idxs]
    pad = jnp.zeros((ROUTE_ROWS - 3 * TOP_K, m), F32)
    routet_ref[...] = jnp.concatenate(idxs + gates + ranks + [pad], axis=0)


def _mix_call(x, *params):
    nb, seq, d = x.shape
    tl = TIME_TILE
    m = nb * tl
    n_tiles = seq // tl
    const = lambda shape: pl.BlockSpec(shape, lambda i: (0,) * len(shape))
    resident = lambda shape: pl.BlockSpec(shape, lambda i: (0,) * len(shape),
                                          pipeline_mode=pl.Buffered(1))
    tile = lambda width: pl.BlockSpec((nb, tl, width), lambda i: (0, i, 0))
    rows = lambda width: pl.BlockSpec((m, width), lambda i: (i, 0))
    operands = (x,) + params
    in_specs = [tile(d)] + [resident(o.shape) for o in params]
    return pl.pallas_call(
        _mix_kernel,
        grid=(n_tiles,),
        in_specs=in_specs,
        out_specs=[rows(d), pl.BlockSpec((SUBROWS, m, LANES), lambda i: (0, i, 0)),
                   pl.BlockSpec((ROUTE_ROWS, m), lambda i: (0, i)), const((N_EXPERTS, LANES))],
        out_shape=[jax.ShapeDtypeStruct((n_tiles * m, d), F32),
                   jax.ShapeDtypeStruct((SUBROWS, n_tiles * m, LANES), jnp.int32),
                   jax.ShapeDtypeStruct((ROUTE_ROWS, n_tiles * m), F32),
                   jax.ShapeDtypeStruct((N_EXPERTS, LANES), F32)],
        scratch_shapes=[pltpu.VMEM((SUBLANES * tl, 2 * SSM_WIDTH), F32),
                        pltpu.VMEM((SUBLANES * tl, 2 * SSM_WIDTH), F32),
                        pltpu.VMEM((4, SUBLANES * tl, LANES), F32),
                        pltpu.VMEM((4, SUBLANES * tl, LANES), F32),
                        pltpu.VMEM((m, SGU_WIDTH), BF16),
                        pltpu.VMEM((2, SUBLANES, 2 * SSM_WIDTH), F32),
                        pltpu.VMEM((N_EXPERTS, LANES), F32)],
        compiler_params=pltpu.CompilerParams(
            dimension_semantics=("arbitrary",), vmem_limit_bytes=VMEM_LIMIT),
        name="mix",
    )(*operands)


def _sc_mesh():
    return plsc.VectorSubcoreMesh(core_axis_name="c", subcore_axis_name="s")


def _worker_id():
    return lax.axis_index("s") * SC_CORES + lax.axis_index("c")


def _dispatch_body(src_hbm, idx_hbm, out_hbm, idx_v, buf, sem):
    n_chunks = idx_v.shape[0] // TOP_K
    wid = _worker_id()
    base = wid * (n_chunks * LANES)
    pltpu.sync_copy(idx_hbm.at[wid], idx_v)

    @pl.loop(0, n_chunks)
    def _(j):
        pltpu.sync_copy(src_hbm.at[pl.ds(base + j * LANES, LANES)], buf)
        copies = [pltpu.async_copy(buf, out_hbm.at[idx_v.at[j * TOP_K + k]], sem)
                  for k in range(TOP_K)]
        for c in copies:
            c.wait()


def _sc_dispatch(src, idx, n_out):
    n_chunks = src.shape[0] // (SC_WORKERS * LANES)
    return pl.kernel(
        _dispatch_body,
        out_type=jax.ShapeDtypeStruct((n_out, LANES), jnp.int32),
        mesh=_sc_mesh(),
        scratch_types=[pltpu.VMEM((n_chunks * TOP_K, LANES), jnp.int32),
                       pltpu.VMEM((LANES, LANES), jnp.int32),
                       pltpu.SemaphoreType.DMA],
        name="sc_dispatch",
    )(src, idx)


def _gather_body(tab_hbm, idx_hbm, out_hbm, idx_v, buf0, buf1, sem0, sem1):
    n_chunks = idx_v.shape[0]
    wid = _worker_id()
    base = wid * (n_chunks * LANES)
    pltpu.sync_copy(idx_hbm.at[wid], idx_v)

    @pl.loop(0, n_chunks, step=2)
    def _(j):
        c0 = pltpu.async_copy(tab_hbm.at[idx_v.at[j]], buf0, sem0)
        c1 = pltpu.async_copy(tab_hbm.at[idx_v.at[j + 1]], buf1, sem1)
        c0.wait()
        pltpu.sync_copy(buf0, out_hbm.at[pl.ds(base + j * LANES, LANES)])
        c1.wait()
        pltpu.sync_copy(buf1, out_hbm.at[pl.ds(base + (j + 1) * LANES, LANES)])


def _sc_gather(tab, idx):
    n_chunks = idx.shape[1]
    return pl.kernel(
        _gather_body,
        out_type=jax.ShapeDtypeStruct((SC_WORKERS * n_chunks * LANES, LANES), jnp.int32),
        mesh=_sc_mesh(),
        scratch_types=[pltpu.VMEM((n_chunks, LANES), jnp.int32),
                       pltpu.VMEM((LANES, LANES), jnp.int32),
                       pltpu.VMEM((LANES, LANES), jnp.int32),
                       pltpu.SemaphoreType.DMA, pltpu.SemaphoreType.DMA],
        name="sc_gather",
    )(tab, idx)


def _slots_kernel(pstart_ref, rt_ref, sub_ref, *, n_rows):
    idx = rt_ref[0:TOP_K, :]
    dest = rt_ref[2 * TOP_K:3 * TOP_K, :]
    for e in range(N_EXPERTS):
        dest = dest + jnp.where(idx == float(e), pstart_ref[e].astype(F32), 0.0)
    dest = dest.astype(jnp.int32)
    for k in range(TOP_K):
        for s in range(SUBROWS):
            sub_ref[k, s:s + 1, :] = dest[k:k + 1, :] + s * n_rows


def _slots_call(pstart, route_t, n_rows):
    t = route_t.shape[1]
    tb = min(t, SLOT_TILE)
    grid_spec = pltpu.PrefetchScalarGridSpec(
        num_scalar_prefetch=1,
        grid=(t // tb,),
        in_specs=[pl.BlockSpec((ROUTE_ROWS, tb), lambda i, ps: (0, i))],
        out_specs=pl.BlockSpec((TOP_K, SUBROWS, tb), lambda i, ps: (0, 0, i)),
    )
    return pl.pallas_call(
        functools.partial(_slots_kernel, n_rows=n_rows),
        grid_spec=grid_spec,
        out_shape=jax.ShapeDtypeStruct((TOP_K, SUBROWS, t), jnp.int32),
        compiler_params=pltpu.CompilerParams(dimension_semantics=("arbitrary",)),
        name="slots",
    )(pstart, route_t)


def _expert_kernel(be_ref, bv_ref, nx_ref, sl_ref, x_ref, xs_ref, bg_ref, bu_ref, bd_ref,
                   wg_hbm, wu_hbm, wd_hbm, y_ref, stage, wg_scr, wu_scr, wd_scr, inv_scr, sems):
    i = pl.program_id(0)
    valid = bv_ref[i]
    first = jnp.logical_and(
        valid > 0, jnp.logical_or(i == 0, be_ref[i] != be_ref[jnp.maximum(i - 1, 0)]))

    def weight_copies(expert, slot):
        return [pltpu.make_async_copy(w.at[expert], stage.at[slot, j], sems.at[slot, j])
                for j, w in enumerate((wg_hbm, wu_hbm, wd_hbm))]

    @pl.when(i == 0)
    def _():
        for c in weight_copies(be_ref[0], sl_ref[0]):
            c.start()

    @pl.when(first)
    def _():
        slot = sl_ref[i]
        for c in weight_copies(be_ref[i], slot):
            c.wait()
        for j, scr in enumerate((wg_scr, wu_scr, wd_scr)):
            w = stage[slot, j].astype(BF16)
            amax = jnp.max(jnp.max(jnp.abs(w), axis=0, keepdims=True), axis=1, keepdims=True)
            scale = jnp.exp2(jnp.floor(jnp.log2(
                F8_TARGET / jnp.maximum(amax.astype(F32), TINY))))
            scr[...] = (w * scale.astype(BF16)).astype(F8)
            inv_scr[j] = jnp.broadcast_to(1.0 / scale, inv_scr.shape[1:])

        @pl.when(nx_ref[i] >= 0)
        def _():
            for c in weight_copies(nx_ref[i], 1 - slot):
                c.start()

    @pl.when(valid > 0)
    def _():
        lo, hi = _unpack_rows(x_ref)
        live = lax.broadcasted_iota(jnp.int32, (ROW_BLOCK, 1), 0) < valid
        x = jnp.where(live, jnp.concatenate(lo + hi, axis=1), 0.0).astype(BF16).astype(F8)
        inv_x = 1.0 / xs_ref[0:1, 0:1]
        cg = (inv_scr[0, 0:1, 0:1] * inv_x).astype(BF16)
        cl = (inv_scr[1, 0:1, 0:1] * inv_x * HID_SCALE).astype(BF16)
        g = _dot(x, wg_scr[...]).astype(BF16) * cg + bg_ref[...].astype(BF16)
        g = jnp.minimum(g, SWIGLU_LIMIT)
        l = _dot(x, wu_scr[...]).astype(BF16) * cl + (bu_ref[...] * HID_SCALE).astype(BF16)
        l = jnp.clip(l, -SWIGLU_LIMIT * HID_SCALE, SWIGLU_LIMIT * HID_SCALE) + HID_SCALE
        hid = g * l / (1.0 + jnp.exp2(g * (-SWIGLU_ALPHA * LOG2_E)))
        y = _dot(hid.astype(F8), wd_scr[...])
        _pack_rows(y * (inv_scr[2, 0:1, 0:1] * (1.0 / HID_SCALE)) + bd_ref[...], y_ref)

    @pl.when(valid <= 0)
    def _():
        y_ref[...] = jnp.zeros_like(y_ref)


def _expert_call(block_expert, block_valid, next_expert, slot, x_tab, x_scale,
                 w_gate, b_gate, w_up, b_up, w_down, b_down):
    d, f = w_gate.shape[-2:]
    assert d == f
    n_blocks = block_expert.shape[0]
    bspec = lambda width: pl.BlockSpec((None, 1, width), lambda i, be, bv, nx, sl: (be[i], 0, 0))
    rows = pl.BlockSpec((SUBROWS, ROW_BLOCK, LANES), lambda i, be, bv, nx, sl: (0, i, 0))
    hbm = pl.BlockSpec(memory_space=pl.ANY)
    grid_spec = pltpu.PrefetchScalarGridSpec(
        num_scalar_prefetch=4,
        grid=(n_blocks,),
        in_specs=[rows, pl.BlockSpec((SUBLANES, LANES), lambda i, be, bv, nx, sl: (0, 0)),
                  bspec(f), bspec(f), bspec(d), hbm, hbm, hbm],
        out_specs=rows,
        scratch_shapes=[pltpu.VMEM((2, 3, d, f), F32),
                        pltpu.VMEM((d, f), F8), pltpu.VMEM((d, f), F8), pltpu.VMEM((f, d), F8),
                        pltpu.VMEM((3, SUBLANES, LANES), F32), pltpu.SemaphoreType.DMA((2, 3))],
    )
    return pl.pallas_call(
        _expert_kernel,
        grid_spec=grid_spec,
        out_shape=jax.ShapeDtypeStruct(x_tab.shape, jnp.int32),
        compiler_params=pltpu.CompilerParams(
            dimension_semantics=("arbitrary",), vmem_limit_bytes=VMEM_LIMIT),
        name="expert",
    )(block_expert, block_valid, next_expert, slot, x_tab, x_scale, b_gate, b_up, b_down,
      w_gate, w_up, w_down)


def _combine_kernel(h_ref, routet_ref, gf_ref, ys_ref, *rest):
    out_ref = rest[-1]
    nb, tl, d = out_ref.shape
    m = nb * tl
    h = h_ref[...]
    route = jnp.transpose(jnp.concatenate(
        [routet_ref[...], jnp.zeros((LANES - ROUTE_ROWS, m), F32)], axis=0))
    lo_acc = [h[:, s * LANES:(s + 1) * LANES] for s in range(SUBROWS)]
    hi_acc = [h[:, PACKED + s * LANES:PACKED + (s + 1) * LANES] for s in range(SUBROWS)]
    for k in range(TOP_K):
        gate = route[:, TOP_K + k:TOP_K + k + 1]
        lo, hi = _unpack_rows(ys_ref.at[k])
        lo_acc = [a + gate * v for a, v in zip(lo_acc, lo)]
        hi_acc = [a + gate * v for a, v in zip(hi_acc, hi)]
    acc = jnp.concatenate(lo_acc + hi_acc, axis=1)
    out_ref[...] = _rmsnorm(acc, gf_ref[...]).reshape(nb, tl, d)


def _combine_call(h, route_t, g_final, y_slots, nb, seq, part, n_parts, prev):
    t, d = h.shape
    tl = TIME_TILE
    m = nb * tl
    n = t // m // n_parts
    first = part * n
    in_specs = [pl.BlockSpec((m, d), lambda i: (first + i, 0)),
                pl.BlockSpec((ROUTE_ROWS, m), lambda i: (0, first + i)),
                pl.BlockSpec((1, d), lambda i: (0, 0)),
                pl.BlockSpec((TOP_K, SUBROWS, m, LANES), lambda i: (0, 0, i, 0))]
    operands = [h, route_t, g_final, y_slots]
    aliases = {}
    if prev is not None:
        in_specs.append(pl.BlockSpec(memory_space=pl.ANY))
        operands.append(prev)
        aliases = {4: 0}
    return pl.pallas_call(
        _combine_kernel,
        grid=(n,),
        in_specs=in_specs,
        out_specs=pl.BlockSpec((nb, tl, d), lambda i: (0, first + i, 0)),
        out_shape=jax.ShapeDtypeStruct((nb, seq, d), F32),
        input_output_aliases=aliases,
        compiler_params=pltpu.CompilerParams(
            dimension_semantics=("arbitrary",), vmem_limit_bytes=VMEM_LIMIT),
        name="combine",
    )(*operands)


def _s5_operands(lam_re, lam_im, log_dt, b_re, b_im, c_re, c_im, nb):
    ns, pg = SSM_STATE, SSM_GROUP
    lam = lax.complex(lam_re.astype(F32), lam_im.astype(F32))
    dt = jnp.exp(log_dt.astype(F32))[:, None]
    lam_bar = jnp.exp(lam * dt)
    b_bar = ((lam_bar - 1.0) / lam)[..., None] * lax.complex(b_re.astype(F32), b_im.astype(F32))
    eye8 = jnp.eye(8, dtype=F32)
    split = lambda a: a.reshape((2, 2, 8) + a.shape[1:])

    def b_blocks(bb):
        return jnp.einsum('hpgnq,gk->phgqkn', split(bb), eye8).reshape(2, 2 * 8 * pg, 8 * ns)

    def c_blocks(cc):
        return jnp.einsum('hpgqn,gk->pgnhkq', split(cc), eye8).reshape(2, 8 * ns, 2 * 8 * pg)

    bd = jnp.concatenate([b_blocks(b_bar.real), b_blocks(b_bar.imag)], axis=2).astype(BF16)
    cd = jnp.concatenate([c_blocks(c_re.astype(F32)), -c_blocks(c_im.astype(F32))],
                         axis=1).astype(BF16)
    lam_rows = split(lam_bar).transpose(1, 0, 2, 3).reshape(2, 2, 8 * ns)
    lam_rows = jnp.repeat(lam_rows, nb, axis=1)
    return bd, cd, lam_rows.real, lam_rows.imag


def kernel(x, norm_mix_g, w_in, lam_re, lam_im, log_dt, b_re, b_im, c_re, c_im, d_skip, w_glu, b_glu, sgu_ln_g, sgu_ln_b, w_s, b_s, w_branch_a, w_branch_b, w_out, norm_moe_g, w_router, b_router, w_gate, b_gate, w_up, b_up, w_down, b_down, norm_final_g):
    nb, seq, d = x.shape
    assert d == D_MODEL and SUBLANES % nb == 0 and SUBLANES // nb == 2
    assert seq % TIME_TILE == 0 and norm_mix_g.shape[0] == 1
    assert (nb * seq * SUBROWS) % (SC_WORKERS * LANES) == 0
    assert (nb * seq * SUBROWS * TOP_K) % (SC_WORKERS * LANES * 2 * COMBINE_PARTS) == 0
    assert (seq // TIME_TILE) % COMBINE_PARTS == 0
    tl = TIME_TILE
    t = nb * seq
    row = lambda v: v.reshape(1, -1).astype(F32)

    bd, cd, a_re, a_im = _s5_operands(
        lam_re[0], lam_im[0], log_dt[0], b_re[0], b_im[0], c_re[0], c_im[0], nb)
    w_r = jnp.zeros((d, LANES), F32).at[:, :N_EXPERTS].set(w_router[0].astype(F32)).astype(BF16)
    b_r = jnp.full((1, LANES), NEG_BIG, F32).at[0, :N_EXPERTS].set(b_router[0].astype(F32))
    x_bound = math.sqrt(d) * jnp.max(jnp.abs(norm_moe_g[0].astype(F32)))
    x_scale = jnp.exp2(jnp.floor(jnp.log2(F8_TARGET / jnp.maximum(x_bound, TINY))))
    x_scale = jnp.full((SUBLANES, LANES), 1.0, F32) * x_scale
    m = nb * tl
    upper = (jnp.arange(m)[:, None] < jnp.arange(m)[None, :]).astype(BF16)
    h, xn2p, route_t, cnt = _mix_call(
        x, row(norm_mix_g[0]), w_in[0].astype(BF16), row(sgu_ln_g[0]), row(sgu_ln_b[0]),
        w_s[0].astype(F32), b_s[0].T.astype(F32), w_branch_b[0].astype(BF16),
        upper, bd, cd, a_re, a_im, row(d_skip[0]),
        w_glu[0].astype(BF16), row(b_glu[0]), w_branch_a[0].astype(BF16), w_out[0].astype(BF16),
        row(norm_moe_g[0]), x_scale, w_r, b_r)

    counts = cnt[:, 0].astype(jnp.int32)
    padded = (counts + ROW_BLOCK - 1) // ROW_BLOCK * ROW_BLOCK
    cum = jnp.cumsum(padded)
    pstart = cum - padded
    experts = jnp.arange(N_EXPERTS, dtype=jnp.int32)
    n_blocks = (t * TOP_K) // ROW_BLOCK + N_EXPERTS
    n_rows = n_blocks * ROW_BLOCK
    block_row0 = jnp.arange(n_blocks, dtype=jnp.int32) * ROW_BLOCK
    block_expert = jnp.minimum(
        jnp.sum((cum[None, :] <= block_row0[:, None]).astype(jnp.int32), axis=1), N_EXPERTS - 1)
    block_valid = jnp.clip(counts[block_expert] - (block_row0 - pstart[block_expert]), 0, ROW_BLOCK)
    present = counts > 0
    slot_e = (jnp.cumsum(present.astype(jnp.int32)) - 1) % 2
    later = lax.cummin(jnp.where(present, experts, N_EXPERTS)[::-1])[::-1]
    next_e = jnp.concatenate([later[1:], jnp.full((1,), N_EXPERTS, jnp.int32)])
    next_e = jnp.where(next_e == N_EXPERTS, -1, next_e)

    sub = _slots_call(pstart, route_t, n_rows)
    idx_dispatch = sub.reshape(TOP_K, SUBROWS * t // LANES, LANES).transpose(1, 0, 2)
    idx_dispatch = idx_dispatch.reshape(SC_WORKERS, -1, LANES)

    x_tab = _sc_dispatch(xn2p.reshape(SUBROWS * t, LANES), idx_dispatch, SUBROWS * n_rows)
    y_tab = _expert_call(
        block_expert, block_valid, next_e[block_expert], slot_e[block_expert],
        x_tab.reshape(SUBROWS, n_rows, LANES), x_scale,
        w_gate[0], b_gate[0][:, None, :], w_up[0], b_up[0][:, None, :],
        w_down[0], b_down[0][:, None, :])
    y_flat = y_tab.reshape(SUBROWS * n_rows, LANES)
    tp = t // COMBINE_PARTS
    idx_parts = sub.reshape(TOP_K, SUBROWS, COMBINE_PARTS, tp).transpose(2, 0, 1, 3)
    out = None
    for q in range(COMBINE_PARTS):
        y_slots = _sc_gather(y_flat, idx_parts[q].reshape(SC_WORKERS, -1, LANES))
        out = _combine_call(h, route_t, row(norm_final_g),
                            y_slots.reshape(TOP_K, SUBROWS, tp, LANES), nb, seq,
                            q, COMBINE_PARTS, out)
    return out
```

```python
import functools
import math

import jax
import jax.numpy as jnp
from jax import lax
from jax.experimental import pallas as pl
from jax.experimental.pallas import tpu as pltpu
from jax.experimental.pallas import tpu_sc as plsc

F32 = jnp.float32
BF16 = jnp.bfloat16
F8 = jnp.float8_e4m3fn
F8_TARGET = 240.0
TINY = 1e-30

NORM_EPS = 1e-5
D_MODEL = 1024
SSM_WIDTH = 512
SSM_GROUP = 16
SSM_GROUPS = 32
SSM_STATE = 64
SGU_WIDTH = 512
SGU_BLOCK = 128
SGU_HEADS = 4
SGU_HEAD_DIM = 128
CHUNK = 64
N_EXPERTS = 32
TOP_K = 4
SWIGLU_ALPHA = 1.702
SWIGLU_LIMIT = 7.0
LOG2_E = 1.4426950408889634
HID_SCALE = 4.0

LANES = 128
SUBLANES = 8
TIME_TILE = 128
SCAN_BLOCK = 64
ROW_BLOCK = 512
VMEM_LIMIT = 56 * 1024 * 1024
NEG_BIG = -1e30
SC_CORES = 2
SC_SUBCORES = 16
SC_WORKERS = SC_CORES * SC_SUBCORES
SUBROWS = 4
COMBINE_PARTS = 4
SLOT_TILE = 4096
ROUTE_ROWS = 16
PACKED = D_MODEL // 2


def _pack_rows(x, out_ref):
    lo = lax.bitcast_convert_type(x[:, :PACKED].astype(BF16).astype(F32), jnp.int32)
    hi = lax.bitcast_convert_type(x[:, PACKED:].astype(BF16).astype(F32), jnp.int32)
    words = lax.shift_right_logical(lo, 16) | (hi & jnp.int32(-65536))
    for s in range(SUBROWS):
        out_ref[s] = words[:, s * LANES:(s + 1) * LANES]


def _unpack_rows(ref):
    lo, hi = [], []
    for s in range(SUBROWS):
        w = ref[s]
        lo.append(lax.bitcast_convert_type(lax.shift_left(w, 16), F32))
        hi.append(lax.bitcast_convert_type(w & jnp.int32(-65536), F32))
    return lo, hi


def _dot(a, b):
    return jnp.dot(a, b, preferred_element_type=F32)


def _sigmoid(x):
    return 1.0 / (1.0 + jnp.exp(-x))


def _gelu(x):
    return 0.5 * x * (1.0 + jnp.tanh(0.7978845608028654 * (x + 0.044715 * (x * x * x))))


def _rmsnorm(x, g):
    return x * lax.rsqrt(jnp.mean(x * x, axis=-1, keepdims=True) + NORM_EPS) * g


def _proj_xa(x, g_ref, w_ref):
    xn = _rmsnorm(x, g_ref[...]).astype(BF16)
    return xn, _dot(xn, w_ref[:, 0:SSM_WIDTH]).astype(BF16)


def _proj_gate(xn, w_ref, lo):
    return _sigmoid(_dot(xn, w_ref[:, lo:lo + D_MODEL]))


def _sgu(z, lng_ref, lnb_ref, ws_ref, bs_ref, yb_scr):
    m = z.shape[0]
    u = z[:, :SGU_WIDTH]
    v = z[:, SGU_WIDTH:]
    mu = jnp.mean(v, axis=-1, keepdims=True)
    vc = v - mu
    v = vc * lax.rsqrt(jnp.mean(vc * vc, axis=-1, keepdims=True) + NORM_EPS)
    v = (v * lng_ref[...] + lnb_ref[...]).astype(BF16)

    ri = lax.broadcasted_iota(jnp.int32, (SGU_BLOCK, SGU_BLOCK), 0) // CHUNK
    ci = lax.broadcasted_iota(jnp.int32, (SGU_BLOCK, SGU_BLOCK), 1) // CHUNK
    causal = ri >= ci
    for h in range(SGU_HEADS):
        wm = jnp.where(causal, ws_ref[h], 0.0).astype(BF16)
        bias = bs_ref[:, h:h + 1]
        lo = h * SGU_HEAD_DIM
        for blk in range(m // SGU_BLOCK):
            r0 = blk * SGU_BLOCK
            s = _dot(wm, v[r0:r0 + SGU_BLOCK, lo:lo + SGU_HEAD_DIM]) + bias
            yb_scr[r0:r0 + SGU_BLOCK, lo:lo + SGU_HEAD_DIM] = (
                u[r0:r0 + SGU_BLOCK, lo:lo + SGU_HEAD_DIM] * s).astype(BF16)


def _mix_kernel(x_ref, g1_ref, win_ref, lng_ref, lnb_ref, ws_ref, bs_ref, wb_ref,
                upper_ref, bd_ref, cd_ref,
                are_ref, aim_ref, dskip_ref, wglu_ref, bglu_ref, wa_ref, wout_ref,
                g2_ref, xs_ref, wr_ref, br_ref,
                h_ref, xn2_ref, routet_ref, cnt_ref,
                bu0_scr, bu1_scr, x8_scr, y8_scr, yb_scr, state_scr, cnt_scr):
    nb, tl, d = x_ref.shape
    m = nb * tl
    x = x_ref[...].reshape(m, d)
    xn, xa_bf = _proj_xa(x, g1_ref, win_ref)

    @pl.when(pl.program_id(0) == 0)
    def _():
        state_scr[...] = jnp.zeros_like(state_scr)
        cnt_scr[...] = jnp.zeros_like(cnt_scr)
        x8_scr[...] = jnp.zeros_like(x8_scr)

    xa = xa_bf.astype(F32)

    def chunk_lo(p, hf):
        return hf * (SSM_WIDTH // 2) + p * LANES

    scans = (bu0_scr, bu1_scr)
    for p in range(2):
        for hf in range(2):
            lo = chunk_lo(p, hf)
            for b in range(nb):
                x8_scr[2 * p + hf, pl.ds(hf * nb + b, tl, stride=SUBLANES), :] = (
                    xa[b * tl:(b + 1) * tl, lo:lo + LANES])
        x8 = jnp.concatenate([x8_scr[2 * p], x8_scr[2 * p + 1]], axis=1).astype(BF16)
        scans[p][...] = _dot(x8, bd_ref[p])

    a_re = [are_ref[p] for p in range(2)]
    a_im = [aim_ref[p] for p in range(2)]
    s_re = [state_scr[p][:, :SSM_WIDTH] for p in range(2)]
    s_im = [state_scr[p][:, SSM_WIDTH:] for p in range(2)]
    s_b = SSM_WIDTH + 2 * SGU_WIDTH
    z = _gelu(_dot(xn, win_ref[:, SSM_WIDTH:s_b]))
    gates = []
    for t0 in range(0, tl, SCAN_BLOCK):
        rows_p = ([], [])
        for t in range(t0, t0 + SCAN_BLOCK):
            r = t * SUBLANES
            for p in range(2):
                bu_scr = scans[p]
                n_re = a_re[p] * s_re[p] - a_im[p] * s_im[p] + bu_scr[r:r + SUBLANES, 0:SSM_WIDTH]
                n_im = (a_re[p] * s_im[p] + a_im[p] * s_re[p]
                        + bu_scr[r:r + SUBLANES, SSM_WIDTH:2 * SSM_WIDTH])
                s_re[p], s_im[p] = n_re, n_im
                rows_p[p].append(jnp.concatenate([n_re, n_im], axis=1))
        r0 = t0 * SUBLANES
        for p in range(2):
            y8 = _dot(jnp.concatenate(rows_p[p], axis=0).astype(BF16), cd_ref[p])
            for hf in range(2):
                y8_scr[2 * p + hf, r0:r0 + SCAN_BLOCK * SUBLANES, :] = (
                    y8[:, hf * LANES:(hf + 1) * LANES])
        gates.append(_proj_gate(xn, win_ref, s_b + len(gates) * D_MODEL))

    sga = gates[0]

    y_chunks = {}
    for p in range(2):
        state_scr[p] = jnp.concatenate([s_re[p], s_im[p]], axis=1)
        for hf in range(2):
            y_chunks[(hf, p)] = jnp.concatenate(
                [y8_scr[2 * p + hf, pl.ds(hf * nb + b, tl, stride=SUBLANES), :]
                 for b in range(nb)], axis=0)
    y = jnp.concatenate([y_chunks[(hf, p)] for hf in range(2) for p in range(2)], axis=1)
    y = y + dskip_ref[...] * xa
    zs = _gelu(y)
    ya = zs * _sigmoid(_dot(zs.astype(BF16), wglu_ref[...]) + bglu_ref[...])
    _sgu(z, lng_ref, lnb_ref, ws_ref, bs_ref, yb_scr)
    pb = (gates[1] * _dot(yb_scr[...], wb_ref[...])).astype(BF16)
    pa = sga * _dot(ya.astype(BF16), wa_ref[...])
    merged = pa + pb.astype(F32)
    h = x + _dot(merged.astype(BF16), wout_ref[...])
    h_ref[...] = h
    xn2 = _rmsnorm(h, g2_ref[...])
    _pack_rows(xn2 * xs_ref[0:1, 0:1], xn2_ref)

    logits = _dot(xn2.astype(BF16), wr_ref[...]) + br_ref[...]
    work = jnp.transpose(logits)[:N_EXPERTS]
    expert = lax.broadcasted_iota(jnp.int32, (N_EXPERTS, m), 0).astype(F32)
    vals, idxs = [], []
    member = jnp.zeros((N_EXPERTS, m), F32)
    for _ in range(TOP_K):
        mx = jnp.max(work, axis=0, keepdims=True)
        ix = jnp.min(jnp.where(work == mx, expert, float(N_EXPERTS)), axis=0, keepdims=True)
        hit = expert == ix
        member = jnp.where(hit, 1.0, member)
        work = jnp.where(hit, NEG_BIG, work)
        vals.append(mx)
        idxs.append(ix)
    exps = [jnp.exp(v - vals[0]) for v in vals]
    denom = exps[0] + exps[1] + exps[2] + exps[3]
    gates = [e / denom for e in exps]

    before = _dot(member.astype(BF16), upper_ref[...]) + cnt_scr[:, 0:1]
    new_cnt = cnt_scr[:, 0:1] + jnp.sum(member, axis=1, keepdims=True)
    cnt_scr[...] = jnp.broadcast_to(new_cnt, cnt_scr.shape)
    cnt_ref[...] = jnp.broadcast_to(new_cnt, cnt_ref.shape)
    ranks = [jnp.sum(jnp.where(expert == ix, before, 0.0), axis=0, keepdims=True) for ix in idxs]
    pad = jnp.zeros((ROUTE_ROWS - 3 * TOP_K, m), F32)
    routet_ref[...] = jnp.concatenate(idxs + gates + ranks + [pad], axis=0)


def _mix_call(x, *params):
    nb, seq, d = x.shape
    tl = TIME_TILE
    m = nb * tl
    n_tiles = seq // tl
    const = lambda shape: pl.BlockSpec(shape, lambda i: (0,) * len(shape))
    resident = lambda shape: pl.BlockSpec(shape, lambda i: (0,) * len(shape),
                                          pipeline_mode=pl.Buffered(1))
    tile = lambda width: pl.BlockSpec((nb, tl, width), lambda i: (0, i, 0))
    rows = lambda width: pl.BlockSpec((m, width), lambda i: (i, 0))
    operands = (x,) + params
    in_specs = [tile(d)] + [resident(o.shape) for o in params]
    return pl.pallas_call(
        _mix_kernel,
        grid=(n_tiles,),
        in_specs=in_specs,
        out_specs=[rows(d), pl.BlockSpec((SUBROWS, m, LANES), lambda i: (0, i, 0)),
                   pl.BlockSpec((ROUTE_ROWS, m), lambda i: (0, i)), const((N_EXPERTS, LANES))],
        out_shape=[jax.ShapeDtypeStruct((n_tiles * m, d), F32),
                   jax.ShapeDtypeStruct((SUBROWS, n_tiles * m, LANES), jnp.int32),
                   jax.ShapeDtypeStruct((ROUTE_ROWS, n_tiles * m), F32),
                   jax.ShapeDtypeStruct((N_EXPERTS, LANES), F32)],
        scratch_shapes=[pltpu.VMEM((SUBLANES * tl, 2 * SSM_WIDTH), F32),
                        pltpu.VMEM((SUBLANES * tl, 2 * SSM_WIDTH), F32),
                        pltpu.VMEM((4, SUBLANES * tl, LANES), F32),
                        pltpu.VMEM((4, SUBLANES * tl, LANES), F32),
                        pltpu.VMEM((m, SGU_WIDTH), BF16),
                        pltpu.VMEM((2, SUBLANES, 2 * SSM_WIDTH), F32),
                        pltpu.VMEM((N_EXPERTS, LANES), F32)],
        compiler_params=pltpu.CompilerParams(
            dimension_semantics=("arbitrary",), vmem_limit_bytes=VMEM_LIMIT),
        name="mix",
    )(*operands)


def _sc_mesh():
    return plsc.VectorSubcoreMesh(core_axis_name="c", subcore_axis_name="s")


def _worker_id():
    return lax.axis_index("s") * SC_CORES + lax.axis_index("c")


def _dispatch_body(src_hbm, idx_hbm, out_hbm, idx_v, buf, sem):
    n_chunks = idx_v.shape[0] // TOP_K
    wid = _worker_id()
    base = wid * (n_chunks * LANES)
    pltpu.sync_copy(idx_hbm.at[wid], idx_v)

    @pl.loop(0, n_chunks)
    def _(j):
        pltpu.sync_copy(src_hbm.at[pl.ds(base + j * LANES, LANES)], buf)
        copies = [pltpu.async_copy(buf, out_hbm.at[idx_v.at[j * TOP_K + k]], sem)
                  for k in range(TOP_K)]
        for c in copies:
            c.wait()


def _sc_dispatch(src, idx, n_out):
    n_chunks = src.shape[0] // (SC_WORKERS * LANES)
    return pl.kernel(
        _dispatch_body,
        out_type=jax.ShapeDtypeStruct((n_out, LANES), jnp.int32),
        mesh=_sc_mesh(),
        scratch_types=[pltpu.VMEM((n_chunks * TOP_K, LANES), jnp.int32),
                       pltpu.VMEM((LANES, LANES), jnp.int32),
                       pltpu.SemaphoreType.DMA],
        name="sc_dispatch",
    )(src, idx)


def _gather_body(tab_hbm, idx_hbm, out_hbm, idx_v, buf0, buf1, sem0, sem1):
    n_chunks = idx_v.shape[0]
    wid = _worker_id()
    base = wid * (n_chunks * LANES)
    pltpu.sync_copy(idx_hbm.at[wid], idx_v)

    @pl.loop(0, n_chunks, step=2)
    def _(j):
        c0 = pltpu.async_copy(tab_hbm.at[idx_v.at[j]], buf0, sem0)
        c1 = pltpu.async_copy(tab_hbm.at[idx_v.at[j + 1]], buf1, sem1)
        c0.wait()
        pltpu.sync_copy(buf0, out_hbm.at[pl.ds(base + j * LANES, LANES)])
        c1.wait()
        pltpu.sync_copy(buf1, out_hbm.at[pl.ds(base + (j + 1) * LANES, LANES)])


def _sc_gather(tab, idx):
    n_chunks = idx.shape[1]
    return pl.kernel(
        _gather_body,
        out_type=jax.ShapeDtypeStruct((SC_WORKERS * n_chunks * LANES, LANES), jnp.int32),
        mesh=_sc_mesh(),
        scratch_types=[pltpu.VMEM((n_chunks, LANES), jnp.int32),
                       pltpu.VMEM((LANES, LANES), jnp.int32),
                       pltpu.VMEM((LANES, LANES), jnp.int32),
                       pltpu.SemaphoreType.DMA, pltpu.SemaphoreType.DMA],
        name="sc_gather",
    )(tab, idx)


def _slots_kernel(pstart_ref, rt_ref, sub_ref, *, n_rows):
    idx = rt_ref[0:TOP_K, :]
    dest = rt_ref[2 * TOP_K:3 * TOP_K, :]
    for e in range(N_EXPERTS):
        dest = dest + jnp.where(idx == float(e), pstart_ref[e].astype(F32), 0.0)
    dest = dest.astype(jnp.int32)
    for k in range(TOP_K):
        for s in range(SUBROWS):
            sub_ref[k, s:s + 1, :] = dest[k:k + 1, :] + s * n_rows


def _slots_call(pstart, route_t, n_rows):
    t = route_t.shape[1]
    tb = min(t, SLOT_TILE)
    grid_spec = pltpu.PrefetchScalarGridSpec(
        num_scalar_prefetch=1,
        grid=(t // tb,),
        in_specs=[pl.BlockSpec((ROUTE_ROWS, tb), lambda i, ps: (0, i))],
        out_specs=pl.BlockSpec((TOP_K, SUBROWS, tb), lambda i, ps: (0, 0, i)),
    )
    return pl.pallas_call(
        functools.partial(_slots_kernel, n_rows=n_rows),
        grid_spec=grid_spec,
        out_shape=jax.ShapeDtypeStruct((TOP_K, SUBROWS, t), jnp.int32),
        compiler_params=pltpu.CompilerParams(dimension_semantics=("arbitrary",)),
        name="slots",
    )(pstart, route_t)


def _expert_kernel(be_ref, bv_ref, nx_ref, sl_ref, x_ref, xs_ref, bg_ref, bu_ref, bd_ref,
                   wg_hbm, wu_hbm, wd_hbm, y_ref, stage, wg_scr, wu_scr, wd_scr, inv_scr, sems):
    i = pl.program_id(0)
    valid = bv_ref[i]
    first = jnp.logical_and(
        valid > 0, jnp.logical_or(i == 0, be_ref[i] != be_ref[jnp.maximum(i - 1, 0)]))

    def weight_copies(expert, slot):
        return [pltpu.make_async_copy(w.at[expert], stage.at[slot, j], sems.at[slot, j])
                for j, w in enumerate((wg_hbm, wu_hbm, wd_hbm))]

    @pl.when(i == 0)
    def _():
        for c in weight_copies(be_ref[0], sl_ref[0]):
            c.start()

    @pl.when(first)
    def _():
        slot = sl_ref[i]
        for c in weight_copies(be_ref[i], slot):
            c.wait()
        for j, scr in enumerate((wg_scr, wu_scr, wd_scr)):
            w = stage[slot, j].astype(BF16)
            amax = jnp.max(jnp.max(jnp.abs(w), axis=0, keepdims=True), axis=1, keepdims=True)
            scale = jnp.exp2(jnp.floor(jnp.log2(
                F8_TARGET / jnp.maximum(amax.astype(F32), TINY))))
            scr[...] = (w * scale.astype(BF16)).astype(F8)
            inv_scr[j] = jnp.broadcast_to(1.0 / scale, inv_scr.shape[1:])

        @pl.when(nx_ref[i] >= 0)
        def _():
            for c in weight_copies(nx_ref[i], 1 - slot):
                c.start()

    @pl.when(valid > 0)
    def _():
        lo, hi = _unpack_rows(x_ref)
        live = lax.broadcasted_iota(jnp.int32, (ROW_BLOCK, 1), 0) < valid
        x = jnp.where(live, jnp.concatenate(lo + hi, axis=1), 0.0).astype(BF16).astype(F8)
        inv_x = 1.0 / xs_ref[0:1, 0:1]
        cg = (inv_scr[0, 0:1, 0:1] * inv_x).astype(BF16)
        cl = (inv_scr[1, 0:1, 0:1] * inv_x * HID_SCALE).astype(BF16)
        g = _dot(x, wg_scr[...]).astype(BF16) * cg + bg_ref[...].astype(BF16)
        g = jnp.minimum(g, SWIGLU_LIMIT)
        l = _dot(x, wu_scr[...]).astype(BF16) * cl + (bu_ref[...] * HID_SCALE).astype(BF16)
        l = jnp.clip(l, -SWIGLU_LIMIT * HID_SCALE, SWIGLU_LIMIT * HID_SCALE) + HID_SCALE
        hid = g * l / (1.0 + jnp.exp2(g * (-SWIGLU_ALPHA * LOG2_E)))
        y = _dot(hid.astype(F8), wd_scr[...])
        _pack_rows(y * (inv_scr[2, 0:1, 0:1] * (1.0 / HID_SCALE)) + bd_ref[...], y_ref)

    @pl.when(valid <= 0)
    def _():
        y_ref[...] = jnp.zeros_like(y_ref)


def _expert_call(block_expert, block_valid, next_expert, slot, x_tab, x_scale,
                 w_gate, b_gate, w_up, b_up, w_down, b_down):
    d, f = w_gate.shape[-2:]
    assert d == f
    n_blocks = block_expert.shape[0]
    bspec = lambda width: pl.BlockSpec((None, 1, width), lambda i, be, bv, nx, sl: (be[i], 0, 0))
    rows = pl.BlockSpec((SUBROWS, ROW_BLOCK, LANES), lambda i, be, bv, nx, sl: (0, i, 0))
    hbm = pl.BlockSpec(memory_space=pl.ANY)
    grid_spec = pltpu.PrefetchScalarGridSpec(
        num_scalar_prefetch=4,
        grid=(n_blocks,),
        in_specs=[rows, pl.BlockSpec((SUBLANES, LANES), lambda i, be, bv, nx, sl: (0, 0)),
                  bspec(f), bspec(f), bspec(d), hbm, hbm, hbm],
        out_specs=rows,
        scratch_shapes=[pltpu.VMEM((2, 3, d, f), F32),
                        pltpu.VMEM((d, f), F8), pltpu.VMEM((d, f), F8), pltpu.VMEM((f, d), F8),
                        pltpu.VMEM((3, SUBLANES, LANES), F32), pltpu.SemaphoreType.DMA((2, 3))],
    )
    return pl.pallas_call(
        _expert_kernel,
        grid_spec=grid_spec,
        out_shape=jax.ShapeDtypeStruct(x_tab.shape, jnp.int32),
        compiler_params=pltpu.CompilerParams(
            dimension_semantics=("arbitrary",), vmem_limit_bytes=VMEM_LIMIT),
        name="expert",
    )(block_expert, block_valid, next_expert, slot, x_tab, x_scale, b_gate, b_up, b_down,
      w_gate, w_up, w_down)


def _combine_kernel(h_ref, routet_ref, gf_ref, ys_ref, *rest):
    out_ref = rest[-1]
    nb, tl, d = out_ref.shape
    m = nb * tl
    h = h_ref[...]
    route = jnp.transpose(jnp.concatenate(
        [routet_ref[...], jnp.zeros((LANES - ROUTE_ROWS, m), F32)], axis=0))
    lo_acc = [h[:, s * LANES:(s + 1) * LANES] for s in range(SUBROWS)]
    hi_acc = [h[:, PACKED + s * LANES:PACKED + (s + 1) * LANES] for s in range(SUBROWS)]
    for k in range(TOP_K):
        gate = route[:, TOP_K + k:TOP_K + k + 1]
        lo, hi = _unpack_rows(ys_ref.at[k])
        lo_acc = [a + gate * v for a, v in zip(lo_acc, lo)]
        hi_acc = [a + gate * v for a, v in zip(hi_acc, hi)]
    acc = jnp.concatenate(lo_acc + hi_acc, axis=1)
    out_ref[...] = _rmsnorm(acc, gf_ref[...]).reshape(nb, tl, d)


def _combine_call(h, route_t, g_final, y_slots, nb, seq, part, n_parts, prev):
    t, d = h.shape
    tl = TIME_TILE
    m = nb * tl
    n = t // m // n_parts
    first = part * n
    in_specs = [pl.BlockSpec((m, d), lambda i: (first + i, 0)),
                pl.BlockSpec((ROUTE_ROWS, m), lambda i: (0, first + i)),
                pl.BlockSpec((1, d), lambda i: (0, 0)),
                pl.BlockSpec((TOP_K, SUBROWS, m, LANES), lambda i: (0, 0, i, 0))]
    operands = [h, route_t, g_final, y_slots]
    aliases = {}
    if prev is not None:
        in_specs.append(pl.BlockSpec(memory_space=pl.ANY))
        operands.append(prev)
        aliases = {4: 0}
    return pl.pallas_call(
        _combine_kernel,
        grid=(n,),
        in_specs=in_specs,
        out_specs=pl.BlockSpec((nb, tl, d), lambda i: (0, first + i, 0)),
        out_shape=jax.ShapeDtypeStruct((nb, seq, d), F32),
        input_output_aliases=aliases,
        compiler_params=pltpu.CompilerParams(
            dimension_semantics=("arbitrary",), vmem_limit_bytes=VMEM_LIMIT),
        name="combine",
    )(*operands)


def _s5_operands(lam_re, lam_im, log_dt, b_re, b_im, c_re, c_im, nb):
    ns, pg = SSM_STATE, SSM_GROUP
    lam = lax.complex(lam_re.astype(F32), lam_im.astype(F32))
    dt = jnp.exp(log_dt.astype(F32))[:, None]
    lam_bar = jnp.exp(lam * dt)
    b_bar = ((lam_bar - 1.0) / lam)[..., None] * lax.complex(b_re.astype(F32), b_im.astype(F32))
    eye8 = jnp.eye(8, dtype=F32)
    split = lambda a: a.reshape((2, 2, 8) + a.shape[1:])

    def b_blocks(bb):
        return jnp.einsum('hpgnq,gk->phgqkn', split(bb), eye8).reshape(2, 2 * 8 * pg, 8 * ns)

    def c_blocks(cc):
        return jnp.einsum('hpgqn,gk->pgnhkq', split(cc), eye8).reshape(2, 8 * ns, 2 * 8 * pg)

    bd = jnp.concatenate([b_blocks(b_bar.real), b_blocks(b_bar.imag)], axis=2).astype(BF16)
    cd = jnp.concatenate([c_blocks(c_re.astype(F32)), -c_blocks(c_im.astype(F32))],
                         axis=1).astype(BF16)
    lam_rows = split(lam_bar).transpose(1, 0, 2, 3).reshape(2, 2, 8 * ns)
    lam_rows = jnp.repeat(lam_rows, nb, axis=1)
    return bd, cd, lam_rows.real, lam_rows.imag


def kernel(x, norm_mix_g, w_in, lam_re, lam_im, log_dt, b_re, b_im, c_re, c_im, d_skip, w_glu, b_glu, sgu_ln_g, sgu_ln_b, w_s, b_s, w_branch_a, w_branch_b, w_out, norm_moe_g, w_router, b_router, w_gate, b_gate, w_up, b_up, w_down, b_down, norm_final_g):
    nb, seq, d = x.shape
    assert d == D_MODEL and SUBLANES % nb == 0 and SUBLANES // nb == 2
    assert seq % TIME_TILE == 0 and norm_mix_g.shape[0] == 1
    assert (nb * seq * SUBROWS) % (SC_WORKERS * LANES) == 0
    assert (nb * seq * SUBROWS * TOP_K) % (SC_WORKERS * LANES * 2 * COMBINE_PARTS) == 0
    assert (seq // TIME_TILE) % COMBINE_PARTS == 0
    tl = TIME_TILE
    t = nb * seq
    row = lambda v: v.reshape(1, -1).astype(F32)

    bd, cd, a_re, a_im = _s5_operands(
        lam_re[0], lam_im[0], log_dt[0], b_re[0], b_im[0], c_re[0], c_im[0], nb)
    w_r = jnp.zeros((d, LANES), F32).at[:, :N_EXPERTS].set(w_router[0].astype(F32)).astype(BF16)
    b_r = jnp.full((1, LANES), NEG_BIG, F32).at[0, :N_EXPERTS].set(b_router[0].astype(F32))
    x_bound = math.sqrt(d) * jnp.max(jnp.abs(norm_moe_g[0].astype(F32)))
    x_scale = jnp.exp2(jnp.floor(jnp.log2(F8_TARGET / jnp.maximum(x_bound, TINY))))
    x_scale = jnp.full((SUBLANES, LANES), 1.0, F32) * x_scale
    m = nb * tl
    upper = (jnp.arange(m)[:, None] < jnp.arange(m)[None, :]).astype(BF16)
    h, xn2p, route_t, cnt = _mix_call(
        x, row(norm_mix_g[0]), w_in[0].astype(BF16), row(sgu_ln_g[0]), row(sgu_ln_b[0]),
        w_s[0].astype(F32), b_s[0].T.astype(F32), w_branch_b[0].astype(BF16),
        upper, bd, cd, a_re, a_im, row(d_skip[0]),
        w_glu[0].astype(BF16), row(b_glu[0]), w_branch_a[0].astype(BF16), w_out[0].astype(BF16),
        row(norm_moe_g[0]), x_scale, w_r, b_r)

    counts = cnt[:, 0].astype(jnp.int32)
    padded = (counts + ROW_BLOCK - 1) // ROW_BLOCK * ROW_BLOCK
    cum = jnp.cumsum(padded)
    pstart = cum - padded
    experts = jnp.arange(N_EXPERTS, dtype=jnp.int32)
    n_blocks = (t * TOP_K) // ROW_BLOCK + N_EXPERTS
    n_rows = n_blocks * ROW_BLOCK
    block_row0 = jnp.arange(n_blocks, dtype=jnp.int32) * ROW_BLOCK
    block_expert = jnp.minimum(
        jnp.sum((cum[None, :] <= block_row0[:, None]).astype(jnp.int32), axis=1), N_EXPERTS - 1)
    block_valid = jnp.clip(counts[block_expert] - (block_row0 - pstart[block_expert]), 0, ROW_BLOCK)
    present = counts > 0
    slot_e = (jnp.cumsum(present.astype(jnp.int32)) - 1) % 2
    later = lax.cummin(jnp.where(present, experts, N_EXPERTS)[::-1])[::-1]
    next_e = jnp.concatenate([later[1:], jnp.full((1,), N_EXPERTS, jnp.int32)])
    next_e = jnp.where(next_e == N_EXPERTS, -1, next_e)

    sub = _slots_call(pstart, route_t, n_rows)
    idx_dispatch = sub.reshape(TOP_K, SUBROWS * t // LANES, LANES).transpose(1, 0, 2)
    idx_dispatch = idx_dispatch.reshape(SC_WORKERS, -1, LANES)

    x_tab = _sc_dispatch(xn2p.reshape(SUBROWS * t, LANES), idx_dispatch, SUBROWS * n_rows)
    y_tab = _expert_call(
        block_expert, block_valid, next_e[block_expert], slot_e[block_expert],
        x_tab.reshape(SUBROWS, n_rows, LANES), x_scale,
        w_gate[0], b_gate[0][:, None, :], w_up[0], b_up[0][:, None, :],
        w_down[0], b_down[0][:, None, :])
    y_flat = y_tab.reshape(SUBROWS * n_rows, LANES)
    tp = t // COMBINE_PARTS
    idx_parts = sub.reshape(TOP_K, SUBROWS, COMBINE_PARTS, tp).transpose(2, 0, 1, 3)
    out = None
    for q in range(COMBINE_PARTS):
        y_slots = _sc_gather(y_flat, idx_parts[q].reshape(SC_WORKERS, -1, LANES))
        out = _combine_call(h, route_t, row(norm_final_g),
                            y_slots.reshape(TOP_K, SUBROWS, tp, LANES), nb, seq,
                            q, COMBINE_PARTS, out)
    return out
```

```python
import functools
import math

import jax
import jax.numpy as jnp
from jax import lax
from jax.experimental import pallas as pl
from jax.experimental.pallas import tpu as pltpu
from jax.experimental.pallas import tpu_sc as plsc

F32 = jnp.float32
BF16 = jnp.bfloat16
F8 = jnp.float8_e4m3fn
F8_TARGET = 240.0
TINY = 1e-30

NORM_EPS = 1e-5
D_MODEL = 1024
SSM_WIDTH = 512
SSM_GROUP = 16
SSM_GROUPS = 32
SSM_STATE = 64
SGU_WIDTH = 512
SGU_BLOCK = 128
SGU_HEADS = 4
SGU_HEAD_DIM = 128
CHUNK = 64
N_EXPERTS = 32
TOP_K = 4
SWIGLU_ALPHA = 1.702
SWIGLU_LIMIT = 7.0
LOG2_E = 1.4426950408889634
HID_SCALE = 4.0

LANES = 128
SUBLANES = 8
TIME_TILE = 128
SCAN_BLOCK = 64
ROW_BLOCK = 512
VMEM_LIMIT = 56 * 1024 * 1024
NEG_BIG = -1e30
SC_CORES = 2
SC_SUBCORES = 16
SC_WORKERS = SC_CORES * SC_SUBCORES
SUBROWS = 4
COMBINE_PARTS = 4
SLOT_TILE = 4096
ROUTE_ROWS = 16
PACKED = D_MODEL // 2


def _pack_rows(x, out_ref):
    lo = lax.bitcast_convert_type(x[:, :PACKED].astype(BF16).astype(F32), jnp.int32)
    hi = lax.bitcast_convert_type(x[:, PACKED:].astype(BF16).astype(F32), jnp.int32)
    words = lax.shift_right_logical(lo, 16) | (hi & jnp.int32(-65536))
    for s in range(SUBROWS):
        out_ref[s] = words[:, s * LANES:(s + 1) * LANES]


def _unpack_rows(ref):
    lo, hi = [], []
    for s in range(SUBROWS):
        w = ref[s]
        lo.append(lax.bitcast_convert_type(lax.shift_left(w, 16), F32))
        hi.append(lax.bitcast_convert_type(w & jnp.int32(-65536), F32))
    return lo, hi


def _dot(a, b):
    return jnp.dot(a, b, preferred_element_type=F32)


def _sigmoid(x):
    return 1.0 / (1.0 + jnp.exp(-x))


def _gelu(x):
    return 0.5 * x * (1.0 + jnp.tanh(0.7978845608028654 * (x + 0.044715 * (x * x * x))))


def _rmsnorm(x, g):
    return x * lax.rsqrt(jnp.mean(x * x, axis=-1, keepdims=True) + NORM_EPS) * g


def _proj_xa(x, g_ref, w_ref):
    xn = _rmsnorm(x, g_ref[...]).astype(BF16)
    return xn, _dot(xn, w_ref[:, 0:SSM_WIDTH]).astype(BF16)


def _proj_gate(xn, w_ref, lo):
    return _sigmoid(_dot(xn, w_ref[:, lo:lo + D_MODEL]))


def _sgu(z, lng_ref, lnb_ref, ws_ref, bs_ref, yb_scr):
    m = z.shape[0]
    u = z[:, :SGU_WIDTH]
    v = z[:, SGU_WIDTH:]
    mu = jnp.mean(v, axis=-1, keepdims=True)
    vc = v - mu
    v = vc * lax.rsqrt(jnp.mean(vc * vc, axis=-1, keepdims=True) + NORM_EPS)
    v = (v * lng_ref[...] + lnb_ref[...]).astype(BF16)

    ri = lax.broadcasted_iota(jnp.int32, (SGU_BLOCK, SGU_BLOCK), 0) // CHUNK
    ci = lax.broadcasted_iota(jnp.int32, (SGU_BLOCK, SGU_BLOCK), 1) // CHUNK
    causal = ri >= ci
    for h in range(SGU_HEADS):
        wm = jnp.where(causal, ws_ref[h], 0.0).astype(BF16)
        bias = bs_ref[:, h:h + 1]
        lo = h * SGU_HEAD_DIM
        for blk in range(m // SGU_BLOCK):
            r0 = blk * SGU_BLOCK
            s = _dot(wm, v[r0:r0 + SGU_BLOCK, lo:lo + SGU_HEAD_DIM]) + bias
            yb_scr[r0:r0 + SGU_BLOCK, lo:lo + SGU_HEAD_DIM] = (
                u[r0:r0 + SGU_BLOCK, lo:lo + SGU_HEAD_DIM] * s).astype(BF16)


def _mix_kernel(x_ref, g1_ref, win_ref, lng_ref, lnb_ref, ws_ref, bs_ref, wb_ref,
                upper_ref, bd_ref, cd_ref,
                are_ref, aim_ref, dskip_ref, wglu_ref, bglu_ref, wa_ref, wout_ref,
                g2_ref, xs_ref, wr_ref, br_ref,
                h_ref, xn2_ref, routet_ref, cnt_ref,
                bu0_scr, bu1_scr, x8_scr, y8_scr, yb_scr, state_scr, cnt_scr):
    nb, tl, d = x_ref.shape
    m = nb * tl
    x = x_ref[...].reshape(m, d)
    xn, xa_bf = _proj_xa(x, g1_ref, win_ref)

    @pl.when(pl.program_id(0) == 0)
    def _():
        state_scr[...] = jnp.zeros_like(state_scr)
        cnt_scr[...] = jnp.zeros_like(cnt_scr)
        x8_scr[...] = jnp.zeros_like(x8_scr)

    xa = xa_bf.astype(F32)

    def chunk_lo(p, hf):
        return hf * (SSM_WIDTH // 2) + p * LANES

    scans = (bu0_scr, bu1_scr)
    for p in range(2):
        for hf in range(2):
            lo = chunk_lo(p, hf)
            for b in range(nb):
                x8_scr[2 * p + hf, pl.ds(hf * nb + b, tl, stride=SUBLANES), :] = (
                    xa[b * tl:(b + 1) * tl, lo:lo + LANES])
        x8 = jnp.concatenate([x8_scr[2 * p], x8_scr[2 * p + 1]], axis=1).astype(BF16)
        scans[p][...] = _dot(x8, bd_ref[p])

    a_re = [are_ref[p] for p in range(2)]
    a_im = [aim_ref[p] for p in range(2)]
    s_re = [state_scr[p][:, :SSM_WIDTH] for p in range(2)]
    s_im = [state_scr[p][:, SSM_WIDTH:] for p in range(2)]
    s_b = SSM_WIDTH + 2 * SGU_WIDTH
    z = _gelu(_dot(xn, win_ref[:, SSM_WIDTH:s_b]))
    gates = []
    for t0 in range(0, tl, SCAN_BLOCK):
        rows_p = ([], [])
        for t in range(t0, t0 + SCAN_BLOCK):
            r = t * SUBLANES
            for p in range(2):
                bu_scr = scans[p]
                n_re = a_re[p] * s_re[p] - a_im[p] * s_im[p] + bu_scr[r:r + SUBLANES, 0:SSM_WIDTH]
                n_im = (a_re[p] * s_im[p] + a_im[p] * s_re[p]
                        + bu_scr[r:r + SUBLANES, SSM_WIDTH:2 * SSM_WIDTH])
                s_re[p], s_im[p] = n_re, n_im
                rows_p[p].append(jnp.concatenate([n_re, n_im], axis=1))
        r0 = t0 * SUBLANES
        for p in range(2):
            y8 = _dot(jnp.concatenate(rows_p[p], axis=0).astype(BF16), cd_ref[p])
            for hf in range(2):
                y8_scr[2 * p + hf, r0:r0 + SCAN_BLOCK * SUBLANES, :] = (
                    y8[:, hf * LANES:(hf + 1) * LANES])
        gates.append(_proj_gate(xn, win_ref, s_b + len(gates) * D_MODEL))

    sga = gates[0]

    y_chunks = {}
    for p in range(2):
        state_scr[p] = jnp.concatenate([s_re[p], s_im[p]], axis=1)
        for hf in range(2):
            y_chunks[(hf, p)] = jnp.concatenate(
                [y8_scr[2 * p + hf, pl.ds(hf * nb + b, tl, stride=SUBLANES), :]
                 for b in range(nb)], axis=0)
    y = jnp.concatenate([y_chunks[(hf, p)] for hf in range(2) for p in range(2)], axis=1)
    y = y + dskip_ref[...] * xa
    zs = _gelu(y)
    ya = zs * _sigmoid(_dot(zs.astype(BF16), wglu_ref[...]) + bglu_ref[...])
    _sgu(z, lng_ref, lnb_ref, ws_ref, bs_ref, yb_scr)
    pb = (gates[1] * _dot(yb_scr[...], wb_ref[...])).astype(BF16)
    pa = sga * _dot(ya.astype(BF16), wa_ref[...])
    merged = pa + pb.astype(F32)
    h = x + _dot(merged.astype(BF16), wout_ref[...])
    h_ref[...] = h
    xn2 = _rmsnorm(h, g2_ref[...])
    _pack_rows(xn2 * xs_ref[0:1, 0:1], xn2_ref)

    logits = _dot(xn2.astype(BF16), wr_ref[...]) + br_ref[...]
    work = jnp.transpose(logits)[:N_EXPERTS]
    expert = lax.broadcasted_iota(jnp.int32, (N_EXPERTS, m), 0).astype(F32)
    vals, idxs = [], []
    member = jnp.zeros((N_EXPERTS, m), F32)
    for _ in range(TOP_K):
        mx = jnp.max(work, axis=0, keepdims=True)
        ix = jnp.min(jnp.where(work == mx, expert, float(N_EXPERTS)), axis=0, keepdims=True)
        hit = expert == ix
        member = jnp.where(hit, 1.0, member)
        work = jnp.where(hit, NEG_BIG, work)
        vals.append(mx)
        idxs.append(ix)
    exps = [jnp.exp(v - vals[0]) for v in vals]
    denom = exps[0] + exps[1] + exps[2] + exps[3]
    gates = [e / denom for e in exps]

    before = _dot(member.astype(BF16), upper_ref[...]) + cnt_scr[:, 0:1]
    new_cnt = cnt_scr[:, 0:1] + jnp.sum(member, axis=1, keepdims=True)
    cnt_scr[...] = jnp.broadcast_to(new_cnt, cnt_scr.shape)
    cnt_ref[...] = jnp.broadcast_to(new_cnt, cnt_ref.shape)
    ranks = [jnp.sum(jnp.where(expert == ix, before, 0.0), axis=0, keepdims=True) for ix in idxs]
    pad = jnp.zeros((ROUTE_ROWS - 3 * TOP_K, m), F32)
    routet_ref[...] = jnp.concatenate(idxs + gates + ranks + [pad], axis=0)


def _mix_call(x, *params):
    nb, seq, d = x.shape
    tl = TIME_TILE
    m = nb * tl
    n_tiles = seq // tl
    const = lambda shape: pl.BlockSpec(shape, lambda i: (0,) * len(shape))
    resident = lambda shape: pl.BlockSpec(shape, lambda i: (0,) * len(shape),
                                          pipeline_mode=pl.Buffered(1))
    tile = lambda width: pl.BlockSpec((nb, tl, width), lambda i: (0, i, 0))
    rows = lambda width: pl.BlockSpec((m, width), lambda i: (i, 0))
    operands = (x,) + params
    in_specs = [tile(d)] + [resident(o.shape) for o in params]
    return pl.pallas_call(
        _mix_kernel,
        grid=(n_tiles,),
        in_specs=in_specs,
        out_specs=[rows(d), pl.BlockSpec((SUBROWS, m, LANES), lambda i: (0, i, 0)),
                   pl.BlockSpec((ROUTE_ROWS, m), lambda i: (0, i)), const((N_EXPERTS, LANES))],
        out_shape=[jax.ShapeDtypeStruct((n_tiles * m, d), F32),
                   jax.ShapeDtypeStruct((SUBROWS, n_tiles * m, LANES), jnp.int32),
                   jax.ShapeDtypeStruct((ROUTE_ROWS, n_tiles * m), F32),
                   jax.ShapeDtypeStruct((N_EXPERTS, LANES), F32)],
        scratch_shapes=[pltpu.VMEM((SUBLANES * tl, 2 * SSM_WIDTH), F32),
                        pltpu.VMEM((SUBLANES * tl, 2 * SSM_WIDTH), F32),
                        pltpu.VMEM((4, SUBLANES * tl, LANES), F32),
                        pltpu.VMEM((4, SUBLANES * tl, LANES), F32),
                        pltpu.VMEM((m, SGU_WIDTH), BF16),
                        pltpu.VMEM((2, SUBLANES, 2 * SSM_WIDTH), F32),
                        pltpu.VMEM((N_EXPERTS, LANES), F32)],
        compiler_params=pltpu.CompilerParams(
            dimension_semantics=("arbitrary",), vmem_limit_bytes=VMEM_LIMIT),
        name="mix",
    )(*operands)


def _sc_mesh():
    return plsc.VectorSubcoreMesh(core_axis_name="c", subcore_axis_name="s")


def _worker_id():
    return lax.axis_index("s") * SC_CORES + lax.axis_index("c")


def _dispatch_body(src_hbm, idx_hbm, out_hbm, idx_v, buf0, buf1, sem_r0, sem_r1, sem_w):
    n_chunks = idx_v.shape[0] // TOP_K
    wid = _worker_id()
    base = wid * (n_chunks * LANES)
    pltpu.sync_copy(idx_hbm.at[wid], idx_v)
    bufs = ((buf0, sem_r0), (buf1, sem_r1))

    def read(j, b):
        return pltpu.make_async_copy(src_hbm.at[pl.ds(base + j * LANES, LANES)], bufs[b][0],
                                     bufs[b][1])

    read(0, 0).start()

    @pl.loop(0, n_chunks, step=2)
    def _(j0):
        for b in range(2):
            j = j0 + b
            read(j, b).wait()

            @pl.when(j + 1 < n_chunks)
            def _():
                read(j + 1, 1 - b).start()

            copies = [pltpu.async_copy(bufs[b][0], out_hbm.at[idx_v.at[j * TOP_K + k]], sem_w)
                      for k in range(TOP_K)]
            for c in copies:
                c.wait()


def _sc_dispatch(src, idx, n_out):
    n_chunks = src.shape[0] // (SC_WORKERS * LANES)
    assert n_chunks % 2 == 0
    return pl.kernel(
        _dispatch_body,
        out_type=jax.ShapeDtypeStruct((n_out, LANES), jnp.int32),
        mesh=_sc_mesh(),
        scratch_types=[pltpu.VMEM((n_chunks * TOP_K, LANES), jnp.int32),
                       pltpu.VMEM((LANES, LANES), jnp.int32),
                       pltpu.VMEM((LANES, LANES), jnp.int32),
                       pltpu.SemaphoreType.DMA, pltpu.SemaphoreType.DMA,
                       pltpu.SemaphoreType.DMA],
        name="sc_dispatch",
    )(src, idx)


def _gather_body(tab_hbm, idx_hbm, out_hbm, idx_v, buf0, buf1, sem0, sem1):
    n_chunks = idx_v.shape[0]
    wid = _worker_id()
    base = wid * (n_chunks * LANES)
    pltpu.sync_copy(idx_hbm.at[wid], idx_v)

    @pl.loop(0, n_chunks, step=2)
    def _(j):
        c0 = pltpu.async_copy(tab_hbm.at[idx_v.at[j]], buf0, sem0)
        c1 = pltpu.async_copy(tab_hbm.at[idx_v.at[j + 1]], buf1, sem1)
        c0.wait()
        pltpu.sync_copy(buf0, out_hbm.at[pl.ds(base + j * LANES, LANES)])
        c1.wait()
        pltpu.sync_copy(buf1, out_hbm.at[pl.ds(base + (j + 1) * LANES, LANES)])


def _sc_gather(tab, idx):
    n_chunks = idx.shape[1]
    return pl.kernel(
        _gather_body,
        out_type=jax.ShapeDtypeStruct((SC_WORKERS * n_chunks * LANES, LANES), jnp.int32),
        mesh=_sc_mesh(),
        scratch_types=[pltpu.VMEM((n_chunks, LANES), jnp.int32),
                       pltpu.VMEM((LANES, LANES), jnp.int32),
                       pltpu.VMEM((LANES, LANES), jnp.int32),
                       pltpu.SemaphoreType.DMA, pltpu.SemaphoreType.DMA],
        name="sc_gather",
    )(tab, idx)


def _slots_kernel(pstart_ref, rt_ref, gat_ref, dis_ref, *, n_rows):
    tb = rt_ref.shape[1]
    idx = rt_ref[0:TOP_K, :]
    dest = rt_ref[2 * TOP_K:3 * TOP_K, :]
    for e in range(N_EXPERTS):
        dest = dest + jnp.where(idx == float(e), pstart_ref[e].astype(F32), 0.0)
    dest = dest.astype(jnp.int32)
    for k in range(TOP_K):
        for s in range(SUBROWS):
            row = dest[k:k + 1, :] + s * n_rows
            gat_ref[k, s:s + 1, :] = row
            for c in range(tb // LANES):
                dis_ref[s, c, k:k + 1, :] = row[:, c * LANES:(c + 1) * LANES]


def _slots_call(pstart, route_t, n_rows, n_parts):
    t = route_t.shape[1]
    tp = t // n_parts
    tb = min(tp, SLOT_TILE)
    per_part = tp // tb
    grid_spec = pltpu.PrefetchScalarGridSpec(
        num_scalar_prefetch=1,
        grid=(t // tb,),
        in_specs=[pl.BlockSpec((ROUTE_ROWS, tb), lambda i, ps: (0, i))],
        out_specs=[pl.BlockSpec((None, TOP_K, SUBROWS, tb),
                                lambda i, ps: (i // per_part, 0, 0, i % per_part)),
                   pl.BlockSpec((SUBROWS, tb // LANES, TOP_K, LANES), lambda i, ps: (0, i, 0, 0))],
    )
    return pl.pallas_call(
        functools.partial(_slots_kernel, n_rows=n_rows),
        grid_spec=grid_spec,
        out_shape=[jax.ShapeDtypeStruct((n_parts, TOP_K, SUBROWS, tp), jnp.int32),
                   jax.ShapeDtypeStruct((SUBROWS, t // LANES, TOP_K, LANES), jnp.int32)],
        compiler_params=pltpu.CompilerParams(dimension_semantics=("arbitrary",)),
        name="slots",
    )(pstart, route_t)


def _expert_kernel(be_ref, bv_ref, nx_ref, sl_ref, x_ref, xs_ref, bg_ref, bu_ref, bd_ref,
                   wg_hbm, wu_hbm, wd_hbm, y_ref, stage, wg_scr, wu_scr, wd_scr, inv_scr, sems):
    i = pl.program_id(0)
    valid = bv_ref[i]
    first = jnp.logical_and(
        valid > 0, jnp.logical_or(i == 0, be_ref[i] != be_ref[jnp.maximum(i - 1, 0)]))

    def weight_copies(expert, slot):
        return [pltpu.make_async_copy(w.at[expert], stage.at[slot, j], sems.at[slot, j])
                for j, w in enumerate((wg_hbm, wu_hbm, wd_hbm))]

    @pl.when(i == 0)
    def _():
        for c in weight_copies(be_ref[0], sl_ref[0]):
            c.start()

    @pl.when(first)
    def _():
        slot = sl_ref[i]
        for c in weight_copies(be_ref[i], slot):
            c.wait()
        for j, scr in enumerate((wg_scr, wu_scr, wd_scr)):
            w = stage[slot, j].astype(BF16)
            amax = jnp.max(jnp.max(jnp.abs(w), axis=0, keepdims=True), axis=1, keepdims=True)
            scale = jnp.exp2(jnp.floor(jnp.log2(
                F8_TARGET / jnp.maximum(amax.astype(F32), TINY))))
            scr[...] = (w * scale.astype(BF16)).astype(F8)
            inv_scr[j] = jnp.broadcast_to(1.0 / scale, inv_scr.shape[1:])

        @pl.when(nx_ref[i] >= 0)
        def _():
            for c in weight_copies(nx_ref[i], 1 - slot):
                c.start()

    @pl.when(valid > 0)
    def _():
        lo, hi = _unpack_rows(x_ref)
        live = lax.broadcasted_iota(jnp.int32, (ROW_BLOCK, 1), 0) < valid
        x = jnp.where(live, jnp.concatenate(lo + hi, axis=1), 0.0).astype(BF16).astype(F8)
        inv_x = 1.0 / xs_ref[0:1, 0:1]
        cg = (inv_scr[0, 0:1, 0:1] * inv_x).astype(BF16)
        cl = (inv_scr[1, 0:1, 0:1] * inv_x * HID_SCALE).astype(BF16)
        g = _dot(x, wg_scr[...]).astype(BF16) * cg + bg_ref[...].astype(BF16)
        g = jnp.minimum(g, SWIGLU_LIMIT)
        l = _dot(x, wu_scr[...]).astype(BF16) * cl + (bu_ref[...] * HID_SCALE).astype(BF16)
        l = jnp.clip(l, -SWIGLU_LIMIT * HID_SCALE, SWIGLU_LIMIT * HID_SCALE) + HID_SCALE
        hid = g * l / (1.0 + jnp.exp2(g * (-SWIGLU_ALPHA * LOG2_E)))
        y = _dot(hid.astype(F8), wd_scr[...])
        _pack_rows(y * (inv_scr[2, 0:1, 0:1] * (1.0 / HID_SCALE)) + bd_ref[...], y_ref)

    @pl.when(valid <= 0)
    def _():
        y_ref[...] = jnp.zeros_like(y_ref)


def _expert_call(block_expert, block_valid, next_expert, slot, x_tab, x_scale,
                 w_gate, b_gate, w_up, b_up, w_down, b_down):
    d, f = w_gate.shape[-2:]
    assert d == f
    n_blocks = block_expert.shape[0]
    bspec = lambda width: pl.BlockSpec((None, 1, width), lambda i, be, bv, nx, sl: (be[i], 0, 0))
    rows = pl.BlockSpec((SUBROWS, ROW_BLOCK, LANES), lambda i, be, bv, nx, sl: (0, i, 0))
    hbm = pl.BlockSpec(memory_space=pl.ANY)
    grid_spec = pltpu.PrefetchScalarGridSpec(
        num_scalar_prefetch=4,
        grid=(n_blocks,),
        in_specs=[rows, pl.BlockSpec((SUBLANES, LANES), lambda i, be, bv, nx, sl: (0, 0)),
                  bspec(f), bspec(f), bspec(d), hbm, hbm, hbm],
        out_specs=rows,
        scratch_shapes=[pltpu.VMEM((2, 3, d, f), F32),
                        pltpu.VMEM((d, f), F8), pltpu.VMEM((d, f), F8), pltpu.VMEM((f, d), F8),
                        pltpu.VMEM((3, SUBLANES, LANES), F32), pltpu.SemaphoreType.DMA((2, 3))],
    )
    return pl.pallas_call(
        _expert_kernel,
        grid_spec=grid_spec,
        out_shape=jax.ShapeDtypeStruct(x_tab.shape, jnp.int32),
        compiler_params=pltpu.CompilerParams(
            dimension_semantics=("arbitrary",), vmem_limit_bytes=VMEM_LIMIT),
        name="expert",
    )(block_expert, block_valid, next_expert, slot, x_tab, x_scale, b_gate, b_up, b_down,
      w_gate, w_up, w_down)


def _combine_kernel(h_ref, routet_ref, gf_ref, ys_ref, *rest):
    out_ref = rest[-1]
    nb, tl, d = out_ref.shape
    m = nb * tl
    h = h_ref[...]
    route = jnp.transpose(jnp.concatenate(
        [routet_ref[...], jnp.zeros((LANES - ROUTE_ROWS, m), F32)], axis=0))
    lo_acc = [h[:, s * LANES:(s + 1) * LANES] for s in range(SUBROWS)]
    hi_acc = [h[:, PACKED + s * LANES:PACKED + (s + 1) * LANES] for s in range(SUBROWS)]
    for k in range(TOP_K):
        gate = route[:, TOP_K + k:TOP_K + k + 1]
        lo, hi = _unpack_rows(ys_ref.at[k])
        lo_acc = [a + gate * v for a, v in zip(lo_acc, lo)]
        hi_acc = [a + gate * v for a, v in zip(hi_acc, hi)]
    acc = jnp.concatenate(lo_acc + hi_acc, axis=1)
    out_ref[...] = _rmsnorm(acc, gf_ref[...]).reshape(nb, tl, d)


def _combine_call(h, route_t, g_final, y_slots, nb, seq, part, n_parts, prev):
    t, d = h.shape
    tl = TIME_TILE
    m = nb * tl
    n = t // m // n_parts
    first = part * n
    in_specs = [pl.BlockSpec((m, d), lambda i: (first + i, 0)),
                pl.BlockSpec((ROUTE_ROWS, m), lambda i: (0, first + i)),
                pl.BlockSpec((1, d), lambda i: (0, 0)),
                pl.BlockSpec((TOP_K, SUBROWS, m, LANES), lambda i: (0, 0, i, 0))]
    operands = [h, route_t, g_final, y_slots]
    aliases = {}
    if prev is not None:
        in_specs.append(pl.BlockSpec(memory_space=pl.ANY))
        operands.append(prev)
        aliases = {4: 0}
    return pl.pallas_call(
        _combine_kernel,
        grid=(n,),
        in_specs=in_specs,
        out_specs=pl.BlockSpec((nb, tl, d), lambda i: (0, first + i, 0)),
        out_shape=jax.ShapeDtypeStruct((nb, seq, d), F32),
        input_output_aliases=aliases,
        compiler_params=pltpu.CompilerParams(
            dimension_semantics=("arbitrary",), vmem_limit_bytes=VMEM_LIMIT),
        name="combine",
    )(*operands)


def _s5_operands(lam_re, lam_im, log_dt, b_re, b_im, c_re, c_im, nb):
    ns, pg = SSM_STATE, SSM_GROUP
    lam = lax.complex(lam_re.astype(F32), lam_im.astype(F32))
    dt = jnp.exp(log_dt.astype(F32))[:, None]
    lam_bar = jnp.exp(lam * dt)
    b_bar = ((lam_bar - 1.0) / lam)[..., None] * lax.complex(b_re.astype(F32), b_im.astype(F32))
    eye8 = jnp.eye(8, dtype=F32)
    split = lambda a: a.reshape((2, 2, 8) + a.shape[1:])

    def b_blocks(bb):
        return jnp.einsum('hpgnq,gk->phgqkn', split(bb), eye8).reshape(2, 2 * 8 * pg, 8 * ns)

    def c_blocks(cc):
        return jnp.einsum('hpgqn,gk->pgnhkq', split(cc), eye8).reshape(2, 8 * ns, 2 * 8 * pg)

    bd = jnp.concatenate([b_blocks(b_bar.real), b_blocks(b_bar.imag)], axis=2).astype(BF16)
    cd = jnp.concatenate([c_blocks(c_re.astype(F32)), -c_blocks(c_im.astype(F32))],
                         axis=1).astype(BF16)
    lam_rows = split(lam_bar).transpose(1, 0, 2, 3).reshape(2, 2, 8 * ns)
    lam_rows = jnp.repeat(lam_rows, nb, axis=1)
    return bd, cd, lam_rows.real, lam_rows.imag


def kernel(x, norm_mix_g, w_in, lam_re, lam_im, log_dt, b_re, b_im, c_re, c_im, d_skip, w_glu, b_glu, sgu_ln_g, sgu_ln_b, w_s, b_s, w_branch_a, w_branch_b, w_out, norm_moe_g, w_router, b_router, w_gate, b_gate, w_up, b_up, w_down, b_down, norm_final_g):
    nb, seq, d = x.shape
    assert d == D_MODEL and SUBLANES % nb == 0 and SUBLANES // nb == 2
    assert seq % TIME_TILE == 0 and norm_mix_g.shape[0] == 1
    assert (nb * seq * SUBROWS) % (SC_WORKERS * LANES) == 0
    assert (nb * seq * SUBROWS * TOP_K) % (SC_WORKERS * LANES * 2 * COMBINE_PARTS) == 0
    assert (seq // TIME_TILE) % COMBINE_PARTS == 0
    tl = TIME_TILE
    t = nb * seq
    row = lambda v: v.reshape(1, -1).astype(F32)

    bd, cd, a_re, a_im = _s5_operands(
        lam_re[0], lam_im[0], log_dt[0], b_re[0], b_im[0], c_re[0], c_im[0], nb)
    w_r = jnp.zeros((d, LANES), F32).at[:, :N_EXPERTS].set(w_router[0].astype(F32)).astype(BF16)
    b_r = jnp.full((1, LANES), NEG_BIG, F32).at[0, :N_EXPERTS].set(b_router[0].astype(F32))
    x_bound = math.sqrt(d) * jnp.max(jnp.abs(norm_moe_g[0].astype(F32)))
    x_scale = jnp.exp2(jnp.floor(jnp.log2(F8_TARGET / jnp.maximum(x_bound, TINY))))
    x_scale = jnp.full((SUBLANES, LANES), 1.0, F32) * x_scale
    m = nb * tl
    upper = (jnp.arange(m)[:, None] < jnp.arange(m)[None, :]).astype(BF16)
    h, xn2p, route_t, cnt = _mix_call(
        x, row(norm_mix_g[0]), w_in[0].astype(BF16), row(sgu_ln_g[0]), row(sgu_ln_b[0]),
        w_s[0].astype(F32), b_s[0].T.astype(F32), w_branch_b[0].astype(BF16),
        upper, bd, cd, a_re, a_im, row(d_skip[0]),
        w_glu[0].astype(BF16), row(b_glu[0]), w_branch_a[0].astype(BF16), w_out[0].astype(BF16),
        row(norm_moe_g[0]), x_scale, w_r, b_r)

    counts = cnt[:, 0].astype(jnp.int32)
    padded = (counts + ROW_BLOCK - 1) // ROW_BLOCK * ROW_BLOCK
    cum = jnp.cumsum(padded)
    pstart = cum - padded
    experts = jnp.arange(N_EXPERTS, dtype=jnp.int32)
    n_blocks = (t * TOP_K) // ROW_BLOCK + N_EXPERTS
    n_rows = n_blocks * ROW_BLOCK
    block_row0 = jnp.arange(n_blocks, dtype=jnp.int32) * ROW_BLOCK
    block_expert = jnp.minimum(
        jnp.sum((cum[None, :] <= block_row0[:, None]).astype(jnp.int32), axis=1), N_EXPERTS - 1)
    block_valid = jnp.clip(counts[block_expert] - (block_row0 - pstart[block_expert]), 0, ROW_BLOCK)
    present = counts > 0
    slot_e = (jnp.cumsum(present.astype(jnp.int32)) - 1) % 2
    later = lax.cummin(jnp.where(present, experts, N_EXPERTS)[::-1])[::-1]
    next_e = jnp.concatenate([later[1:], jnp.full((1,), N_EXPERTS, jnp.int32)])
    next_e = jnp.where(next_e == N_EXPERTS, -1, next_e)

    idx_parts, idx_dispatch = _slots_call(pstart, route_t, n_rows, COMBINE_PARTS)
    idx_dispatch = idx_dispatch.reshape(SC_WORKERS, -1, LANES)

    x_tab = _sc_dispatch(xn2p.reshape(SUBROWS * t, LANES), idx_dispatch, SUBROWS * n_rows)
    y_tab = _expert_call(
        block_expert, block_valid, next_e[block_expert], slot_e[block_expert],
        x_tab.reshape(SUBROWS, n_rows, LANES), x_scale,
        w_gate[0], b_gate[0][:, None, :], w_up[0], b_up[0][:, None, :],
        w_down[0], b_down[0][:, None, :])
    y_flat = y_tab.reshape(SUBROWS * n_rows, LANES)
    tp = t // COMBINE_PARTS
    out = None
    for q in range(COMBINE_PARTS):
        y_slots = _sc_gather(y_flat, idx_parts[q].reshape(SC_WORKERS, -1, LANES))
        out = _combine_call(h, route_t, row(norm_final_g),
                            y_slots.reshape(TOP_K, SUBROWS, tp, LANES), nb, seq,
                            q, COMBINE_PARTS, out)
    return out
```

```python
import functools
import math

import jax
import jax.numpy as jnp
from jax import lax
from jax.experimental import pallas as pl
from jax.experimental.pallas import tpu as pltpu
from jax.experimental.pallas import tpu_sc as plsc

F32 = jnp.float32
BF16 = jnp.bfloat16
F8 = jnp.float8_e4m3fn
F8_TARGET = 240.0
TINY = 1e-30

NORM_EPS = 1e-5
D_MODEL = 1024
SSM_WIDTH = 512
SSM_GROUP = 16
SSM_GROUPS = 32
SSM_STATE = 64
SGU_WIDTH = 512
SGU_BLOCK = 128
SGU_HEADS = 4
SGU_HEAD_DIM = 128
CHUNK = 64
N_EXPERTS = 32
TOP_K = 4
SWIGLU_ALPHA = 1.702
SWIGLU_LIMIT = 7.0
LOG2_E = 1.4426950408889634
HID_SCALE = 4.0

LANES = 128
SUBLANES = 8
TIME_TILE = 128
SCAN_BLOCK = 64
ROW_BLOCK = 512
VMEM_LIMIT = 56 * 1024 * 1024
NEG_BIG = -1e30
SC_CORES = 2
SC_SUBCORES = 16
SC_WORKERS = SC_CORES * SC_SUBCORES
SUBROWS = 4
COMBINE_PARTS = 4
SLOT_TILE = 4096
ROUTE_ROWS = 16
PACKED = D_MODEL // 2


def _pack_rows(x, out_ref):
    lo = lax.bitcast_convert_type(x[:, :PACKED].astype(BF16).astype(F32), jnp.int32)
    hi = lax.bitcast_convert_type(x[:, PACKED:].astype(BF16).astype(F32), jnp.int32)
    words = lax.shift_right_logical(lo, 16) | (hi & jnp.int32(-65536))
    for s in range(SUBROWS):
        out_ref[s] = words[:, s * LANES:(s + 1) * LANES]


def _unpack_rows(ref):
    lo, hi = [], []
    for s in range(SUBROWS):
        w = ref[s]
        lo.append(lax.bitcast_convert_type(lax.shift_left(w, 16), F32))
        hi.append(lax.bitcast_convert_type(w & jnp.int32(-65536), F32))
    return lo, hi


def _dot(a, b):
    return jnp.dot(a, b, preferred_element_type=F32)


def _sigmoid(x):
    return 1.0 / (1.0 + jnp.exp(-x))


def _gelu(x):
    return 0.5 * x * (1.0 + jnp.tanh(0.7978845608028654 * (x + 0.044715 * (x * x * x))))


def _rmsnorm(x, g):
    return x * lax.rsqrt(jnp.mean(x * x, axis=-1, keepdims=True) + NORM_EPS) * g


def _proj_xa(x, g_ref, w_ref):
    xn = _rmsnorm(x, g_ref[...]).astype(BF16)
    return xn, _dot(xn, w_ref[:, 0:SSM_WIDTH]).astype(BF16)


def _proj_gate(xn, w_ref, lo):
    return _sigmoid(_dot(xn, w_ref[:, lo:lo + D_MODEL]))


def _sgu(z, lng_ref, lnb_ref, ws_ref, bs_ref, yb_scr):
    m = z.shape[0]
    u = z[:, :SGU_WIDTH]
    v = z[:, SGU_WIDTH:]
    mu = jnp.mean(v, axis=-1, keepdims=True)
    vc = v - mu
    v = vc * lax.rsqrt(jnp.mean(vc * vc, axis=-1, keepdims=True) + NORM_EPS)
    v = (v * lng_ref[...] + lnb_ref[...]).astype(BF16)

    ri = lax.broadcasted_iota(jnp.int32, (SGU_BLOCK, SGU_BLOCK), 0) // CHUNK
    ci = lax.broadcasted_iota(jnp.int32, (SGU_BLOCK, SGU_BLOCK), 1) // CHUNK
    causal = ri >= ci
    for h in range(SGU_HEADS):
        wm = jnp.where(causal, ws_ref[h], 0.0).astype(BF16)
        bias = bs_ref[:, h:h + 1]
        lo = h * SGU_HEAD_DIM
        for blk in range(m // SGU_BLOCK):
            r0 = blk * SGU_BLOCK
            s = _dot(wm, v[r0:r0 + SGU_BLOCK, lo:lo + SGU_HEAD_DIM]) + bias
            yb_scr[r0:r0 + SGU_BLOCK, lo:lo + SGU_HEAD_DIM] = (
                u[r0:r0 + SGU_BLOCK, lo:lo + SGU_HEAD_DIM] * s).astype(BF16)


def _mix_kernel(x_ref, g1_ref, win_ref, lng_ref, lnb_ref, ws_ref, bs_ref, wb_ref,
                upper_ref, bd_ref, cd_ref,
                are_ref, aim_ref, dskip_ref, wglu_ref, bglu_ref, wa_ref, wout_ref,
                g2_ref, xs_ref, wr_ref, br_ref,
                h_ref, xn2_ref, routet_ref, cnt_ref,
                bu0_scr, bu1_scr, x8_scr, y8_scr, yb_scr, state_scr, cnt_scr):
    nb, tl, d = x_ref.shape
    m = nb * tl
    x = x_ref[...].reshape(m, d)
    xn, xa_bf = _proj_xa(x, g1_ref, win_ref)

    @pl.when(pl.program_id(0) == 0)
    def _():
        state_scr[...] = jnp.zeros_like(state_scr)
        cnt_scr[...] = jnp.zeros_like(cnt_scr)
        x8_scr[...] = jnp.zeros_like(x8_scr)

    xa = xa_bf.astype(F32)

    def chunk_lo(p, hf):
        return hf * (SSM_WIDTH // 2) + p * LANES

    scans = (bu0_scr, bu1_scr)
    for p in range(2):
        for hf in range(2):
            lo = chunk_lo(p, hf)
            for b in range(nb):
                x8_scr[2 * p + hf, pl.ds(hf * nb + b, tl, stride=SUBLANES), :] = (
                    xa[b * tl:(b + 1) * tl, lo:lo + LANES])
        x8 = jnp.concatenate([x8_scr[2 * p], x8_scr[2 * p + 1]], axis=1).astype(BF16)
        scans[p][...] = _dot(x8, bd_ref[p])

    a_re = [are_ref[p] for p in range(2)]
    a_im = [aim_ref[p] for p in range(2)]
    s_re = [state_scr[p][:, :SSM_WIDTH] for p in range(2)]
    s_im = [state_scr[p][:, SSM_WIDTH:] for p in range(2)]
    s_b = SSM_WIDTH + 2 * SGU_WIDTH
    z = _gelu(_dot(xn, win_ref[:, SSM_WIDTH:s_b]))
    gates = []
    for t0 in range(0, tl, SCAN_BLOCK):
        rows_p = ([], [])
        for t in range(t0, t0 + SCAN_BLOCK):
            r = t * SUBLANES
            for p in range(2):
                bu_scr = scans[p]
                n_re = a_re[p] * s_re[p] - a_im[p] * s_im[p] + bu_scr[r:r + SUBLANES, 0:SSM_WIDTH]
                n_im = (a_re[p] * s_im[p] + a_im[p] * s_re[p]
                        + bu_scr[r:r + SUBLANES, SSM_WIDTH:2 * SSM_WIDTH])
                s_re[p], s_im[p] = n_re, n_im
                rows_p[p].append(jnp.concatenate([n_re, n_im], axis=1))
        r0 = t0 * SUBLANES
        for p in range(2):
            y8 = _dot(jnp.concatenate(rows_p[p], axis=0).astype(BF16), cd_ref[p])
            for hf in range(2):
                y8_scr[2 * p + hf, r0:r0 + SCAN_BLOCK * SUBLANES, :] = (
                    y8[:, hf * LANES:(hf + 1) * LANES])
        gates.append(_proj_gate(xn, win_ref, s_b + len(gates) * D_MODEL))

    sga = gates[0]

    y_chunks = {}
    for p in range(2):
        state_scr[p] = jnp.concatenate([s_re[p], s_im[p]], axis=1)
        for hf in range(2):
            y_chunks[(hf, p)] = jnp.concatenate(
                [y8_scr[2 * p + hf, pl.ds(hf * nb + b, tl, stride=SUBLANES), :]
                 for b in range(nb)], axis=0)
    y = jnp.concatenate([y_chunks[(hf, p)] for hf in range(2) for p in range(2)], axis=1)
    y = y + dskip_ref[...] * xa
    zs = _gelu(y)
    ya = zs * _sigmoid(_dot(zs.astype(BF16), wglu_ref[...]) + bglu_ref[...])
    _sgu(z, lng_ref, lnb_ref, ws_ref, bs_ref, yb_scr)
    pb = (gates[1] * _dot(yb_scr[...], wb_ref[...])).astype(BF16)
    pa = sga * _dot(ya.astype(BF16), wa_ref[...])
    merged = pa + pb.astype(F32)
    h = x + _dot(merged.astype(BF16), wout_ref[...])
    _pack_rows(h, h_ref)
    xn2 = _rmsnorm(h, g2_ref[...])
    _pack_rows(xn2 * xs_ref[0:1, 0:1], xn2_ref)

    logits = _dot(xn2.astype(BF16), wr_ref[...]) + br_ref[...]
    work = jnp.transpose(logits)[:N_EXPERTS]
    expert = lax.broadcasted_iota(jnp.int32, (N_EXPERTS, m), 0).astype(F32)
    vals, idxs = [], []
    member = jnp.zeros((N_EXPERTS, m), F32)
    for _ in range(TOP_K):
        mx = jnp.max(work, axis=0, keepdims=True)
        ix = jnp.min(jnp.where(work == mx, expert, float(N_EXPERTS)), axis=0, keepdims=True)
        hit = expert == ix
        member = jnp.where(hit, 1.0, member)
        work = jnp.where(hit, NEG_BIG, work)
        vals.append(mx)
        idxs.append(ix)
    exps = [jnp.exp(v - vals[0]) for v in vals]
    denom = exps[0] + exps[1] + exps[2] + exps[3]
    gates = [e / denom for e in exps]

    before = _dot(member.astype(BF16), upper_ref[...]) + cnt_scr[:, 0:1]
    new_cnt = cnt_scr[:, 0:1] + jnp.sum(member, axis=1, keepdims=True)
    cnt_scr[...] = jnp.broadcast_to(new_cnt, cnt_scr.shape)
    cnt_ref[...] = jnp.broadcast_to(new_cnt, cnt_ref.shape)
    ranks = [jnp.sum(jnp.where(expert == ix, before, 0.0), axis=0, keepdims=True) for ix in idxs]
    pad = jnp.zeros((ROUTE_ROWS - 3 * TOP_K, m), F32)
    routet_ref[...] = jnp.concatenate(idxs + gates + ranks + [pad], axis=0)


def _mix_call(x, *params):
    nb, seq, d = x.shape
    tl = TIME_TILE
    m = nb * tl
    n_tiles = seq // tl
    const = lambda shape: pl.BlockSpec(shape, lambda i: (0,) * len(shape))
    resident = lambda shape: pl.BlockSpec(shape, lambda i: (0,) * len(shape),
                                          pipeline_mode=pl.Buffered(1))
    tile = lambda width: pl.BlockSpec((nb, tl, width), lambda i: (0, i, 0))
    packed_rows = pl.BlockSpec((SUBROWS, m, LANES), lambda i: (0, i, 0))
    operands = (x,) + params
    in_specs = [tile(d)] + [resident(o.shape) for o in params]
    return pl.pallas_call(
        _mix_kernel,
        grid=(n_tiles,),
        in_specs=in_specs,
        out_specs=[packed_rows, packed_rows,
                   pl.BlockSpec((ROUTE_ROWS, m), lambda i: (0, i)), const((N_EXPERTS, LANES))],
        out_shape=[jax.ShapeDtypeStruct((SUBROWS, n_tiles * m, LANES), jnp.int32),
                   jax.ShapeDtypeStruct((SUBROWS, n_tiles * m, LANES), jnp.int32),
                   jax.ShapeDtypeStruct((ROUTE_ROWS, n_tiles * m), F32),
                   jax.ShapeDtypeStruct((N_EXPERTS, LANES), F32)],
        scratch_shapes=[pltpu.VMEM((SUBLANES * tl, 2 * SSM_WIDTH), F32),
                        pltpu.VMEM((SUBLANES * tl, 2 * SSM_WIDTH), F32),
                        pltpu.VMEM((4, SUBLANES * tl, LANES), F32),
                        pltpu.VMEM((4, SUBLANES * tl, LANES), F32),
                        pltpu.VMEM((m, SGU_WIDTH), BF16),
                        pltpu.VMEM((2, SUBLANES, 2 * SSM_WIDTH), F32),
                        pltpu.VMEM((N_EXPERTS, LANES), F32)],
        compiler_params=pltpu.CompilerParams(
            dimension_semantics=("arbitrary",), vmem_limit_bytes=VMEM_LIMIT),
        name="mix",
    )(*operands)


def _sc_mesh():
    return plsc.VectorSubcoreMesh(core_axis_name="c", subcore_axis_name="s")


def _worker_id():
    return lax.axis_index("s") * SC_CORES + lax.axis_index("c")


def _dispatch_body(src_hbm, idx_hbm, out_hbm, idx_v, buf0, buf1, sem_r0, sem_r1, sem_w):
    n_chunks = idx_v.shape[0] // TOP_K
    wid = _worker_id()
    base = wid * (n_chunks * LANES)
    pltpu.sync_copy(idx_hbm.at[wid], idx_v)
    bufs = ((buf0, sem_r0), (buf1, sem_r1))

    def read(j, b):
        return pltpu.make_async_copy(src_hbm.at[pl.ds(base + j * LANES, LANES)], bufs[b][0],
                                     bufs[b][1])

    read(0, 0).start()

    @pl.loop(0, n_chunks, step=2)
    def _(j0):
        for b in range(2):
            j = j0 + b
            read(j, b).wait()

            @pl.when(j + 1 < n_chunks)
            def _():
                read(j + 1, 1 - b).start()

            copies = [pltpu.async_copy(bufs[b][0], out_hbm.at[idx_v.at[j * TOP_K + k]], sem_w)
                      for k in range(TOP_K)]
            for c in copies:
                c.wait()


def _sc_dispatch(src, idx, n_out):
    n_chunks = src.shape[0] // (SC_WORKERS * LANES)
    assert n_chunks % 2 == 0
    return pl.kernel(
        _dispatch_body,
        out_type=jax.ShapeDtypeStruct((n_out, LANES), jnp.int32),
        mesh=_sc_mesh(),
        scratch_types=[pltpu.VMEM((n_chunks * TOP_K, LANES), jnp.int32),
                       pltpu.VMEM((LANES, LANES), jnp.int32),
                       pltpu.VMEM((LANES, LANES), jnp.int32),
                       pltpu.SemaphoreType.DMA, pltpu.SemaphoreType.DMA,
                       pltpu.SemaphoreType.DMA],
        name="sc_dispatch",
    )(src, idx)


def _gather_body(tab_hbm, idx_hbm, out_hbm, idx_v, buf0, buf1, sem0, sem1):
    n_chunks = idx_v.shape[0]
    wid = _worker_id()
    base = wid * (n_chunks * LANES)
    pltpu.sync_copy(idx_hbm.at[wid], idx_v)

    @pl.loop(0, n_chunks, step=2)
    def _(j):
        c0 = pltpu.async_copy(tab_hbm.at[idx_v.at[j]], buf0, sem0)
        c1 = pltpu.async_copy(tab_hbm.at[idx_v.at[j + 1]], buf1, sem1)
        c0.wait()
        pltpu.sync_copy(buf0, out_hbm.at[pl.ds(base + j * LANES, LANES)])
        c1.wait()
        pltpu.sync_copy(buf1, out_hbm.at[pl.ds(base + (j + 1) * LANES, LANES)])


def _sc_gather(tab, idx):
    n_chunks = idx.shape[1]
    return pl.kernel(
        _gather_body,
        out_type=jax.ShapeDtypeStruct((SC_WORKERS * n_chunks * LANES, LANES), jnp.int32),
        mesh=_sc_mesh(),
        scratch_types=[pltpu.VMEM((n_chunks, LANES), jnp.int32),
                       pltpu.VMEM((LANES, LANES), jnp.int32),
                       pltpu.VMEM((LANES, LANES), jnp.int32),
                       pltpu.SemaphoreType.DMA, pltpu.SemaphoreType.DMA],
        name="sc_gather",
    )(tab, idx)


def _slots_kernel(pstart_ref, rt_ref, gat_ref, dis_ref, *, n_rows):
    tb = rt_ref.shape[1]
    idx = rt_ref[0:TOP_K, :]
    dest = rt_ref[2 * TOP_K:3 * TOP_K, :]
    for e in range(N_EXPERTS):
        dest = dest + jnp.where(idx == float(e), pstart_ref[e].astype(F32), 0.0)
    dest = dest.astype(jnp.int32)
    for k in range(TOP_K):
        for s in range(SUBROWS):
            row = dest[k:k + 1, :] + s * n_rows
            gat_ref[k, s:s + 1, :] = row
            for c in range(tb // LANES):
                dis_ref[s, c, k:k + 1, :] = row[:, c * LANES:(c + 1) * LANES]


def _slots_call(pstart, route_t, n_rows, n_parts):
    t = route_t.shape[1]
    tp = t // n_parts
    tb = min(tp, SLOT_TILE)
    per_part = tp // tb
    grid_spec = pltpu.PrefetchScalarGridSpec(
        num_scalar_prefetch=1,
        grid=(t // tb,),
        in_specs=[pl.BlockSpec((ROUTE_ROWS, tb), lambda i, ps: (0, i))],
        out_specs=[pl.BlockSpec((None, TOP_K, SUBROWS, tb),
                                lambda i, ps: (i // per_part, 0, 0, i % per_part)),
                   pl.BlockSpec((SUBROWS, tb // LANES, TOP_K, LANES), lambda i, ps: (0, i, 0, 0))],
    )
    return pl.pallas_call(
        functools.partial(_slots_kernel, n_rows=n_rows),
        grid_spec=grid_spec,
        out_shape=[jax.ShapeDtypeStruct((n_parts, TOP_K, SUBROWS, tp), jnp.int32),
                   jax.ShapeDtypeStruct((SUBROWS, t // LANES, TOP_K, LANES), jnp.int32)],
        compiler_params=pltpu.CompilerParams(dimension_semantics=("arbitrary",)),
        name="slots",
    )(pstart, route_t)


def _expert_kernel(be_ref, bv_ref, nx_ref, sl_ref, x_ref, xs_ref, bg_ref, bu_ref, bd_ref,
                   wg_hbm, wu_hbm, wd_hbm, y_ref, stage, wg_scr, wu_scr, wd_scr, inv_scr, sems):
    i = pl.program_id(0)
    valid = bv_ref[i]
    first = jnp.logical_and(
        valid > 0, jnp.logical_or(i == 0, be_ref[i] != be_ref[jnp.maximum(i - 1, 0)]))

    def weight_copies(expert, slot):
        return [pltpu.make_async_copy(w.at[expert], stage.at[slot, j], sems.at[slot, j])
                for j, w in enumerate((wg_hbm, wu_hbm, wd_hbm))]

    @pl.when(i == 0)
    def _():
        for c in weight_copies(be_ref[0], sl_ref[0]):
            c.start()

    @pl.when(first)
    def _():
        slot = sl_ref[i]
        for c in weight_copies(be_ref[i], slot):
            c.wait()
        for j, scr in enumerate((wg_scr, wu_scr, wd_scr)):
            w = stage[slot, j].astype(BF16)
            amax = jnp.max(jnp.max(jnp.abs(w), axis=0, keepdims=True), axis=1, keepdims=True)
            scale = jnp.exp2(jnp.floor(jnp.log2(
                F8_TARGET / jnp.maximum(amax.astype(F32), TINY))))
            scr[...] = (w * scale.astype(BF16)).astype(F8)
            inv_scr[j] = jnp.broadcast_to(1.0 / scale, inv_scr.shape[1:])

        @pl.when(nx_ref[i] >= 0)
        def _():
            for c in weight_copies(nx_ref[i], 1 - slot):
                c.start()

    @pl.when(valid > 0)
    def _():
        lo, hi = _unpack_rows(x_ref)
        live = lax.broadcasted_iota(jnp.int32, (ROW_BLOCK, 1), 0) < valid
        x = jnp.where(live, jnp.concatenate(lo + hi, axis=1), 0.0).astype(BF16).astype(F8)
        inv_x = 1.0 / xs_ref[0:1, 0:1]
        cg = (inv_scr[0, 0:1, 0:1] * inv_x).astype(BF16)
        cl = (inv_scr[1, 0:1, 0:1] * inv_x * HID_SCALE).astype(BF16)
        g = _dot(x, wg_scr[...]).astype(BF16) * cg + bg_ref[...].astype(BF16)
        g = jnp.minimum(g, SWIGLU_LIMIT)
        l = _dot(x, wu_scr[...]).astype(BF16) * cl + (bu_ref[...] * HID_SCALE).astype(BF16)
        l = jnp.clip(l, -SWIGLU_LIMIT * HID_SCALE, SWIGLU_LIMIT * HID_SCALE) + HID_SCALE
        hid = g * l / (1.0 + jnp.exp2(g * (-SWIGLU_ALPHA * LOG2_E)))
        y = _dot(hid.astype(F8), wd_scr[...])
        _pack_rows(y * (inv_scr[2, 0:1, 0:1] * (1.0 / HID_SCALE)) + bd_ref[...], y_ref)

    @pl.when(valid <= 0)
    def _():
        y_ref[...] = jnp.zeros_like(y_ref)


def _expert_call(block_expert, block_valid, next_expert, slot, x_tab, x_scale,
                 w_gate, b_gate, w_up, b_up, w_down, b_down):
    d, f = w_gate.shape[-2:]
    assert d == f
    n_blocks = block_expert.shape[0]
    bspec = lambda width: pl.BlockSpec((None, 1, width), lambda i, be, bv, nx, sl: (be[i], 0, 0))
    rows = pl.BlockSpec((SUBROWS, ROW_BLOCK, LANES), lambda i, be, bv, nx, sl: (0, i, 0))
    hbm = pl.BlockSpec(memory_space=pl.ANY)
    grid_spec = pltpu.PrefetchScalarGridSpec(
        num_scalar_prefetch=4,
        grid=(n_blocks,),
        in_specs=[rows, pl.BlockSpec((SUBLANES, LANES), lambda i, be, bv, nx, sl: (0, 0)),
                  bspec(f), bspec(f), bspec(d), hbm, hbm, hbm],
        out_specs=rows,
        scratch_shapes=[pltpu.VMEM((2, 3, d, f), F32),
                        pltpu.VMEM((d, f), F8), pltpu.VMEM((d, f), F8), pltpu.VMEM((f, d), F8),
                        pltpu.VMEM((3, SUBLANES, LANES), F32), pltpu.SemaphoreType.DMA((2, 3))],
    )
    return pl.pallas_call(
        _expert_kernel,
        grid_spec=grid_spec,
        out_shape=jax.ShapeDtypeStruct(x_tab.shape, jnp.int32),
        compiler_params=pltpu.CompilerParams(
            dimension_semantics=("arbitrary",), vmem_limit_bytes=VMEM_LIMIT),
        name="expert",
    )(block_expert, block_valid, next_expert, slot, x_tab, x_scale, b_gate, b_up, b_down,
      w_gate, w_up, w_down)


def _combine_kernel(h_ref, routet_ref, gf_ref, ys_ref, *rest):
    out_ref = rest[-1]
    nb, tl, d = out_ref.shape
    m = nb * tl
    route = jnp.transpose(jnp.concatenate(
        [routet_ref[...], jnp.zeros((LANES - ROUTE_ROWS, m), F32)], axis=0))
    lo_acc, hi_acc = _unpack_rows(h_ref)
    for k in range(TOP_K):
        gate = route[:, TOP_K + k:TOP_K + k + 1]
        lo, hi = _unpack_rows(ys_ref.at[k])
        lo_acc = [a + gate * v for a, v in zip(lo_acc, lo)]
        hi_acc = [a + gate * v for a, v in zip(hi_acc, hi)]
    acc = jnp.concatenate(lo_acc + hi_acc, axis=1)
    out_ref[...] = _rmsnorm(acc, gf_ref[...]).reshape(nb, tl, d)


def _combine_call(h, route_t, g_final, y_slots, nb, seq, part, n_parts, prev):
    t, d = h.shape[1], D_MODEL
    tl = TIME_TILE
    m = nb * tl
    n = t // m // n_parts
    first = part * n
    in_specs = [pl.BlockSpec((SUBROWS, m, LANES), lambda i: (0, first + i, 0)),
                pl.BlockSpec((ROUTE_ROWS, m), lambda i: (0, first + i)),
                pl.BlockSpec((1, d), lambda i: (0, 0)),
                pl.BlockSpec((TOP_K, SUBROWS, m, LANES), lambda i: (0, 0, i, 0))]
    operands = [h, route_t, g_final, y_slots]
    aliases = {}
    if prev is not None:
        in_specs.append(pl.BlockSpec(memory_space=pl.ANY))
        operands.append(prev)
        aliases = {4: 0}
    return pl.pallas_call(
        _combine_kernel,
        grid=(n,),
        in_specs=in_specs,
        out_specs=pl.BlockSpec((nb, tl, d), lambda i: (0, first + i, 0)),
        out_shape=jax.ShapeDtypeStruct((nb, seq, d), F32),
        input_output_aliases=aliases,
        compiler_params=pltpu.CompilerParams(
            dimension_semantics=("arbitrary",), vmem_limit_bytes=VMEM_LIMIT),
        name="combine",
    )(*operands)


def _s5_operands(lam_re, lam_im, log_dt, b_re, b_im, c_re, c_im, nb):
    ns, pg = SSM_STATE, SSM_GROUP
    lam = lax.complex(lam_re.astype(F32), lam_im.astype(F32))
    dt = jnp.exp(log_dt.astype(F32))[:, None]
    lam_bar = jnp.exp(lam * dt)
    b_bar = ((lam_bar - 1.0) / lam)[..., None] * lax.complex(b_re.astype(F32), b_im.astype(F32))
    eye8 = jnp.eye(8, dtype=F32)
    split = lambda a: a.reshape((2, 2, 8) + a.shape[1:])

    def b_blocks(bb):
        return jnp.einsum('hpgnq,gk->phgqkn', split(bb), eye8).reshape(2, 2 * 8 * pg, 8 * ns)

    def c_blocks(cc):
        return jnp.einsum('hpgqn,gk->pgnhkq', split(cc), eye8).reshape(2, 8 * ns, 2 * 8 * pg)

    bd = jnp.concatenate([b_blocks(b_bar.real), b_blocks(b_bar.imag)], axis=2).astype(BF16)
    cd = jnp.concatenate([c_blocks(c_re.astype(F32)), -c_blocks(c_im.astype(F32))],
                         axis=1).astype(BF16)
    lam_rows = split(lam_bar).transpose(1, 0, 2, 3).reshape(2, 2, 8 * ns)
    lam_rows = jnp.repeat(lam_rows, nb, axis=1)
    return bd, cd, lam_rows.real, lam_rows.imag


def kernel(x, norm_mix_g, w_in, lam_re, lam_im, log_dt, b_re, b_im, c_re, c_im, d_skip, w_glu, b_glu, sgu_ln_g, sgu_ln_b, w_s, b_s, w_branch_a, w_branch_b, w_out, norm_moe_g, w_router, b_router, w_gate, b_gate, w_up, b_up, w_down, b_down, norm_final_g):
    nb, seq, d = x.shape
    assert d == D_MODEL and SUBLANES % nb == 0 and SUBLANES // nb == 2
    assert seq % TIME_TILE == 0 and norm_mix_g.shape[0] == 1
    assert (nb * seq * SUBROWS) % (SC_WORKERS * LANES) == 0
    assert (nb * seq * SUBROWS * TOP_K) % (SC_WORKERS * LANES * 2 * COMBINE_PARTS) == 0
    assert (seq // TIME_TILE) % COMBINE_PARTS == 0
    tl = TIME_TILE
    t = nb * seq
    row = lambda v: v.reshape(1, -1).astype(F32)

    bd, cd, a_re, a_im = _s5_operands(
        lam_re[0], lam_im[0], log_dt[0], b_re[0], b_im[0], c_re[0], c_im[0], nb)
    w_r = jnp.zeros((d, LANES), F32).at[:, :N_EXPERTS].set(w_router[0].astype(F32)).astype(BF16)
    b_r = jnp.full((1, LANES), NEG_BIG, F32).at[0, :N_EXPERTS].set(b_router[0].astype(F32))
    x_bound = math.sqrt(d) * jnp.max(jnp.abs(norm_moe_g[0].astype(F32)))
    x_scale = jnp.exp2(jnp.floor(jnp.log2(F8_TARGET / jnp.maximum(x_bound, TINY))))
    x_scale = jnp.full((SUBLANES, LANES), 1.0, F32) * x_scale
    m = nb * tl
    upper = (jnp.arange(m)[:, None] < jnp.arange(m)[None, :]).astype(BF16)
    h, xn2p, route_t, cnt = _mix_call(
        x, row(norm_mix_g[0]), w_in[0].astype(BF16), row(sgu_ln_g[0]), row(sgu_ln_b[0]),
        w_s[0].astype(F32), b_s[0].T.astype(F32), w_branch_b[0].astype(BF16),
        upper, bd, cd, a_re, a_im, row(d_skip[0]),
        w_glu[0].astype(BF16), row(b_glu[0]), w_branch_a[0].astype(BF16), w_out[0].astype(BF16),
        row(norm_moe_g[0]), x_scale, w_r, b_r)

    counts = cnt[:, 0].astype(jnp.int32)
    padded = (counts + ROW_BLOCK - 1) // ROW_BLOCK * ROW_BLOCK
    experts = jnp.arange(N_EXPERTS, dtype=jnp.int32)
    upto = experts[None, :] <= experts[:, None]
    cum = jnp.sum(jnp.where(upto, padded[None, :], 0), axis=1)
    pstart = cum - padded
    n_blocks = (t * TOP_K) // ROW_BLOCK + N_EXPERTS
    n_rows = n_blocks * ROW_BLOCK
    block_row0 = jnp.arange(n_blocks, dtype=jnp.int32) * ROW_BLOCK
    block_expert = jnp.minimum(
        jnp.sum((cum[None, :] <= block_row0[:, None]).astype(jnp.int32), axis=1), N_EXPERTS - 1)
    of_block = block_expert[:, None] == experts[None, :]
    pick = lambda table: jnp.sum(jnp.where(of_block, table[None, :], 0), axis=1)
    block_valid = jnp.clip(pick(counts) - (block_row0 - pick(pstart)), 0, ROW_BLOCK)
    present = counts > 0
    slot_e = (jnp.sum(jnp.where(upto, present[None, :].astype(jnp.int32), 0), axis=1) - 1) % 2
    later = jnp.logical_and(experts[None, :] > experts[:, None], present[None, :])
    next_e = jnp.min(jnp.where(later, experts[None, :], N_EXPERTS), axis=1)
    next_e = jnp.where(next_e == N_EXPERTS, -1, next_e)

    idx_parts, idx_dispatch = _slots_call(pstart, route_t, n_rows, COMBINE_PARTS)
    idx_dispatch = idx_dispatch.reshape(SC_WORKERS, -1, LANES)

    x_tab = _sc_dispatch(xn2p.reshape(SUBROWS * t, LANES), idx_dispatch, SUBROWS * n_rows)
    y_tab = _expert_call(
        block_expert, block_valid, pick(next_e), pick(slot_e),
        x_tab.reshape(SUBROWS, n_rows, LANES), x_scale,
        w_gate[0], b_gate[0][:, None, :], w_up[0], b_up[0][:, None, :],
        w_down[0], b_down[0][:, None, :])
    y_flat = y_tab.reshape(SUBROWS * n_rows, LANES)
    tp = t // COMBINE_PARTS
    out = None
    for q in range(COMBINE_PARTS):
        y_slots = _sc_gather(y_flat, idx_parts[q].reshape(SC_WORKERS, -1, LANES))
        out = _combine_call(h, route_t, row(norm_final_g),
                            y_slots.reshape(TOP_K, SUBROWS, tp, LANES), nb, seq,
                            q, COMBINE_PARTS, out)
    return out
```

```python
import functools
import math

import jax
import jax.numpy as jnp
from jax import lax
from jax.experimental import pallas as pl
from jax.experimental.pallas import tpu as pltpu
from jax.experimental.pallas import tpu_sc as plsc

F32 = jnp.float32
BF16 = jnp.bfloat16
F8 = jnp.float8_e4m3fn
F8_TARGET = 240.0
TINY = 1e-30

NORM_EPS = 1e-5
D_MODEL = 1024
SSM_WIDTH = 512
SSM_GROUP = 16
SSM_GROUPS = 32
SSM_STATE = 64
SGU_WIDTH = 512
SGU_BLOCK = 128
SGU_HEADS = 4
SGU_HEAD_DIM = 128
CHUNK = 64
N_EXPERTS = 32
TOP_K = 4
SWIGLU_ALPHA = 1.702
SWIGLU_LIMIT = 7.0
LOG2_E = 1.4426950408889634
HID_SCALE = 4.0

LANES = 128
SUBLANES = 8
TIME_TILE = 128
SCAN_BLOCK = 64
ROW_BLOCK = 512
VMEM_LIMIT = 56 * 1024 * 1024
NEG_BIG = -1e30
SC_CORES = 2
SC_SUBCORES = 16
SC_WORKERS = SC_CORES * SC_SUBCORES
DISPATCH_RING = 4
SUBROWS = 4
COMBINE_PARTS = 4
SLOT_TILE = 4096
ROUTE_ROWS = 16
PACKED = D_MODEL // 2


def _pack_rows(x, out_ref):
    lo = lax.bitcast_convert_type(x[:, :PACKED].astype(BF16).astype(F32), jnp.int32)
    hi = lax.bitcast_convert_type(x[:, PACKED:].astype(BF16).astype(F32), jnp.int32)
    words = lax.shift_right_logical(lo, 16) | (hi & jnp.int32(-65536))
    for s in range(SUBROWS):
        out_ref[s] = words[:, s * LANES:(s + 1) * LANES]


def _unpack_rows(ref):
    lo, hi = [], []
    for s in range(SUBROWS):
        w = ref[s]
        lo.append(lax.bitcast_convert_type(lax.shift_left(w, 16), F32))
        hi.append(lax.bitcast_convert_type(w & jnp.int32(-65536), F32))
    return lo, hi


def _dot(a, b):
    return jnp.dot(a, b, preferred_element_type=F32)


def _sigmoid(x):
    return 1.0 / (1.0 + jnp.exp(-x))


def _gelu(x):
    return 0.5 * x * (1.0 + jnp.tanh(0.7978845608028654 * (x + 0.044715 * (x * x * x))))


def _rmsnorm(x, g):
    return x * lax.rsqrt(jnp.mean(x * x, axis=-1, keepdims=True) + NORM_EPS) * g


def _proj_xa(x, g_ref, w_ref):
    xn = _rmsnorm(x, g_ref[...]).astype(BF16)
    return xn, _dot(xn, w_ref[:, 0:SSM_WIDTH]).astype(BF16)


def _proj_gate(xn, w_ref, lo):
    return _sigmoid(_dot(xn, w_ref[:, lo:lo + D_MODEL]))


def _sgu(z, lng_ref, lnb_ref, ws_ref, bs_ref, yb_scr):
    m = z.shape[0]
    u = z[:, :SGU_WIDTH]
    v = z[:, SGU_WIDTH:]
    mu = jnp.mean(v, axis=-1, keepdims=True)
    vc = v - mu
    v = vc * lax.rsqrt(jnp.mean(vc * vc, axis=-1, keepdims=True) + NORM_EPS)
    v = (v * lng_ref[...] + lnb_ref[...]).astype(BF16)

    ri = lax.broadcasted_iota(jnp.int32, (SGU_BLOCK, SGU_BLOCK), 0) // CHUNK
    ci = lax.broadcasted_iota(jnp.int32, (SGU_BLOCK, SGU_BLOCK), 1) // CHUNK
    causal = ri >= ci
    for h in range(SGU_HEADS):
        wm = jnp.where(causal, ws_ref[h], 0.0).astype(BF16)
        bias = bs_ref[:, h:h + 1]
        lo = h * SGU_HEAD_DIM
        for blk in range(m // SGU_BLOCK):
            r0 = blk * SGU_BLOCK
            s = _dot(wm, v[r0:r0 + SGU_BLOCK, lo:lo + SGU_HEAD_DIM]) + bias
            yb_scr[r0:r0 + SGU_BLOCK, lo:lo + SGU_HEAD_DIM] = (
                u[r0:r0 + SGU_BLOCK, lo:lo + SGU_HEAD_DIM] * s).astype(BF16)


def _mix_kernel(x_ref, g1_ref, win_ref, lng_ref, lnb_ref, ws_ref, bs_ref, wb_ref,
                upper_ref, bd_ref, cd_ref,
                are_ref, aim_ref, dskip_ref, wglu_ref, bglu_ref, wa_ref, wout_ref,
                g2_ref, xs_ref, wr_ref, br_ref,
                h_ref, xn2_ref, routet_ref, cnt_ref,
                bu0_scr, bu1_scr, x8_scr, y8_scr, yb_scr, state_scr, cnt_scr):
    nb, tl, d = x_ref.shape
    m = nb * tl
    x = x_ref[...].reshape(m, d)
    xn, xa_bf = _proj_xa(x, g1_ref, win_ref)

    @pl.when(pl.program_id(0) == 0)
    def _():
        state_scr[...] = jnp.zeros_like(state_scr)
        cnt_scr[...] = jnp.zeros_like(cnt_scr)
        x8_scr[...] = jnp.zeros_like(x8_scr)

    xa = xa_bf.astype(F32)

    def chunk_lo(p, hf):
        return hf * (SSM_WIDTH // 2) + p * LANES

    scans = (bu0_scr, bu1_scr)
    for p in range(2):
        for hf in range(2):
            lo = chunk_lo(p, hf)
            for b in range(nb):
                x8_scr[2 * p + hf, pl.ds(hf * nb + b, tl, stride=SUBLANES), :] = (
                    xa[b * tl:(b + 1) * tl, lo:lo + LANES])
        x8 = jnp.concatenate([x8_scr[2 * p], x8_scr[2 * p + 1]], axis=1).astype(BF16)
        scans[p][...] = _dot(x8, bd_ref[p])

    a_re = [are_ref[p] for p in range(2)]
    a_im = [aim_ref[p] for p in range(2)]
    s_re = [state_scr[p][:, :SSM_WIDTH] for p in range(2)]
    s_im = [state_scr[p][:, SSM_WIDTH:] for p in range(2)]
    s_b = SSM_WIDTH + 2 * SGU_WIDTH
    z = _gelu(_dot(xn, win_ref[:, SSM_WIDTH:s_b]))
    gates = []
    for t0 in range(0, tl, SCAN_BLOCK):
        rows_p = ([], [])
        for t in range(t0, t0 + SCAN_BLOCK):
            r = t * SUBLANES
            for p in range(2):
                bu_scr = scans[p]
                n_re = a_re[p] * s_re[p] - a_im[p] * s_im[p] + bu_scr[r:r + SUBLANES, 0:SSM_WIDTH]
                n_im = (a_re[p] * s_im[p] + a_im[p] * s_re[p]
                        + bu_scr[r:r + SUBLANES, SSM_WIDTH:2 * SSM_WIDTH])
                s_re[p], s_im[p] = n_re, n_im
                rows_p[p].append(jnp.concatenate([n_re, n_im], axis=1))
        r0 = t0 * SUBLANES
        for p in range(2):
            y8 = _dot(jnp.concatenate(rows_p[p], axis=0).astype(BF16), cd_ref[p])
            for hf in range(2):
                y8_scr[2 * p + hf, r0:r0 + SCAN_BLOCK * SUBLANES, :] = (
                    y8[:, hf * LANES:(hf + 1) * LANES])
        gates.append(_proj_gate(xn, win_ref, s_b + len(gates) * D_MODEL))

    sga = gates[0]

    y_chunks = {}
    for p in range(2):
        state_scr[p] = jnp.concatenate([s_re[p], s_im[p]], axis=1)
        for hf in range(2):
            y_chunks[(hf, p)] = jnp.concatenate(
                [y8_scr[2 * p + hf, pl.ds(hf * nb + b, tl, stride=SUBLANES), :]
                 for b in range(nb)], axis=0)
    y = jnp.concatenate([y_chunks[(hf, p)] for hf in range(2) for p in range(2)], axis=1)
    y = y + dskip_ref[...] * xa
    zs = _gelu(y)
    ya = zs * _sigmoid(_dot(zs.astype(BF16), wglu_ref[...]) + bglu_ref[...])
    _sgu(z, lng_ref, lnb_ref, ws_ref, bs_ref, yb_scr)
    pb = (gates[1] * _dot(yb_scr[...], wb_ref[...])).astype(BF16)
    pa = sga * _dot(ya.astype(BF16), wa_ref[...])
    merged = pa + pb.astype(F32)
    h = x + _dot(merged.astype(BF16), wout_ref[...])
    _pack_rows(h, h_ref)
    xn2 = _rmsnorm(h, g2_ref[...])
    _pack_rows(xn2 * xs_ref[0:1, 0:1], xn2_ref)

    logits = _dot(xn2.astype(BF16), wr_ref[...]) + br_ref[...]
    work = jnp.transpose(logits)[:N_EXPERTS]
    expert = lax.broadcasted_iota(jnp.int32, (N_EXPERTS, m), 0).astype(F32)
    vals, idxs = [], []
    member = jnp.zeros((N_EXPERTS, m), F32)
    for _ in range(TOP_K):
        mx = jnp.max(work, axis=0, keepdims=True)
        ix = jnp.min(jnp.where(work == mx, expert, float(N_EXPERTS)), axis=0, keepdims=True)
        hit = expert == ix
        member = jnp.where(hit, 1.0, member)
        work = jnp.where(hit, NEG_BIG, work)
        vals.append(mx)
        idxs.append(ix)
    exps = [jnp.exp(v - vals[0]) for v in vals]
    denom = exps[0] + exps[1] + exps[2] + exps[3]
    gates = [e / denom for e in exps]

    before = _dot(member.astype(BF16), upper_ref[...]) + cnt_scr[:, 0:1]
    new_cnt = cnt_scr[:, 0:1] + jnp.sum(member, axis=1, keepdims=True)
    cnt_scr[...] = jnp.broadcast_to(new_cnt, cnt_scr.shape)
    cnt_ref[...] = jnp.broadcast_to(new_cnt, cnt_ref.shape)
    ranks = [jnp.sum(jnp.where(expert == ix, before, 0.0), axis=0, keepdims=True) for ix in idxs]
    pad = jnp.zeros((ROUTE_ROWS - 3 * TOP_K, m), F32)
    routet_ref[...] = jnp.concatenate(idxs + gates + ranks + [pad], axis=0)


def _mix_call(x, *params):
    nb, seq, d = x.shape
    tl = TIME_TILE
    m = nb * tl
    n_tiles = seq // tl
    const = lambda shape: pl.BlockSpec(shape, lambda i: (0,) * len(shape))
    resident = lambda shape: pl.BlockSpec(shape, lambda i: (0,) * len(shape),
                                          pipeline_mode=pl.Buffered(1))
    tile = lambda width: pl.BlockSpec((nb, tl, width), lambda i: (0, i, 0))
    packed_rows = pl.BlockSpec((SUBROWS, m, LANES), lambda i: (0, i, 0))
    operands = (x,) + params
    in_specs = [tile(d)] + [resident(o.shape) for o in params]
    return pl.pallas_call(
        _mix_kernel,
        grid=(n_tiles,),
        in_specs=in_specs,
        out_specs=[packed_rows, packed_rows,
                   pl.BlockSpec((ROUTE_ROWS, m), lambda i: (0, i)), const((N_EXPERTS, LANES))],
        out_shape=[jax.ShapeDtypeStruct((SUBROWS, n_tiles * m, LANES), jnp.int32),
                   jax.ShapeDtypeStruct((SUBROWS, n_tiles * m, LANES), jnp.int32),
                   jax.ShapeDtypeStruct((ROUTE_ROWS, n_tiles * m), F32),
                   jax.ShapeDtypeStruct((N_EXPERTS, LANES), F32)],
        scratch_shapes=[pltpu.VMEM((SUBLANES * tl, 2 * SSM_WIDTH), F32),
                        pltpu.VMEM((SUBLANES * tl, 2 * SSM_WIDTH), F32),
                        pltpu.VMEM((4, SUBLANES * tl, LANES), F32),
                        pltpu.VMEM((4, SUBLANES * tl, LANES), F32),
                        pltpu.VMEM((m, SGU_WIDTH), BF16),
                        pltpu.VMEM((2, SUBLANES, 2 * SSM_WIDTH), F32),
                        pltpu.VMEM((N_EXPERTS, LANES), F32)],
        compiler_params=pltpu.CompilerParams(
            dimension_semantics=("arbitrary",), vmem_limit_bytes=VMEM_LIMIT),
        name="mix",
    )(*operands)


def _sc_mesh():
    return plsc.VectorSubcoreMesh(core_axis_name="c", subcore_axis_name="s")


def _worker_id():
    return lax.axis_index("s") * SC_CORES + lax.axis_index("c")


def _dispatch_body(src_hbm, idx_hbm, out_hbm, idx_v, *scratch):
    bufs = scratch[:DISPATCH_RING]
    sem_r = scratch[DISPATCH_RING:2 * DISPATCH_RING]
    sem_w = scratch[2 * DISPATCH_RING:]
    n_chunks = idx_v.shape[0] // TOP_K
    wid = _worker_id()
    base = wid * (n_chunks * LANES)
    pltpu.sync_copy(idx_hbm.at[wid], idx_v)

    def read(j, b):
        return pltpu.make_async_copy(src_hbm.at[pl.ds(base + j * LANES, LANES)], bufs[b], sem_r[b])

    def scatters(j, b):
        return [pltpu.make_async_copy(bufs[b], out_hbm.at[idx_v.at[j * TOP_K + k]], sem_w[b])
                for k in range(TOP_K)]

    ahead = DISPATCH_RING // 2
    for j in range(ahead):
        read(j, j).start()

    @pl.loop(0, n_chunks, step=DISPATCH_RING)
    def _(j0):
        for b in range(DISPATCH_RING):
            j = j0 + b
            freed = (b + ahead) % DISPATCH_RING
            read(j, b).wait()
            for c in scatters(j, b):
                c.start()

            @pl.when(j >= ahead)
            def _():
                for c in scatters(j - ahead, freed):
                    c.wait()

            @pl.when(j + ahead < n_chunks)
            def _():
                read(j + ahead, freed).start()

    for j in range(n_chunks - ahead, n_chunks):
        for c in scatters(j, j % DISPATCH_RING):
            c.wait()


def _sc_dispatch(src, idx, n_out):
    n_chunks = src.shape[0] // (SC_WORKERS * LANES)
    assert n_chunks % DISPATCH_RING == 0
    return pl.kernel(
        _dispatch_body,
        out_type=jax.ShapeDtypeStruct((n_out, LANES), jnp.int32),
        mesh=_sc_mesh(),
        scratch_types=([pltpu.VMEM((n_chunks * TOP_K, LANES), jnp.int32)]
                       + [pltpu.VMEM((LANES, LANES), jnp.int32)] * DISPATCH_RING
                       + [pltpu.SemaphoreType.DMA] * (2 * DISPATCH_RING)),
        name="sc_dispatch",
    )(src, idx)


def _gather_body(tab_hbm, idx_hbm, out_hbm, idx_v, buf0, buf1, sem0, sem1):
    n_chunks = idx_v.shape[0]
    wid = _worker_id()
    base = wid * (n_chunks * LANES)
    pltpu.sync_copy(idx_hbm.at[wid], idx_v)

    @pl.loop(0, n_chunks, step=2)
    def _(j):
        c0 = pltpu.async_copy(tab_hbm.at[idx_v.at[j]], buf0, sem0)
        c1 = pltpu.async_copy(tab_hbm.at[idx_v.at[j + 1]], buf1, sem1)
        c0.wait()
        pltpu.sync_copy(buf0, out_hbm.at[pl.ds(base + j * LANES, LANES)])
        c1.wait()
        pltpu.sync_copy(buf1, out_hbm.at[pl.ds(base + (j + 1) * LANES, LANES)])


def _sc_gather(tab, idx):
    n_chunks = idx.shape[1]
    return pl.kernel(
        _gather_body,
        out_type=jax.ShapeDtypeStruct((SC_WORKERS * n_chunks * LANES, LANES), jnp.int32),
        mesh=_sc_mesh(),
        scratch_types=[pltpu.VMEM((n_chunks, LANES), jnp.int32),
                       pltpu.VMEM((LANES, LANES), jnp.int32),
                       pltpu.VMEM((LANES, LANES), jnp.int32),
                       pltpu.SemaphoreType.DMA, pltpu.SemaphoreType.DMA],
        name="sc_gather",
    )(tab, idx)


def _slots_kernel(pstart_ref, rt_ref, gat_ref, dis_ref, *, n_rows):
    tb = rt_ref.shape[1]
    idx = rt_ref[0:TOP_K, :]
    dest = rt_ref[2 * TOP_K:3 * TOP_K, :]
    for e in range(N_EXPERTS):
        dest = dest + jnp.where(idx == float(e), pstart_ref[e].astype(F32), 0.0)
    dest = dest.astype(jnp.int32)
    for k in range(TOP_K):
        for s in range(SUBROWS):
            row = dest[k:k + 1, :] + s * n_rows
            gat_ref[k, s:s + 1, :] = row
            for c in range(tb // LANES):
                dis_ref[s, c, k:k + 1, :] = row[:, c * LANES:(c + 1) * LANES]


def _slots_call(pstart, route_t, n_rows, n_parts):
    t = route_t.shape[1]
    tp = t // n_parts
    tb = min(tp, SLOT_TILE)
    per_part = tp // tb
    grid_spec = pltpu.PrefetchScalarGridSpec(
        num_scalar_prefetch=1,
        grid=(t // tb,),
        in_specs=[pl.BlockSpec((ROUTE_ROWS, tb), lambda i, ps: (0, i))],
        out_specs=[pl.BlockSpec((None, TOP_K, SUBROWS, tb),
                                lambda i, ps: (i // per_part, 0, 0, i % per_part)),
                   pl.BlockSpec((SUBROWS, tb // LANES, TOP_K, LANES), lambda i, ps: (0, i, 0, 0))],
    )
    return pl.pallas_call(
        functools.partial(_slots_kernel, n_rows=n_rows),
        grid_spec=grid_spec,
        out_shape=[jax.ShapeDtypeStruct((n_parts, TOP_K, SUBROWS, tp), jnp.int32),
                   jax.ShapeDtypeStruct((SUBROWS, t // LANES, TOP_K, LANES), jnp.int32)],
        compiler_params=pltpu.CompilerParams(dimension_semantics=("arbitrary",)),
        name="slots",
    )(pstart, route_t)


def _expert_kernel(be_ref, bv_ref, nx_ref, sl_ref, x_ref, xs_ref, bg_ref, bu_ref, bd_ref,
                   wg_hbm, wu_hbm, wd_hbm, y_ref, stage, wg_scr, wu_scr, wd_scr, inv_scr, sems):
    i = pl.program_id(0)
    valid = bv_ref[i]
    first = jnp.logical_and(
        valid > 0, jnp.logical_or(i == 0, be_ref[i] != be_ref[jnp.maximum(i - 1, 0)]))

    def weight_copies(expert, slot):
        return [pltpu.make_async_copy(w.at[expert], stage.at[slot, j], sems.at[slot, j])
                for j, w in enumerate((wg_hbm, wu_hbm, wd_hbm))]

    @pl.when(i == 0)
    def _():
        for c in weight_copies(be_ref[0], sl_ref[0]):
            c.start()

    @pl.when(first)
    def _():
        slot = sl_ref[i]
        for c in weight_copies(be_ref[i], slot):
            c.wait()
        for j, scr in enumerate((wg_scr, wu_scr, wd_scr)):
            w = stage[slot, j].astype(BF16)
            amax = jnp.max(jnp.max(jnp.abs(w), axis=0, keepdims=True), axis=1, keepdims=True)
            scale = jnp.exp2(jnp.floor(jnp.log2(
                F8_TARGET / jnp.maximum(amax.astype(F32), TINY))))
            scr[...] = (w * scale.astype(BF16)).astype(F8)
            inv_scr[j] = jnp.broadcast_to(1.0 / scale, inv_scr.shape[1:])

        @pl.when(nx_ref[i] >= 0)
        def _():
            for c in weight_copies(nx_ref[i], 1 - slot):
                c.start()

    @pl.when(valid > 0)
    def _():
        lo, hi = _unpack_rows(x_ref)
        live = lax.broadcasted_iota(jnp.int32, (ROW_BLOCK, 1), 0) < valid
        x = jnp.where(live, jnp.concatenate(lo + hi, axis=1), 0.0).astype(BF16).astype(F8)
        inv_x = 1.0 / xs_ref[0:1, 0:1]
        cg = (inv_scr[0, 0:1, 0:1] * inv_x).astype(BF16)
        cl = (inv_scr[1, 0:1, 0:1] * inv_x * HID_SCALE).astype(BF16)
        g = _dot(x, wg_scr[...]).astype(BF16) * cg + bg_ref[...].astype(BF16)
        g = jnp.minimum(g, SWIGLU_LIMIT)
        l = _dot(x, wu_scr[...]).astype(BF16) * cl + (bu_ref[...] * HID_SCALE).astype(BF16)
        l = jnp.clip(l, -SWIGLU_LIMIT * HID_SCALE, SWIGLU_LIMIT * HID_SCALE) + HID_SCALE
        hid = g * l / (1.0 + jnp.exp2(g * (-SWIGLU_ALPHA * LOG2_E)))
        y = _dot(hid.astype(F8), wd_scr[...])
        _pack_rows(y * (inv_scr[2, 0:1, 0:1] * (1.0 / HID_SCALE)) + bd_ref[...], y_ref)

    @pl.when(valid <= 0)
    def _():
        y_ref[...] = jnp.zeros_like(y_ref)


def _expert_call(block_expert, block_valid, next_expert, slot, x_tab, x_scale,
                 w_gate, b_gate, w_up, b_up, w_down, b_down):
    d, f = w_gate.shape[-2:]
    assert d == f
    n_blocks = block_expert.shape[0]
    bspec = lambda width: pl.BlockSpec((None, 1, width), lambda i, be, bv, nx, sl: (be[i], 0, 0))
    rows = pl.BlockSpec((SUBROWS, ROW_BLOCK, LANES), lambda i, be, bv, nx, sl: (0, i, 0))
    hbm = pl.BlockSpec(memory_space=pl.ANY)
    grid_spec = pltpu.PrefetchScalarGridSpec(
        num_scalar_prefetch=4,
        grid=(n_blocks,),
        in_specs=[rows, pl.BlockSpec((SUBLANES, LANES), lambda i, be, bv, nx, sl: (0, 0)),
                  bspec(f), bspec(f), bspec(d), hbm, hbm, hbm],
        out_specs=rows,
        scratch_shapes=[pltpu.VMEM((2, 3, d, f), F32),
                        pltpu.VMEM((d, f), F8), pltpu.VMEM((d, f), F8), pltpu.VMEM((f, d), F8),
                        pltpu.VMEM((3, SUBLANES, LANES), F32), pltpu.SemaphoreType.DMA((2, 3))],
    )
    return pl.pallas_call(
        _expert_kernel,
        grid_spec=grid_spec,
        out_shape=jax.ShapeDtypeStruct(x_tab.shape, jnp.int32),
        compiler_params=pltpu.CompilerParams(
            dimension_semantics=("arbitrary",), vmem_limit_bytes=VMEM_LIMIT),
        name="expert",
    )(block_expert, block_valid, next_expert, slot, x_tab, x_scale, b_gate, b_up, b_down,
      w_gate, w_up, w_down)


def _combine_kernel(h_ref, routet_ref, gf_ref, ys_ref, *rest):
    out_ref = rest[-1]
    nb, tl, d = out_ref.shape
    m = nb * tl
    route = jnp.transpose(jnp.concatenate(
        [routet_ref[...], jnp.zeros((LANES - ROUTE_ROWS, m), F32)], axis=0))
    lo_acc, hi_acc = _unpack_rows(h_ref)
    for k in range(TOP_K):
        gate = route[:, TOP_K + k:TOP_K + k + 1]
        lo, hi = _unpack_rows(ys_ref.at[k])
        lo_acc = [a + gate * v for a, v in zip(lo_acc, lo)]
        hi_acc = [a + gate * v for a, v in zip(hi_acc, hi)]
    acc = jnp.concatenate(lo_acc + hi_acc, axis=1)
    out_ref[...] = _rmsnorm(acc, gf_ref[...]).reshape(nb, tl, d)


def _combine_call(h, route_t, g_final, y_slots, nb, seq, part, n_parts, prev):
    t, d = h.shape[1], D_MODEL
    tl = TIME_TILE
    m = nb * tl
    n = t // m // n_parts
    first = part * n
    in_specs = [pl.BlockSpec((SUBROWS, m, LANES), lambda i: (0, first + i, 0)),
                pl.BlockSpec((ROUTE_ROWS, m), lambda i: (0, first + i)),
                pl.BlockSpec((1, d), lambda i: (0, 0)),
                pl.BlockSpec((TOP_K, SUBROWS, m, LANES), lambda i: (0, 0, i, 0))]
    operands = [h, route_t, g_final, y_slots]
    aliases = {}
    if prev is not None:
        in_specs.append(pl.BlockSpec(memory_space=pl.ANY))
        operands.append(prev)
        aliases = {4: 0}
    return pl.pallas_call(
        _combine_kernel,
        grid=(n,),
        in_specs=in_specs,
        out_specs=pl.BlockSpec((nb, tl, d), lambda i: (0, first + i, 0)),
        out_shape=jax.ShapeDtypeStruct((nb, seq, d), F32),
        input_output_aliases=aliases,
        compiler_params=pltpu.CompilerParams(
            dimension_semantics=("arbitrary",), vmem_limit_bytes=VMEM_LIMIT),
        name="combine",
    )(*operands)


def _s5_operands(lam_re, lam_im, log_dt, b_re, b_im, c_re, c_im, nb):
    ns, pg = SSM_STATE, SSM_GROUP
    lam = lax.complex(lam_re.astype(F32), lam_im.astype(F32))
    dt = jnp.exp(log_dt.astype(F32))[:, None]
    lam_bar = jnp.exp(lam * dt)
    b_bar = ((lam_bar - 1.0) / lam)[..., None] * lax.complex(b_re.astype(F32), b_im.astype(F32))
    eye8 = jnp.eye(8, dtype=F32)
    split = lambda a: a.reshape((2, 2, 8) + a.shape[1:])

    def b_blocks(bb):
        return jnp.einsum('hpgnq,gk->phgqkn', split(bb), eye8).reshape(2, 2 * 8 * pg, 8 * ns)

    def c_blocks(cc):
        return jnp.einsum('hpgqn,gk->pgnhkq', split(cc), eye8).reshape(2, 8 * ns, 2 * 8 * pg)

    bd = jnp.concatenate([b_blocks(b_bar.real), b_blocks(b_bar.imag)], axis=2).astype(BF16)
    cd = jnp.concatenate([c_blocks(c_re.astype(F32)), -c_blocks(c_im.astype(F32))],
                         axis=1).astype(BF16)
    lam_rows = split(lam_bar).transpose(1, 0, 2, 3).reshape(2, 2, 8 * ns)
    lam_rows = jnp.repeat(lam_rows, nb, axis=1)
    return bd, cd, lam_rows.real, lam_rows.imag


def kernel(x, norm_mix_g, w_in, lam_re, lam_im, log_dt, b_re, b_im, c_re, c_im, d_skip, w_glu, b_glu, sgu_ln_g, sgu_ln_b, w_s, b_s, w_branch_a, w_branch_b, w_out, norm_moe_g, w_router, b_router, w_gate, b_gate, w_up, b_up, w_down, b_down, norm_final_g):
    nb, seq, d = x.shape
    assert d == D_MODEL and SUBLANES % nb == 0 and SUBLANES // nb == 2
    assert seq % TIME_TILE == 0 and norm_mix_g.shape[0] == 1
    assert (nb * seq * SUBROWS) % (SC_WORKERS * LANES) == 0
    assert (nb * seq * SUBROWS * TOP_K) % (SC_WORKERS * LANES * 2 * COMBINE_PARTS) == 0
    assert (seq // TIME_TILE) % COMBINE_PARTS == 0
    tl = TIME_TILE
    t = nb * seq
    row = lambda v: v.reshape(1, -1).astype(F32)

    bd, cd, a_re, a_im = _s5_operands(
        lam_re[0], lam_im[0], log_dt[0], b_re[0], b_im[0], c_re[0], c_im[0], nb)
    w_r = jnp.zeros((d, LANES), F32).at[:, :N_EXPERTS].set(w_router[0].astype(F32)).astype(BF16)
    b_r = jnp.full((1, LANES), NEG_BIG, F32).at[0, :N_EXPERTS].set(b_router[0].astype(F32))
    x_bound = math.sqrt(d) * jnp.max(jnp.abs(norm_moe_g[0].astype(F32)))
    x_scale = jnp.exp2(jnp.floor(jnp.log2(F8_TARGET / jnp.maximum(x_bound, TINY))))
    x_scale = jnp.full((SUBLANES, LANES), 1.0, F32) * x_scale
    m = nb * tl
    upper = (jnp.arange(m)[:, None] < jnp.arange(m)[None, :]).astype(BF16)
    h, xn2p, route_t, cnt = _mix_call(
        x, row(norm_mix_g[0]), w_in[0].astype(BF16), row(sgu_ln_g[0]), row(sgu_ln_b[0]),
        w_s[0].astype(F32), b_s[0].T.astype(F32), w_branch_b[0].astype(BF16),
        upper, bd, cd, a_re, a_im, row(d_skip[0]),
        w_glu[0].astype(BF16), row(b_glu[0]), w_branch_a[0].astype(BF16), w_out[0].astype(BF16),
        row(norm_moe_g[0]), x_scale, w_r, b_r)

    counts = cnt[:, 0].astype(jnp.int32)
    padded = (counts + ROW_BLOCK - 1) // ROW_BLOCK * ROW_BLOCK
    experts = jnp.arange(N_EXPERTS, dtype=jnp.int32)
    upto = experts[None, :] <= experts[:, None]
    cum = jnp.sum(jnp.where(upto, padded[None, :], 0), axis=1)
    pstart = cum - padded
    n_blocks = (t * TOP_K) // ROW_BLOCK + N_EXPERTS
    n_rows = n_blocks * ROW_BLOCK
    block_row0 = jnp.arange(n_blocks, dtype=jnp.int32) * ROW_BLOCK
    block_expert = jnp.minimum(
        jnp.sum((cum[None, :] <= block_row0[:, None]).astype(jnp.int32), axis=1), N_EXPERTS - 1)
    of_block = block_expert[:, None] == experts[None, :]
    pick = lambda table: jnp.sum(jnp.where(of_block, table[None, :], 0), axis=1)
    block_valid = jnp.clip(pick(counts) - (block_row0 - pick(pstart)), 0, ROW_BLOCK)
    present = counts > 0
    slot_e = (jnp.sum(jnp.where(upto, present[None, :].astype(jnp.int32), 0), axis=1) - 1) % 2
    later = jnp.logical_and(experts[None, :] > experts[:, None], present[None, :])
    next_e = jnp.min(jnp.where(later, experts[None, :], N_EXPERTS), axis=1)
    next_e = jnp.where(next_e == N_EXPERTS, -1, next_e)

    idx_parts, idx_dispatch = _slots_call(pstart, route_t, n_rows, COMBINE_PARTS)
    idx_dispatch = idx_dispatch.reshape(SC_WORKERS, -1, LANES)

    x_tab = _sc_dispatch(xn2p.reshape(SUBROWS * t, LANES), idx_dispatch, SUBROWS * n_rows)
    y_tab = _expert_call(
        block_expert, block_valid, pick(next_e), pick(slot_e),
        x_tab.reshape(SUBROWS, n_rows, LANES), x_scale,
        w_gate[0], b_gate[0][:, None, :], w_up[0], b_up[0][:, None, :],
        w_down[0], b_down[0][:, None, :])
    y_flat = y_tab.reshape(SUBROWS * n_rows, LANES)
    tp = t // COMBINE_PARTS
    out = None
    for q in range(COMBINE_PARTS):
        y_slots = _sc_gather(y_flat, idx_parts[q].reshape(SC_WORKERS, -1, LANES))
        out = _combine_call(h, route_t, row(norm_final_g),
                            y_slots.reshape(TOP_K, SUBROWS, tp, LANES), nb, seq,
                            q, COMBINE_PARTS, out)
    return out
```

```python
import functools
import math

import jax
import jax.numpy as jnp
from jax import lax
from jax.experimental import pallas as pl
from jax.experimental.pallas import tpu as pltpu
from jax.experimental.pallas import tpu_sc as plsc

F32 = jnp.float32
BF16 = jnp.bfloat16
F8 = jnp.float8_e4m3fn
F8_TARGET = 240.0
TINY = 1e-30

NORM_EPS = 1e-5
D_MODEL = 1024
SSM_WIDTH = 512
SSM_GROUP = 16
SSM_GROUPS = 32
SSM_STATE = 64
SGU_WIDTH = 512
SGU_BLOCK = 128
SGU_HEADS = 4
SGU_HEAD_DIM = 128
CHUNK = 64
N_EXPERTS = 32
TOP_K = 4
SWIGLU_ALPHA = 1.702
SWIGLU_LIMIT = 7.0
GATE_FLOOR = -128.0
LOG2_E = 1.4426950408889634
HID_SCALE = 4.0

LANES = 128
SUBLANES = 8
TIME_TILE = 128
SCAN_BLOCK = 64
ROW_BLOCK = 1024
VMEM_LIMIT = 56 * 1024 * 1024
NEG_BIG = -1e30
SC_CORES = 2
SC_SUBCORES = 16
SC_WORKERS = SC_CORES * SC_SUBCORES
SUBROWS = 4
COMBINE_PARTS = 4
SLOT_TILE = 4096
ROUTE_ROWS = 16
PACKED = D_MODEL // 2


def _pack_rows(x, out_ref):
    lo = lax.bitcast_convert_type(x[:, :PACKED].astype(BF16).astype(F32), jnp.int32)
    hi = lax.bitcast_convert_type(x[:, PACKED:].astype(BF16).astype(F32), jnp.int32)
    words = lax.shift_right_logical(lo, 16) | (hi & jnp.int32(-65536))
    for s in range(SUBROWS):
        out_ref[s] = words[:, s * LANES:(s + 1) * LANES]


def _unpack_rows(ref):
    lo, hi = [], []
    for s in range(SUBROWS):
        w = ref[s]
        lo.append(lax.bitcast_convert_type(lax.shift_left(w, 16), F32))
        hi.append(lax.bitcast_convert_type(w & jnp.int32(-65536), F32))
    return lo, hi


def _dot(a, b):
    return jnp.dot(a, b, preferred_element_type=F32)


def _sigmoid(x):
    return 1.0 / (1.0 + jnp.exp(-x))


def _gelu(x):
    return 0.5 * x * (1.0 + jnp.tanh(0.7978845608028654 * (x + 0.044715 * (x * x * x))))


def _rmsnorm(x, g):
    return x * lax.rsqrt(jnp.mean(x * x, axis=-1, keepdims=True) + NORM_EPS) * g


def _proj_xa(x, g_ref, w_ref):
    xn = _rmsnorm(x, g_ref[...]).astype(BF16)
    return xn, _dot(xn, w_ref[:, 0:SSM_WIDTH]).astype(BF16)


def _proj_gate(xn, w_ref, lo):
    return _sigmoid(_dot(xn, w_ref[:, lo:lo + D_MODEL]))


def _sgu(z, lng_ref, lnb_ref, ws_ref, bs_ref, yb_scr):
    m = z.shape[0]
    u = z[:, :SGU_WIDTH]
    v = z[:, SGU_WIDTH:]
    mu = jnp.mean(v, axis=-1, keepdims=True)
    vc = v - mu
    v = vc * lax.rsqrt(jnp.mean(vc * vc, axis=-1, keepdims=True) + NORM_EPS)
    v = (v * lng_ref[...] + lnb_ref[...]).astype(BF16)

    ri = lax.broadcasted_iota(jnp.int32, (SGU_BLOCK, SGU_BLOCK), 0) // CHUNK
    ci = lax.broadcasted_iota(jnp.int32, (SGU_BLOCK, SGU_BLOCK), 1) // CHUNK
    causal = ri >= ci
    for h in range(SGU_HEADS):
        wm = jnp.where(causal, ws_ref[h], 0.0).astype(BF16)
        bias = bs_ref[:, h:h + 1]
        lo = h * SGU_HEAD_DIM
        for blk in range(m // SGU_BLOCK):
            r0 = blk * SGU_BLOCK
            s = _dot(wm, v[r0:r0 + SGU_BLOCK, lo:lo + SGU_HEAD_DIM]) + bias
            yb_scr[r0:r0 + SGU_BLOCK, lo:lo + SGU_HEAD_DIM] = (
                u[r0:r0 + SGU_BLOCK, lo:lo + SGU_HEAD_DIM] * s).astype(BF16)


def _mix_kernel(x_ref, g1_ref, win_ref, lng_ref, lnb_ref, ws_ref, bs_ref, wb_ref,
                upper_ref, bd_ref, cd_ref,
                are_ref, aim_ref, dskip_ref, wglu_ref, bglu_ref, wa_ref, wout_ref,
                g2_ref, xs_ref, wr_ref, br_ref,
                h_ref, xn2_ref, routet_ref, cnt_ref,
                bu0_scr, bu1_scr, x8_scr, y8_scr, yb_scr, state_scr, cnt_scr):
    nb, tl, d = x_ref.shape
    m = nb * tl
    x = x_ref[...].reshape(m, d)
    xn, xa_bf = _proj_xa(x, g1_ref, win_ref)

    @pl.when(pl.program_id(0) == 0)
    def _():
        state_scr[...] = jnp.zeros_like(state_scr)
        cnt_scr[...] = jnp.zeros_like(cnt_scr)
        x8_scr[...] = jnp.zeros_like(x8_scr)

    xa = xa_bf.astype(F32)

    def chunk_lo(p, hf):
        return hf * (SSM_WIDTH // 2) + p * LANES

    scans = (bu0_scr, bu1_scr)
    for p in range(2):
        for hf in range(2):
            lo = chunk_lo(p, hf)
            for b in range(nb):
                x8_scr[2 * p + hf, pl.ds(hf * nb + b, tl, stride=SUBLANES), :] = (
                    xa[b * tl:(b + 1) * tl, lo:lo + LANES])
        x8 = jnp.concatenate([x8_scr[2 * p], x8_scr[2 * p + 1]], axis=1).astype(BF16)
        scans[p][...] = _dot(x8, bd_ref[p])

    a_re = [are_ref[p] for p in range(2)]
    a_im = [aim_ref[p] for p in range(2)]
    s_re = [state_scr[p][:, :SSM_WIDTH] for p in range(2)]
    s_im = [state_scr[p][:, SSM_WIDTH:] for p in range(2)]
    s_b = SSM_WIDTH + 2 * SGU_WIDTH
    z = _gelu(_dot(xn, win_ref[:, SSM_WIDTH:s_b]))
    gates = []
    for t0 in range(0, tl, SCAN_BLOCK):
        rows_p = ([], [])
        for t in range(t0, t0 + SCAN_BLOCK):
            r = t * SUBLANES
            for p in range(2):
                bu_scr = scans[p]
                n_re = a_re[p] * s_re[p] - a_im[p] * s_im[p] + bu_scr[r:r + SUBLANES, 0:SSM_WIDTH]
                n_im = (a_re[p] * s_im[p] + a_im[p] * s_re[p]
                        + bu_scr[r:r + SUBLANES, SSM_WIDTH:2 * SSM_WIDTH])
                s_re[p], s_im[p] = n_re, n_im
                rows_p[p].append(jnp.concatenate([n_re, n_im], axis=1))
        r0 = t0 * SUBLANES
        for p in range(2):
            y8 = _dot(jnp.concatenate(rows_p[p], axis=0).astype(BF16), cd_ref[p])
            for hf in range(2):
                y8_scr[2 * p + hf, r0:r0 + SCAN_BLOCK * SUBLANES, :] = (
                    y8[:, hf * LANES:(hf + 1) * LANES])
        gates.append(_proj_gate(xn, win_ref, s_b + len(gates) * D_MODEL))

    sga = gates[0]

    y_chunks = {}
    for p in range(2):
        state_scr[p] = jnp.concatenate([s_re[p], s_im[p]], axis=1)
        for hf in range(2):
            y_chunks[(hf, p)] = jnp.concatenate(
                [y8_scr[2 * p + hf, pl.ds(hf * nb + b, tl, stride=SUBLANES), :]
                 for b in range(nb)], axis=0)
    y = jnp.concatenate([y_chunks[(hf, p)] for hf in range(2) for p in range(2)], axis=1)
    y = y + dskip_ref[...] * xa
    zs = _gelu(y)
    ya = zs * _sigmoid(_dot(zs.astype(BF16), wglu_ref[...]) + bglu_ref[...])
    _sgu(z, lng_ref, lnb_ref, ws_ref, bs_ref, yb_scr)
    pb = (gates[1] * _dot(yb_scr[...], wb_ref[...])).astype(BF16)
    pa = sga * _dot(ya.astype(BF16), wa_ref[...])
    merged = pa + pb.astype(F32)
    h = x + _dot(merged.astype(BF16), wout_ref[...])
    _pack_rows(h, h_ref)
    xn2 = _rmsnorm(h, g2_ref[...])
    _pack_rows(xn2 * xs_ref[0:1, 0:1], xn2_ref)

    logits = _dot(xn2.astype(BF16), wr_ref[...]) + br_ref[...]
    work = jnp.transpose(logits)[:N_EXPERTS]
    expert = lax.broadcasted_iota(jnp.int32, (N_EXPERTS, m), 0).astype(F32)
    vals, idxs = [], []
    member = jnp.zeros((N_EXPERTS, m), F32)
    for _ in range(TOP_K):
        mx = jnp.max(work, axis=0, keepdims=True)
        ix = jnp.min(jnp.where(work == mx, expert, float(N_EXPERTS)), axis=0, keepdims=True)
        hit = expert == ix
        member = jnp.where(hit, 1.0, member)
        work = jnp.where(hit, NEG_BIG, work)
        vals.append(mx)
        idxs.append(ix)
    exps = [jnp.exp(v - vals[0]) for v in vals]
    denom = exps[0] + exps[1] + exps[2] + exps[3]
    gates = [e / denom for e in exps]

    before = _dot(member.astype(BF16), upper_ref[...]) + cnt_scr[:, 0:1]
    new_cnt = cnt_scr[:, 0:1] + jnp.sum(member, axis=1, keepdims=True)
    cnt_scr[...] = jnp.broadcast_to(new_cnt, cnt_scr.shape)
    cnt_ref[...] = jnp.broadcast_to(new_cnt, cnt_ref.shape)
    ranks = [jnp.sum(jnp.where(expert == ix, before, 0.0), axis=0, keepdims=True) for ix in idxs]
    pad = jnp.zeros((ROUTE_ROWS - 3 * TOP_K, m), F32)
    routet_ref[...] = jnp.concatenate(idxs + gates + ranks + [pad], axis=0)


def _mix_call(x, *params):
    nb, seq, d = x.shape
    tl = TIME_TILE
    m = nb * tl
    n_tiles = seq // tl
    const = lambda shape: pl.BlockSpec(shape, lambda i: (0,) * len(shape))
    resident = lambda shape: pl.BlockSpec(shape, lambda i: (0,) * len(shape),
                                          pipeline_mode=pl.Buffered(1))
    tile = lambda width: pl.BlockSpec((nb, tl, width), lambda i: (0, i, 0))
    packed_rows = pl.BlockSpec((SUBROWS, m, LANES), lambda i: (0, i, 0))
    operands = (x,) + params
    in_specs = [tile(d)] + [resident(o.shape) for o in params]
    return pl.pallas_call(
        _mix_kernel,
        grid=(n_tiles,),
        in_specs=in_specs,
        out_specs=[packed_rows, packed_rows,
                   pl.BlockSpec((ROUTE_ROWS, m), lambda i: (0, i)), const((N_EXPERTS, LANES))],
        out_shape=[jax.ShapeDtypeStruct((SUBROWS, n_tiles * m, LANES), jnp.int32),
                   jax.ShapeDtypeStruct((SUBROWS, n_tiles * m, LANES), jnp.int32),
                   jax.ShapeDtypeStruct((ROUTE_ROWS, n_tiles * m), F32),
                   jax.ShapeDtypeStruct((N_EXPERTS, LANES), F32)],
        scratch_shapes=[pltpu.VMEM((SUBLANES * tl, 2 * SSM_WIDTH), F32),
                        pltpu.VMEM((SUBLANES * tl, 2 * SSM_WIDTH), F32),
                        pltpu.VMEM((4, SUBLANES * tl, LANES), F32),
                        pltpu.VMEM((4, SUBLANES * tl, LANES), F32),
                        pltpu.VMEM((m, SGU_WIDTH), BF16),
                        pltpu.VMEM((2, SUBLANES, 2 * SSM_WIDTH), F32),
                        pltpu.VMEM((N_EXPERTS, LANES), F32)],
        compiler_params=pltpu.CompilerParams(
            dimension_semantics=("arbitrary",), vmem_limit_bytes=VMEM_LIMIT),
        name="mix",
    )(*operands)


def _sc_mesh():
    return plsc.VectorSubcoreMesh(core_axis_name="c", subcore_axis_name="s")


def _worker_id():
    return lax.axis_index("s") * SC_CORES + lax.axis_index("c")


def _dispatch_body(src_hbm, idx_hbm, out_hbm, idx_v, buf0, buf1, sem_r0, sem_r1, sem_w):
    n_chunks = idx_v.shape[0] // TOP_K
    wid = _worker_id()
    base = wid * (n_chunks * LANES)
    pltpu.sync_copy(idx_hbm.at[wid], idx_v)
    bufs = ((buf0, sem_r0), (buf1, sem_r1))

    def read(j, b):
        return pltpu.make_async_copy(src_hbm.at[pl.ds(base + j * LANES, LANES)], bufs[b][0],
                                     bufs[b][1])

    read(0, 0).start()

    @pl.loop(0, n_chunks, step=2)
    def _(j0):
        for b in range(2):
            j = j0 + b
            read(j, b).wait()

            @pl.when(j + 1 < n_chunks)
            def _():
                read(j + 1, 1 - b).start()

            copies = [pltpu.async_copy(bufs[b][0], out_hbm.at[idx_v.at[j * TOP_K + k]], sem_w)
                      for k in range(TOP_K)]
            for c in copies:
                c.wait()


def _sc_dispatch(src, idx, n_out):
    n_chunks = src.shape[0] // (SC_WORKERS * LANES)
    assert n_chunks % 2 == 0
    return pl.kernel(
        _dispatch_body,
        out_type=jax.ShapeDtypeStruct((n_out, LANES), jnp.int32),
        mesh=_sc_mesh(),
        scratch_types=[pltpu.VMEM((n_chunks * TOP_K, LANES), jnp.int32),
                       pltpu.VMEM((LANES, LANES), jnp.int32),
                       pltpu.VMEM((LANES, LANES), jnp.int32),
                       pltpu.SemaphoreType.DMA, pltpu.SemaphoreType.DMA,
                       pltpu.SemaphoreType.DMA],
        name="sc_dispatch",
    )(src, idx)


def _gather_body(tab_hbm, idx_hbm, out_hbm, idx_v, buf0, buf1, sem0, sem1):
    n_chunks = idx_v.shape[0]
    wid = _worker_id()
    base = wid * (n_chunks * LANES)
    pltpu.sync_copy(idx_hbm.at[wid], idx_v)

    @pl.loop(0, n_chunks, step=2)
    def _(j):
        c0 = pltpu.async_copy(tab_hbm.at[idx_v.at[j]], buf0, sem0)
        c1 = pltpu.async_copy(tab_hbm.at[idx_v.at[j + 1]], buf1, sem1)
        c0.wait()
        pltpu.sync_copy(buf0, out_hbm.at[pl.ds(base + j * LANES, LANES)])
        c1.wait()
        pltpu.sync_copy(buf1, out_hbm.at[pl.ds(base + (j + 1) * LANES, LANES)])


def _sc_gather(tab, idx):
    n_chunks = idx.shape[1]
    return pl.kernel(
        _gather_body,
        out_type=jax.ShapeDtypeStruct((SC_WORKERS * n_chunks * LANES, LANES), jnp.int32),
        mesh=_sc_mesh(),
        scratch_types=[pltpu.VMEM((n_chunks, LANES), jnp.int32),
                       pltpu.VMEM((LANES, LANES), jnp.int32),
                       pltpu.VMEM((LANES, LANES), jnp.int32),
                       pltpu.SemaphoreType.DMA, pltpu.SemaphoreType.DMA],
        name="sc_gather",
    )(tab, idx)


def _slots_kernel(pstart_ref, rt_ref, gat_ref, dis_ref, *, n_rows):
    tb = rt_ref.shape[1]
    idx = rt_ref[0:TOP_K, :]
    dest = rt_ref[2 * TOP_K:3 * TOP_K, :]
    for e in range(N_EXPERTS):
        dest = dest + jnp.where(idx == float(e), pstart_ref[e].astype(F32), 0.0)
    dest = dest.astype(jnp.int32)
    for k in range(TOP_K):
        for s in range(SUBROWS):
            row = dest[k:k + 1, :] + s * n_rows
            gat_ref[k, s:s + 1, :] = row
            for c in range(tb // LANES):
                dis_ref[s, c, k:k + 1, :] = row[:, c * LANES:(c + 1) * LANES]


def _slots_call(pstart, route_t, n_rows, n_parts):
    t = route_t.shape[1]
    tp = t // n_parts
    tb = min(tp, SLOT_TILE)
    per_part = tp // tb
    grid_spec = pltpu.PrefetchScalarGridSpec(
        num_scalar_prefetch=1,
        grid=(t // tb,),
        in_specs=[pl.BlockSpec((ROUTE_ROWS, tb), lambda i, ps: (0, i))],
        out_specs=[pl.BlockSpec((None, TOP_K, SUBROWS, tb),
                                lambda i, ps: (i // per_part, 0, 0, i % per_part)),
                   pl.BlockSpec((SUBROWS, tb // LANES, TOP_K, LANES), lambda i, ps: (0, i, 0, 0))],
    )
    return pl.pallas_call(
        functools.partial(_slots_kernel, n_rows=n_rows),
        grid_spec=grid_spec,
        out_shape=[jax.ShapeDtypeStruct((n_parts, TOP_K, SUBROWS, tp), jnp.int32),
                   jax.ShapeDtypeStruct((SUBROWS, t // LANES, TOP_K, LANES), jnp.int32)],
        compiler_params=pltpu.CompilerParams(dimension_semantics=("arbitrary",)),
        name="slots",
    )(pstart, route_t)


def _expert_kernel(be_ref, bv_ref, nx_ref, sl_ref, x_ref, xs_ref, bg_ref, bu_ref, bd_ref,
                   wg_hbm, wu_hbm, wd_hbm, y_ref, stage, wg_scr, wu_scr, wd_scr, inv_scr, sems):
    i = pl.program_id(0)
    valid = bv_ref[i]
    first = jnp.logical_and(
        valid > 0, jnp.logical_or(i == 0, be_ref[i] != be_ref[jnp.maximum(i - 1, 0)]))

    def weight_copies(expert, slot):
        return [pltpu.make_async_copy(w.at[expert], stage.at[slot, j], sems.at[slot, j])
                for j, w in enumerate((wg_hbm, wu_hbm, wd_hbm))]

    @pl.when(i == 0)
    def _():
        for c in weight_copies(be_ref[0], sl_ref[0]):
            c.start()

    @pl.when(first)
    def _():
        slot = sl_ref[i]
        for c in weight_copies(be_ref[i], slot):
            c.wait()
        for j, scr in enumerate((wg_scr, wu_scr, wd_scr)):
            w = stage[slot, j].astype(BF16)
            amax = jnp.max(jnp.max(jnp.abs(w), axis=0, keepdims=True), axis=1, keepdims=True)
            scale = jnp.exp2(jnp.floor(jnp.log2(
                F8_TARGET / jnp.maximum(amax.astype(F32), TINY))))
            scr[...] = (w * scale.astype(BF16)).astype(F8)
            inv_scr[j] = jnp.broadcast_to(1.0 / scale, inv_scr.shape[1:])

        @pl.when(nx_ref[i] >= 0)
        def _():
            for c in weight_copies(nx_ref[i], 1 - slot):
                c.start()

    @pl.when(valid > 0)
    def _():
        lo, hi = _unpack_rows(x_ref)
        live = lax.broadcasted_iota(jnp.int32, (ROW_BLOCK, 1), 0) < valid
        x = jnp.where(live, jnp.concatenate(lo + hi, axis=1), 0.0).astype(BF16).astype(F8)
        inv_x = 1.0 / xs_ref[0:1, 0:1]
        cg = (inv_scr[0, 0:1, 0:1] * inv_x).astype(BF16)
        cl = (inv_scr[1, 0:1, 0:1] * inv_x * HID_SCALE).astype(BF16)
        g = _dot(x, wg_scr[...]).astype(BF16) * cg + bg_ref[...].astype(BF16)
        g = jnp.clip(g, GATE_FLOOR, SWIGLU_LIMIT)
        l = _dot(x, wu_scr[...]).astype(BF16) * cl + (bu_ref[...] * HID_SCALE).astype(BF16)
        l = jnp.clip(l, -SWIGLU_LIMIT * HID_SCALE, SWIGLU_LIMIT * HID_SCALE) + HID_SCALE
        hid = g * l / (1.0 + jnp.exp2(g * (-SWIGLU_ALPHA * LOG2_E)))
        y = _dot(hid.astype(F8), wd_scr[...])
        _pack_rows(y * (inv_scr[2, 0:1, 0:1] * (1.0 / HID_SCALE)) + bd_ref[...], y_ref)

    @pl.when(valid <= 0)
    def _():
        y_ref[...] = jnp.zeros_like(y_ref)


def _expert_call(block_expert, block_valid, next_expert, slot, x_tab, x_scale,
                 w_gate, b_gate, w_up, b_up, w_down, b_down):
    d, f = w_gate.shape[-2:]
    assert d == f
    n_blocks = block_expert.shape[0]
    bspec = lambda width: pl.BlockSpec((None, 1, width), lambda i, be, bv, nx, sl: (be[i], 0, 0))
    rows = pl.BlockSpec((SUBROWS, ROW_BLOCK, LANES), lambda i, be, bv, nx, sl: (0, i, 0))
    hbm = pl.BlockSpec(memory_space=pl.ANY)
    grid_spec = pltpu.PrefetchScalarGridSpec(
        num_scalar_prefetch=4,
        grid=(n_blocks,),
        in_specs=[rows, pl.BlockSpec((SUBLANES, LANES), lambda i, be, bv, nx, sl: (0, 0)),
                  bspec(f), bspec(f), bspec(d), hbm, hbm, hbm],
        out_specs=rows,
        scratch_shapes=[pltpu.VMEM((2, 3, d, f), F32),
                        pltpu.VMEM((d, f), F8), pltpu.VMEM((d, f), F8), pltpu.VMEM((f, d), F8),
                        pltpu.VMEM((3, SUBLANES, LANES), F32), pltpu.SemaphoreType.DMA((2, 3))],
    )
    return pl.pallas_call(
        _expert_kernel,
        grid_spec=grid_spec,
        out_shape=jax.ShapeDtypeStruct(x_tab.shape, jnp.int32),
        compiler_params=pltpu.CompilerParams(
            dimension_semantics=("arbitrary",), vmem_limit_bytes=VMEM_LIMIT),
        name="expert",
    )(block_expert, block_valid, next_expert, slot, x_tab, x_scale, b_gate, b_up, b_down,
      w_gate, w_up, w_down)


def _combine_kernel(h_ref, routet_ref, gf_ref, ys_ref, *rest):
    out_ref = rest[-1]
    nb, tl, d = out_ref.shape
    m = nb * tl
    route = jnp.transpose(jnp.concatenate(
        [routet_ref[...], jnp.zeros((LANES - ROUTE_ROWS, m), F32)], axis=0))
    lo_acc, hi_acc = _unpack_rows(h_ref)
    for k in range(TOP_K):
        gate = route[:, TOP_K + k:TOP_K + k + 1]
        lo, hi = _unpack_rows(ys_ref.at[k])
        lo_acc = [a + gate * v for a, v in zip(lo_acc, lo)]
        hi_acc = [a + gate * v for a, v in zip(hi_acc, hi)]
    acc = jnp.concatenate(lo_acc + hi_acc, axis=1)
    out_ref[...] = _rmsnorm(acc, gf_ref[...]).reshape(nb, tl, d)


def _combine_call(h, route_t, g_final, y_slots, nb, seq, part, n_parts, prev):
    t, d = h.shape[1], D_MODEL
    tl = TIME_TILE
    m = nb * tl
    n = t // m // n_parts
    first = part * n
    in_specs = [pl.BlockSpec((SUBROWS, m, LANES), lambda i: (0, first + i, 0)),
                pl.BlockSpec((ROUTE_ROWS, m), lambda i: (0, first + i)),
                pl.BlockSpec((1, d), lambda i: (0, 0)),
                pl.BlockSpec((TOP_K, SUBROWS, m, LANES), lambda i: (0, 0, i, 0))]
    operands = [h, route_t, g_final, y_slots]
    aliases = {}
    if prev is not None:
        in_specs.append(pl.BlockSpec(memory_space=pl.ANY))
        operands.append(prev)
        aliases = {4: 0}
    return pl.pallas_call(
        _combine_kernel,
        grid=(n,),
        in_specs=in_specs,
        out_specs=pl.BlockSpec((nb, tl, d), lambda i: (0, first + i, 0)),
        out_shape=jax.ShapeDtypeStruct((nb, seq, d), F32),
        input_output_aliases=aliases,
        compiler_params=pltpu.CompilerParams(
            dimension_semantics=("arbitrary",), vmem_limit_bytes=VMEM_LIMIT),
        name="combine",
    )(*operands)


def _s5_operands(lam_re, lam_im, log_dt, b_re, b_im, c_re, c_im, nb):
    ns, pg = SSM_STATE, SSM_GROUP
    lam = lax.complex(lam_re.astype(F32), lam_im.astype(F32))
    dt = jnp.exp(log_dt.astype(F32))[:, None]
    lam_bar = jnp.exp(lam * dt)
    b_bar = ((lam_bar - 1.0) / lam)[..., None] * lax.complex(b_re.astype(F32), b_im.astype(F32))
    eye8 = jnp.eye(8, dtype=F32)
    split = lambda a: a.reshape((2, 2, 8) + a.shape[1:])

    def b_blocks(bb):
        return jnp.einsum('hpgnq,gk->phgqkn', split(bb), eye8).reshape(2, 2 * 8 * pg, 8 * ns)

    def c_blocks(cc):
        return jnp.einsum('hpgqn,gk->pgnhkq', split(cc), eye8).reshape(2, 8 * ns, 2 * 8 * pg)

    bd = jnp.concatenate([b_blocks(b_bar.real), b_blocks(b_bar.imag)], axis=2).astype(BF16)
    cd = jnp.concatenate([c_blocks(c_re.astype(F32)), -c_blocks(c_im.astype(F32))],
                         axis=1).astype(BF16)
    lam_rows = split(lam_bar).transpose(1, 0, 2, 3).reshape(2, 2, 8 * ns)
    lam_rows = jnp.repeat(lam_rows, nb, axis=1)
    return bd, cd, lam_rows.real, lam_rows.imag


def kernel(x, norm_mix_g, w_in, lam_re, lam_im, log_dt, b_re, b_im, c_re, c_im, d_skip, w_glu, b_glu, sgu_ln_g, sgu_ln_b, w_s, b_s, w_branch_a, w_branch_b, w_out, norm_moe_g, w_router, b_router, w_gate, b_gate, w_up, b_up, w_down, b_down, norm_final_g):
    nb, seq, d = x.shape
    assert d == D_MODEL and SUBLANES % nb == 0 and SUBLANES // nb == 2
    assert seq % TIME_TILE == 0 and norm_mix_g.shape[0] == 1
    assert (nb * seq * SUBROWS) % (SC_WORKERS * LANES) == 0
    assert (nb * seq * SUBROWS * TOP_K) % (SC_WORKERS * LANES * 2 * COMBINE_PARTS) == 0
    assert (seq // TIME_TILE) % COMBINE_PARTS == 0
    tl = TIME_TILE
    t = nb * seq
    row = lambda v: v.reshape(1, -1).astype(F32)

    bd, cd, a_re, a_im = _s5_operands(
        lam_re[0], lam_im[0], log_dt[0], b_re[0], b_im[0], c_re[0], c_im[0], nb)
    w_r = jnp.zeros((d, LANES), F32).at[:, :N_EXPERTS].set(w_router[0].astype(F32)).astype(BF16)
    b_r = jnp.full((1, LANES), NEG_BIG, F32).at[0, :N_EXPERTS].set(b_router[0].astype(F32))
    x_bound = math.sqrt(d) * jnp.max(jnp.abs(norm_moe_g[0].astype(F32)))
    x_scale = jnp.exp2(jnp.floor(jnp.log2(F8_TARGET / jnp.maximum(x_bound, TINY))))
    x_scale = jnp.full((SUBLANES, LANES), 1.0, F32) * x_scale
    m = nb * tl
    upper = (jnp.arange(m)[:, None] < jnp.arange(m)[None, :]).astype(BF16)
    h, xn2p, route_t, cnt = _mix_call(
        x, row(norm_mix_g[0]), w_in[0].astype(BF16), row(sgu_ln_g[0]), row(sgu_ln_b[0]),
        w_s[0].astype(F32), b_s[0].T.astype(F32), w_branch_b[0].astype(BF16),
        upper, bd, cd, a_re, a_im, row(d_skip[0]),
        w_glu[0].astype(BF16), row(b_glu[0]), w_branch_a[0].astype(BF16), w_out[0].astype(BF16),
        row(norm_moe_g[0]), x_scale, w_r, b_r)

    counts = cnt[:, 0].astype(jnp.int32)
    padded = (counts + ROW_BLOCK - 1) // ROW_BLOCK * ROW_BLOCK
    experts = jnp.arange(N_EXPERTS, dtype=jnp.int32)
    upto = experts[None, :] <= experts[:, None]
    cum = jnp.sum(jnp.where(upto, padded[None, :], 0), axis=1)
    pstart = cum - padded
    n_blocks = (t * TOP_K) // ROW_BLOCK + N_EXPERTS
    n_rows = n_blocks * ROW_BLOCK
    block_row0 = jnp.arange(n_blocks, dtype=jnp.int32) * ROW_BLOCK
    block_expert = jnp.minimum(
        jnp.sum((cum[None, :] <= block_row0[:, None]).astype(jnp.int32), axis=1), N_EXPERTS - 1)
    of_block = block_expert[:, None] == experts[None, :]
    pick = lambda table: jnp.sum(jnp.where(of_block, table[None, :], 0), axis=1)
    block_valid = jnp.clip(pick(counts) - (block_row0 - pick(pstart)), 0, ROW_BLOCK)
    present = counts > 0
    slot_e = (jnp.sum(jnp.where(upto, present[None, :].astype(jnp.int32), 0), axis=1) - 1) % 2
    later = jnp.logical_and(experts[None, :] > experts[:, None], present[None, :])
    next_e = jnp.min(jnp.where(later, experts[None, :], N_EXPERTS), axis=1)
    next_e = jnp.where(next_e == N_EXPERTS, -1, next_e)

    idx_parts, idx_dispatch = _slots_call(pstart, route_t, n_rows, COMBINE_PARTS)
    idx_dispatch = idx_dispatch.reshape(SC_WORKERS, -1, LANES)

    x_tab = _sc_dispatch(xn2p.reshape(SUBROWS * t, LANES), idx_dispatch, SUBROWS * n_rows)
    y_tab = _expert_call(
        block_expert, block_valid, pick(next_e), pick(slot_e),
        x_tab.reshape(SUBROWS, n_rows, LANES), x_scale,
        w_gate[0], b_gate[0][:, None, :], w_up[0], b_up[0][:, None, :],
        w_down[0], b_down[0][:, None, :])
    y_flat = y_tab.reshape(SUBROWS * n_rows, LANES)
    tp = t // COMBINE_PARTS
    out = None
    for q in range(COMBINE_PARTS):
        y_slots = _sc_gather(y_flat, idx_parts[q].reshape(SC_WORKERS, -1, LANES))
        out = _combine_call(h, route_t, row(norm_final_g),
                            y_slots.reshape(TOP_K, SUBROWS, tp, LANES), nb, seq,
                            q, COMBINE_PARTS, out)
    return out
```

```python
import functools
import math

import jax
import jax.numpy as jnp
from jax import lax
from jax.experimental import pallas as pl
from jax.experimental.pallas import tpu as pltpu
from jax.experimental.pallas import tpu_sc as plsc

F32 = jnp.float32
BF16 = jnp.bfloat16
F8 = jnp.float8_e4m3fn
F8_TARGET = 240.0
TINY = 1e-30

NORM_EPS = 1e-5
D_MODEL = 1024
SSM_WIDTH = 512
SSM_GROUP = 16
SSM_GROUPS = 32
SSM_STATE = 64
SGU_WIDTH = 512
SGU_BLOCK = 128
SGU_HEADS = 4
SGU_HEAD_DIM = 128
CHUNK = 64
N_EXPERTS = 32
TOP_K = 4
SWIGLU_ALPHA = 1.702
SWIGLU_LIMIT = 7.0
GATE_FLOOR = -128.0
LOG2_E = 1.4426950408889634
HID_SCALE = 4.0

LANES = 128
SUBLANES = 8
TIME_TILE = 128
SCAN_BLOCK = 64
ROW_BLOCK = 1024
VMEM_LIMIT = 56 * 1024 * 1024
NEG_BIG = -1e30
SC_CORES = 2
SC_SUBCORES = 16
SC_WORKERS = SC_CORES * SC_SUBCORES
SUBROWS = 4
COMBINE_PARTS = 4
SLOT_TILE = 4096
ROUTE_ROWS = 16
PACKED = D_MODEL // 2


def _pack_rows(x, out_ref):
    lo = lax.bitcast_convert_type(x[:, :PACKED].astype(BF16).astype(F32), jnp.int32)
    hi = lax.bitcast_convert_type(x[:, PACKED:].astype(BF16).astype(F32), jnp.int32)
    words = lax.shift_right_logical(lo, 16) | (hi & jnp.int32(-65536))
    for s in range(SUBROWS):
        out_ref[s] = words[:, s * LANES:(s + 1) * LANES]


def _unpack_rows(ref):
    lo, hi = [], []
    for s in range(SUBROWS):
        w = ref[s]
        lo.append(lax.bitcast_convert_type(lax.shift_left(w, 16), F32))
        hi.append(lax.bitcast_convert_type(w & jnp.int32(-65536), F32))
    return lo, hi


def _dot(a, b):
    return jnp.dot(a, b, preferred_element_type=F32)


def _sigmoid(x):
    return 1.0 / (1.0 + jnp.exp(-x))


def _gelu(x):
    return 0.5 * x * (1.0 + jnp.tanh(0.7978845608028654 * (x + 0.044715 * (x * x * x))))


def _rmsnorm(x, g):
    return x * lax.rsqrt(jnp.mean(x * x, axis=-1, keepdims=True) + NORM_EPS) * g


def _proj_xa(x, g_ref, w_ref):
    xn = _rmsnorm(x, g_ref[...]).astype(BF16)
    return xn, _dot(xn, w_ref[:, 0:SSM_WIDTH]).astype(BF16)


def _proj_gate(xn, w_ref, lo):
    return _sigmoid(_dot(xn, w_ref[:, lo:lo + D_MODEL]))


def _sgu(z, lng_ref, lnb_ref, ws_ref, bs_ref, yb_scr):
    m = z.shape[0]
    u = z[:, :SGU_WIDTH]
    v = z[:, SGU_WIDTH:]
    mu = jnp.mean(v, axis=-1, keepdims=True)
    vc = v - mu
    v = vc * lax.rsqrt(jnp.mean(vc * vc, axis=-1, keepdims=True) + NORM_EPS)
    v = (v * lng_ref[...] + lnb_ref[...]).astype(BF16)

    ri = lax.broadcasted_iota(jnp.int32, (SGU_BLOCK, SGU_BLOCK), 0) // CHUNK
    ci = lax.broadcasted_iota(jnp.int32, (SGU_BLOCK, SGU_BLOCK), 1) // CHUNK
    causal = ri >= ci
    for h in range(SGU_HEADS):
        wm = jnp.where(causal, ws_ref[h], 0.0).astype(BF16)
        bias = bs_ref[:, h:h + 1]
        lo = h * SGU_HEAD_DIM
        for blk in range(m // SGU_BLOCK):
            r0 = blk * SGU_BLOCK
            s = _dot(wm, v[r0:r0 + SGU_BLOCK, lo:lo + SGU_HEAD_DIM]) + bias
            yb_scr[r0:r0 + SGU_BLOCK, lo:lo + SGU_HEAD_DIM] = (
                u[r0:r0 + SGU_BLOCK, lo:lo + SGU_HEAD_DIM] * s).astype(BF16)


def _mix_kernel(x_ref, g1_ref, win_ref, lng_ref, lnb_ref, ws_ref, bs_ref, wb_ref,
                upper_ref, bd_ref, cd_ref,
                are_ref, aim_ref, dskip_ref, wglu_ref, bglu_ref, wa_ref, wout_ref,
                g2_ref, xs_ref, wr_ref, br_ref,
                h_ref, xn2_ref, routet_ref, cnt_ref,
                bu0_scr, bu1_scr, x8_scr, y8_scr, yb_scr, state_scr, cnt_scr):
    nb, tl, d = x_ref.shape
    m = nb * tl
    x = x_ref[...].reshape(m, d)
    xn, xa_bf = _proj_xa(x, g1_ref, win_ref)

    @pl.when(pl.program_id(0) == 0)
    def _():
        state_scr[...] = jnp.zeros_like(state_scr)
        cnt_scr[...] = jnp.zeros_like(cnt_scr)
        x8_scr[...] = jnp.zeros_like(x8_scr)

    xa = xa_bf.astype(F32)

    def chunk_lo(p, hf):
        return hf * (SSM_WIDTH // 2) + p * LANES

    scans = (bu0_scr, bu1_scr)
    for p in range(2):
        for hf in range(2):
            lo = chunk_lo(p, hf)
            for b in range(nb):
                x8_scr[2 * p + hf, pl.ds(hf * nb + b, tl, stride=SUBLANES), :] = (
                    xa[b * tl:(b + 1) * tl, lo:lo + LANES])
        x8 = jnp.concatenate([x8_scr[2 * p], x8_scr[2 * p + 1]], axis=1).astype(BF16)
        scans[p][...] = _dot(x8, bd_ref[p])

    a_re = [are_ref[p] for p in range(2)]
    a_im = [aim_ref[p] for p in range(2)]
    s_re = [state_scr[p][:, :SSM_WIDTH] for p in range(2)]
    s_im = [state_scr[p][:, SSM_WIDTH:] for p in range(2)]
    s_b = SSM_WIDTH + 2 * SGU_WIDTH
    z = _gelu(_dot(xn, win_ref[:, SSM_WIDTH:s_b]))
    gates = []
    for t0 in range(0, tl, SCAN_BLOCK):
        rows_p = ([], [])
        for t in range(t0, t0 + SCAN_BLOCK):
            r = t * SUBLANES
            for p in range(2):
                bu_scr = scans[p]
                n_re = a_re[p] * s_re[p] - a_im[p] * s_im[p] + bu_scr[r:r + SUBLANES, 0:SSM_WIDTH]
                n_im = (a_re[p] * s_im[p] + a_im[p] * s_re[p]
                        + bu_scr[r:r + SUBLANES, SSM_WIDTH:2 * SSM_WIDTH])
                s_re[p], s_im[p] = n_re, n_im
                rows_p[p].append(jnp.concatenate([n_re, n_im], axis=1))
        r0 = t0 * SUBLANES
        for p in range(2):
            y8 = _dot(jnp.concatenate(rows_p[p], axis=0).astype(BF16), cd_ref[p])
            for hf in range(2):
                y8_scr[2 * p + hf, r0:r0 + SCAN_BLOCK * SUBLANES, :] = (
                    y8[:, hf * LANES:(hf + 1) * LANES])
        gates.append(_proj_gate(xn, win_ref, s_b + len(gates) * D_MODEL))

    sga = gates[0]

    y_chunks = {}
    for p in range(2):
        state_scr[p] = jnp.concatenate([s_re[p], s_im[p]], axis=1)
        for hf in range(2):
            y_chunks[(hf, p)] = jnp.concatenate(
                [y8_scr[2 * p + hf, pl.ds(hf * nb + b, tl, stride=SUBLANES), :]
                 for b in range(nb)], axis=0)
    y = jnp.concatenate([y_chunks[(hf, p)] for hf in range(2) for p in range(2)], axis=1)
    y = y + dskip_ref[...] * xa
    zs = _gelu(y)
    ya = zs * _sigmoid(_dot(zs.astype(BF16), wglu_ref[...]) + bglu_ref[...])
    _sgu(z, lng_ref, lnb_ref, ws_ref, bs_ref, yb_scr)
    pb = (gates[1] * _dot(yb_scr[...], wb_ref[...])).astype(BF16)
    pa = sga * _dot(ya.astype(BF16), wa_ref[...])
    merged = pa + pb.astype(F32)
    h = x + _dot(merged.astype(BF16), wout_ref[...])
    _pack_rows(h, h_ref)
    xn2 = _rmsnorm(h, g2_ref[...])
    _pack_rows(xn2 * xs_ref[0:1, 0:1], xn2_ref)

    logits = _dot(xn2.astype(BF16), wr_ref[...]) + br_ref[...]
    work = jnp.transpose(logits)[:N_EXPERTS]
    expert = lax.broadcasted_iota(jnp.int32, (N_EXPERTS, m), 0).astype(F32)
    vals, idxs = [], []
    member = jnp.zeros((N_EXPERTS, m), F32)
    for _ in range(TOP_K):
        mx = jnp.max(work, axis=0, keepdims=True)
        ix = jnp.min(jnp.where(work == mx, expert, float(N_EXPERTS)), axis=0, keepdims=True)
        hit = expert == ix
        member = jnp.where(hit, 1.0, member)
        work = jnp.where(hit, NEG_BIG, work)
        vals.append(mx)
        idxs.append(ix)
    exps = [jnp.exp(v - vals[0]) for v in vals]
    denom = exps[0] + exps[1] + exps[2] + exps[3]
    gates = [e / denom for e in exps]

    before = _dot(member.astype(BF16), upper_ref[...]) + cnt_scr[:, 0:1]
    new_cnt = cnt_scr[:, 0:1] + jnp.sum(member, axis=1, keepdims=True)
    cnt_scr[...] = jnp.broadcast_to(new_cnt, cnt_scr.shape)
    cnt_ref[...] = jnp.broadcast_to(new_cnt, cnt_ref.shape)
    ranks = [jnp.sum(jnp.where(expert == ix, before, 0.0), axis=0, keepdims=True) for ix in idxs]
    pad = jnp.zeros((ROUTE_ROWS - 3 * TOP_K, m), F32)
    routet_ref[...] = jnp.concatenate(idxs + gates + ranks + [pad], axis=0)


def _mix_call(x, *params):
    nb, seq, d = x.shape
    tl = TIME_TILE
    m = nb * tl
    n_tiles = seq // tl
    const = lambda shape: pl.BlockSpec(shape, lambda i: (0,) * len(shape))
    resident = lambda shape: pl.BlockSpec(shape, lambda i: (0,) * len(shape),
                                          pipeline_mode=pl.Buffered(1))
    tile = lambda width: pl.BlockSpec((nb, tl, width), lambda i: (0, i, 0))
    packed_rows = pl.BlockSpec((SUBROWS, m, LANES), lambda i: (0, i, 0))
    operands = (x,) + params
    in_specs = [tile(d)] + [resident(o.shape) for o in params]
    return pl.pallas_call(
        _mix_kernel,
        grid=(n_tiles,),
        in_specs=in_specs,
        out_specs=[packed_rows, packed_rows,
                   pl.BlockSpec((ROUTE_ROWS, m), lambda i: (0, i)), const((N_EXPERTS, LANES))],
        out_shape=[jax.ShapeDtypeStruct((SUBROWS, n_tiles * m, LANES), jnp.int32),
                   jax.ShapeDtypeStruct((SUBROWS, n_tiles * m, LANES), jnp.int32),
                   jax.ShapeDtypeStruct((ROUTE_ROWS, n_tiles * m), F32),
                   jax.ShapeDtypeStruct((N_EXPERTS, LANES), F32)],
        scratch_shapes=[pltpu.VMEM((SUBLANES * tl, 2 * SSM_WIDTH), F32),
                        pltpu.VMEM((SUBLANES * tl, 2 * SSM_WIDTH), F32),
                        pltpu.VMEM((4, SUBLANES * tl, LANES), F32),
                        pltpu.VMEM((4, SUBLANES * tl, LANES), F32),
                        pltpu.VMEM((m, SGU_WIDTH), BF16),
                        pltpu.VMEM((2, SUBLANES, 2 * SSM_WIDTH), F32),
                        pltpu.VMEM((N_EXPERTS, LANES), F32)],
        compiler_params=pltpu.CompilerParams(
            dimension_semantics=("arbitrary",), vmem_limit_bytes=VMEM_LIMIT),
        name="mix",
    )(*operands)


def _sc_mesh():
    return plsc.VectorSubcoreMesh(core_axis_name="c", subcore_axis_name="s")


def _worker_id():
    return lax.axis_index("s") * SC_CORES + lax.axis_index("c")


def _dispatch_body(src_hbm, idx_hbm, out_hbm, idx_v, buf0, buf1, sem_r0, sem_r1, sem_w):
    n_chunks = idx_v.shape[0] // TOP_K
    wid = _worker_id()
    base = wid * (n_chunks * LANES)
    pltpu.sync_copy(idx_hbm.at[wid], idx_v)
    bufs = ((buf0, sem_r0), (buf1, sem_r1))

    def read(j, b):
        return pltpu.make_async_copy(src_hbm.at[pl.ds(base + j * LANES, LANES)], bufs[b][0],
                                     bufs[b][1])

    read(0, 0).start()

    @pl.loop(0, n_chunks, step=2)
    def _(j0):
        for b in range(2):
            j = j0 + b
            read(j, b).wait()

            @pl.when(j + 1 < n_chunks)
            def _():
                read(j + 1, 1 - b).start()

            copies = [pltpu.async_copy(bufs[b][0], out_hbm.at[idx_v.at[j * TOP_K + k]], sem_w)
                      for k in range(TOP_K)]
            for c in copies:
                c.wait()


def _sc_dispatch(src, idx, n_out):
    n_chunks = src.shape[0] // (SC_WORKERS * LANES)
    assert n_chunks % 2 == 0
    return pl.kernel(
        _dispatch_body,
        out_type=jax.ShapeDtypeStruct((n_out, LANES), jnp.int32),
        mesh=_sc_mesh(),
        scratch_types=[pltpu.VMEM((n_chunks * TOP_K, LANES), jnp.int32),
                       pltpu.VMEM((LANES, LANES), jnp.int32),
                       pltpu.VMEM((LANES, LANES), jnp.int32),
                       pltpu.SemaphoreType.DMA, pltpu.SemaphoreType.DMA,
                       pltpu.SemaphoreType.DMA],
        name="sc_dispatch",
    )(src, idx)


def _gather_body(tab_hbm, idx_hbm, out_hbm, idx_v, buf0, buf1, sem0, sem1):
    n_chunks = idx_v.shape[0]
    wid = _worker_id()
    base = wid * (n_chunks * LANES)
    pltpu.sync_copy(idx_hbm.at[wid], idx_v)

    @pl.loop(0, n_chunks, step=2)
    def _(j):
        c0 = pltpu.async_copy(tab_hbm.at[idx_v.at[j]], buf0, sem0)
        c1 = pltpu.async_copy(tab_hbm.at[idx_v.at[j + 1]], buf1, sem1)
        c0.wait()
        pltpu.sync_copy(buf0, out_hbm.at[pl.ds(base + j * LANES, LANES)])
        c1.wait()
        pltpu.sync_copy(buf1, out_hbm.at[pl.ds(base + (j + 1) * LANES, LANES)])


def _sc_gather(tab, idx):
    n_chunks = idx.shape[1]
    return pl.kernel(
        _gather_body,
        out_type=jax.ShapeDtypeStruct((SC_WORKERS * n_chunks * LANES, LANES), jnp.int32),
        mesh=_sc_mesh(),
        scratch_types=[pltpu.VMEM((n_chunks, LANES), jnp.int32),
                       pltpu.VMEM((LANES, LANES), jnp.int32),
                       pltpu.VMEM((LANES, LANES), jnp.int32),
                       pltpu.SemaphoreType.DMA, pltpu.SemaphoreType.DMA],
        name="sc_gather",
    )(tab, idx)


def _slots_kernel(pstart_ref, rt_ref, gat_ref, dis_ref, *, n_rows):
    tb = rt_ref.shape[1]
    idx = rt_ref[0:TOP_K, :]
    dest = rt_ref[2 * TOP_K:3 * TOP_K, :]
    for e in range(N_EXPERTS):
        dest = dest + jnp.where(idx == float(e), pstart_ref[e].astype(F32), 0.0)
    dest = dest.astype(jnp.int32)
    for k in range(TOP_K):
        for s in range(SUBROWS):
            row = dest[k:k + 1, :] + s * n_rows
            gat_ref[k, s:s + 1, :] = row
            for c in range(tb // LANES):
                dis_ref[s, c, k:k + 1, :] = row[:, c * LANES:(c + 1) * LANES]


def _slots_call(pstart, route_t, n_rows, n_parts):
    t = route_t.shape[1]
    tp = t // n_parts
    tb = min(tp, SLOT_TILE)
    per_part = tp // tb
    grid_spec = pltpu.PrefetchScalarGridSpec(
        num_scalar_prefetch=1,
        grid=(t // tb,),
        in_specs=[pl.BlockSpec((ROUTE_ROWS, tb), lambda i, ps: (0, i))],
        out_specs=[pl.BlockSpec((None, TOP_K, SUBROWS, tb),
                                lambda i, ps: (i // per_part, 0, 0, i % per_part)),
                   pl.BlockSpec((SUBROWS, tb // LANES, TOP_K, LANES), lambda i, ps: (0, i, 0, 0))],
    )
    return pl.pallas_call(
        functools.partial(_slots_kernel, n_rows=n_rows),
        grid_spec=grid_spec,
        out_shape=[jax.ShapeDtypeStruct((n_parts, TOP_K, SUBROWS, tp), jnp.int32),
                   jax.ShapeDtypeStruct((SUBROWS, t // LANES, TOP_K, LANES), jnp.int32)],
        compiler_params=pltpu.CompilerParams(dimension_semantics=("arbitrary",)),
        name="slots",
    )(pstart, route_t)


def _expert_kernel(be_ref, bv_ref, nx_ref, sl_ref, x_ref, xs_ref, bg_ref, bu_ref, bd_ref,
                   wg_hbm, wu_hbm, wd_hbm, y_ref, stage, wg_scr, wu_scr, wd_scr, inv_scr, sems):
    i = pl.program_id(0)
    valid = bv_ref[i]
    first = jnp.logical_and(
        valid > 0, jnp.logical_or(i == 0, be_ref[i] != be_ref[jnp.maximum(i - 1, 0)]))

    def weight_copies(expert, slot):
        return [pltpu.make_async_copy(w.at[expert], stage.at[slot, j], sems.at[slot, j])
                for j, w in enumerate((wg_hbm, wu_hbm, wd_hbm))]

    @pl.when(i == 0)
    def _():
        for c in weight_copies(be_ref[0], sl_ref[0]):
            c.start()

    @pl.when(first)
    def _():
        slot = sl_ref[i]
        for c in weight_copies(be_ref[i], slot):
            c.wait()
        wg_scr[...] = stage[slot, 0].astype(BF16)
        inv_scr[0] = jnp.ones(inv_scr.shape[1:], F32)
        for j, scr in ((1, wu_scr), (2, wd_scr)):
            w = stage[slot, j].astype(BF16)
            amax = jnp.max(jnp.max(jnp.abs(w), axis=0, keepdims=True), axis=1, keepdims=True)
            scale = jnp.exp2(jnp.floor(jnp.log2(
                F8_TARGET / jnp.maximum(amax.astype(F32), TINY))))
            scr[...] = (w * scale.astype(BF16)).astype(F8)
            inv_scr[j] = jnp.broadcast_to(1.0 / scale, inv_scr.shape[1:])

        @pl.when(nx_ref[i] >= 0)
        def _():
            for c in weight_copies(nx_ref[i], 1 - slot):
                c.start()

    @pl.when(valid > 0)
    def _():
        lo, hi = _unpack_rows(x_ref)
        live = lax.broadcasted_iota(jnp.int32, (ROW_BLOCK, 1), 0) < valid
        xb = jnp.where(live, jnp.concatenate(lo + hi, axis=1), 0.0).astype(BF16)
        x = xb.astype(F8)
        inv_x = 1.0 / xs_ref[0:1, 0:1]
        cg = (inv_scr[0, 0:1, 0:1] * inv_x).astype(BF16)
        cl = (inv_scr[1, 0:1, 0:1] * inv_x * HID_SCALE).astype(BF16)
        g = _dot(xb, wg_scr[...]).astype(BF16) * cg + bg_ref[...].astype(BF16)
        g = jnp.clip(g, GATE_FLOOR, SWIGLU_LIMIT)
        l = _dot(x, wu_scr[...]).astype(BF16) * cl + (bu_ref[...] * HID_SCALE).astype(BF16)
        l = jnp.clip(l, -SWIGLU_LIMIT * HID_SCALE, SWIGLU_LIMIT * HID_SCALE) + HID_SCALE
        hid = g * l / (1.0 + jnp.exp2(g * (-SWIGLU_ALPHA * LOG2_E)))
        y = _dot(hid.astype(F8), wd_scr[...])
        _pack_rows(y * (inv_scr[2, 0:1, 0:1] * (1.0 / HID_SCALE)) + bd_ref[...], y_ref)

    @pl.when(valid <= 0)
    def _():
        y_ref[...] = jnp.zeros_like(y_ref)


def _expert_call(block_expert, block_valid, next_expert, slot, x_tab, x_scale,
                 w_gate, b_gate, w_up, b_up, w_down, b_down):
    d, f = w_gate.shape[-2:]
    assert d == f
    n_blocks = block_expert.shape[0]
    bspec = lambda width: pl.BlockSpec((None, 1, width), lambda i, be, bv, nx, sl: (be[i], 0, 0))
    rows = pl.BlockSpec((SUBROWS, ROW_BLOCK, LANES), lambda i, be, bv, nx, sl: (0, i, 0))
    hbm = pl.BlockSpec(memory_space=pl.ANY)
    grid_spec = pltpu.PrefetchScalarGridSpec(
        num_scalar_prefetch=4,
        grid=(n_blocks,),
        in_specs=[rows, pl.BlockSpec((SUBLANES, LANES), lambda i, be, bv, nx, sl: (0, 0)),
                  bspec(f), bspec(f), bspec(d), hbm, hbm, hbm],
        out_specs=rows,
        scratch_shapes=[pltpu.VMEM((2, 3, d, f), F32),
                        pltpu.VMEM((d, f), BF16), pltpu.VMEM((d, f), F8), pltpu.VMEM((f, d), F8),
                        pltpu.VMEM((3, SUBLANES, LANES), F32), pltpu.SemaphoreType.DMA((2, 3))],
    )
    return pl.pallas_call(
        _expert_kernel,
        grid_spec=grid_spec,
        out_shape=jax.ShapeDtypeStruct(x_tab.shape, jnp.int32),
        compiler_params=pltpu.CompilerParams(
            dimension_semantics=("arbitrary",), vmem_limit_bytes=VMEM_LIMIT),
        name="expert",
    )(block_expert, block_valid, next_expert, slot, x_tab, x_scale, b_gate, b_up, b_down,
      w_gate, w_up, w_down)


def _combine_kernel(h_ref, routet_ref, gf_ref, ys_ref, *rest):
    out_ref = rest[-1]
    nb, tl, d = out_ref.shape
    m = nb * tl
    route = jnp.transpose(jnp.concatenate(
        [routet_ref[...], jnp.zeros((LANES - ROUTE_ROWS, m), F32)], axis=0))
    lo_acc, hi_acc = _unpack_rows(h_ref)
    for k in range(TOP_K):
        gate = route[:, TOP_K + k:TOP_K + k + 1]
        lo, hi = _unpack_rows(ys_ref.at[k])
        lo_acc = [a + gate * v for a, v in zip(lo_acc, lo)]
        hi_acc = [a + gate * v for a, v in zip(hi_acc, hi)]
    acc = jnp.concatenate(lo_acc + hi_acc, axis=1)
    out_ref[...] = _rmsnorm(acc, gf_ref[...]).reshape(nb, tl, d)


def _combine_call(h, route_t, g_final, y_slots, nb, seq, part, n_parts, prev):
    t, d = h.shape[1], D_MODEL
    tl = TIME_TILE
    m = nb * tl
    n = t // m // n_parts
    first = part * n
    in_specs = [pl.BlockSpec((SUBROWS, m, LANES), lambda i: (0, first + i, 0)),
                pl.BlockSpec((ROUTE_ROWS, m), lambda i: (0, first + i)),
                pl.BlockSpec((1, d), lambda i: (0, 0)),
                pl.BlockSpec((TOP_K, SUBROWS, m, LANES), lambda i: (0, 0, i, 0))]
    operands = [h, route_t, g_final, y_slots]
    aliases = {}
    if prev is not None:
        in_specs.append(pl.BlockSpec(memory_space=pl.ANY))
        operands.append(prev)
        aliases = {4: 0}
    return pl.pallas_call(
        _combine_kernel,
        grid=(n,),
        in_specs=in_specs,
        out_specs=pl.BlockSpec((nb, tl, d), lambda i: (0, first + i, 0)),
        out_shape=jax.ShapeDtypeStruct((nb, seq, d), F32),
        input_output_aliases=aliases,
        compiler_params=pltpu.CompilerParams(
            dimension_semantics=("arbitrary",), vmem_limit_bytes=VMEM_LIMIT),
        name="combine",
    )(*operands)


def _s5_operands(lam_re, lam_im, log_dt, b_re, b_im, c_re, c_im, nb):
    ns, pg = SSM_STATE, SSM_GROUP
    lam = lax.complex(lam_re.astype(F32), lam_im.astype(F32))
    dt = jnp.exp(log_dt.astype(F32))[:, None]
    lam_bar = jnp.exp(lam * dt)
    b_bar = ((lam_bar - 1.0) / lam)[..., None] * lax.complex(b_re.astype(F32), b_im.astype(F32))
    eye8 = jnp.eye(8, dtype=F32)
    split = lambda a: a.reshape((2, 2, 8) + a.shape[1:])

    def b_blocks(bb):
        return jnp.einsum('hpgnq,gk->phgqkn', split(bb), eye8).reshape(2, 2 * 8 * pg, 8 * ns)

    def c_blocks(cc):
        return jnp.einsum('hpgqn,gk->pgnhkq', split(cc), eye8).reshape(2, 8 * ns, 2 * 8 * pg)

    bd = jnp.concatenate([b_blocks(b_bar.real), b_blocks(b_bar.imag)], axis=2).astype(BF16)
    cd = jnp.concatenate([c_blocks(c_re.astype(F32)), -c_blocks(c_im.astype(F32))],
                         axis=1).astype(BF16)
    lam_rows = split(lam_bar).transpose(1, 0, 2, 3).reshape(2, 2, 8 * ns)
    lam_rows = jnp.repeat(lam_rows, nb, axis=1)
    return bd, cd, lam_rows.real, lam_rows.imag


def kernel(x, norm_mix_g, w_in, lam_re, lam_im, log_dt, b_re, b_im, c_re, c_im, d_skip, w_glu, b_glu, sgu_ln_g, sgu_ln_b, w_s, b_s, w_branch_a, w_branch_b, w_out, norm_moe_g, w_router, b_router, w_gate, b_gate, w_up, b_up, w_down, b_down, norm_final_g):
    nb, seq, d = x.shape
    assert d == D_MODEL and SUBLANES % nb == 0 and SUBLANES // nb == 2
    assert seq % TIME_TILE == 0 and norm_mix_g.shape[0] == 1
    assert (nb * seq * SUBROWS) % (SC_WORKERS * LANES) == 0
    assert (nb * seq * SUBROWS * TOP_K) % (SC_WORKERS * LANES * 2 * COMBINE_PARTS) == 0
    assert (seq // TIME_TILE) % COMBINE_PARTS == 0
    tl = TIME_TILE
    t = nb * seq
    row = lambda v: v.reshape(1, -1).astype(F32)

    bd, cd, a_re, a_im = _s5_operands(
        lam_re[0], lam_im[0], log_dt[0], b_re[0], b_im[0], c_re[0], c_im[0], nb)
    w_r = jnp.zeros((d, LANES), F32).at[:, :N_EXPERTS].set(w_router[0].astype(F32)).astype(BF16)
    b_r = jnp.full((1, LANES), NEG_BIG, F32).at[0, :N_EXPERTS].set(b_router[0].astype(F32))
    x_bound = math.sqrt(d) * jnp.max(jnp.abs(norm_moe_g[0].astype(F32)))
    x_scale = jnp.exp2(jnp.floor(jnp.log2(F8_TARGET / jnp.maximum(x_bound, TINY))))
    x_scale = jnp.full((SUBLANES, LANES), 1.0, F32) * x_scale
    m = nb * tl
    upper = (jnp.arange(m)[:, None] < jnp.arange(m)[None, :]).astype(BF16)
    h, xn2p, route_t, cnt = _mix_call(
        x, row(norm_mix_g[0]), w_in[0].astype(BF16), row(sgu_ln_g[0]), row(sgu_ln_b[0]),
        w_s[0].astype(F32), b_s[0].T.astype(F32), w_branch_b[0].astype(BF16),
        upper, bd, cd, a_re, a_im, row(d_skip[0]),
        w_glu[0].astype(BF16), row(b_glu[0]), w_branch_a[0].astype(BF16), w_out[0].astype(BF16),
        row(norm_moe_g[0]), x_scale, w_r, b_r)

    counts = cnt[:, 0].astype(jnp.int32)
    padded = (counts + ROW_BLOCK - 1) // ROW_BLOCK * ROW_BLOCK
    experts = jnp.arange(N_EXPERTS, dtype=jnp.int32)
    upto = experts[None, :] <= experts[:, None]
    cum = jnp.sum(jnp.where(upto, padded[None, :], 0), axis=1)
    pstart = cum - padded
    n_blocks = (t * TOP_K) // ROW_BLOCK + N_EXPERTS
    n_rows = n_blocks * ROW_BLOCK
    block_row0 = jnp.arange(n_blocks, dtype=jnp.int32) * ROW_BLOCK
    block_expert = jnp.minimum(
        jnp.sum((cum[None, :] <= block_row0[:, None]).astype(jnp.int32), axis=1), N_EXPERTS - 1)
    of_block = block_expert[:, None] == experts[None, :]
    pick = lambda table: jnp.sum(jnp.where(of_block, table[None, :], 0), axis=1)
    block_valid = jnp.clip(pick(counts) - (block_row0 - pick(pstart)), 0, ROW_BLOCK)
    present = counts > 0
    slot_e = (jnp.sum(jnp.where(upto, present[None, :].astype(jnp.int32), 0), axis=1) - 1) % 2
    later = jnp.logical_and(experts[None, :] > experts[:, None], present[None, :])
    next_e = jnp.min(jnp.where(later, experts[None, :], N_EXPERTS), axis=1)
    next_e = jnp.where(next_e == N_EXPERTS, -1, next_e)

    idx_parts, idx_dispatch = _slots_call(pstart, route_t, n_rows, COMBINE_PARTS)
    idx_dispatch = idx_dispatch.reshape(SC_WORKERS, -1, LANES)

    x_tab = _sc_dispatch(xn2p.reshape(SUBROWS * t, LANES), idx_dispatch, SUBROWS * n_rows)
    y_tab = _expert_call(
        block_expert, block_valid, pick(next_e), pick(slot_e),
        x_tab.reshape(SUBROWS, n_rows, LANES), x_scale,
        w_gate[0], b_gate[0][:, None, :], w_up[0], b_up[0][:, None, :],
        w_down[0], b_down[0][:, None, :])
    y_flat = y_tab.reshape(SUBROWS * n_rows, LANES)
    tp = t // COMBINE_PARTS
    out = None
    for q in range(COMBINE_PARTS):
        y_slots = _sc_gather(y_flat, idx_parts[q].reshape(SC_WORKERS, -1, LANES))
        out = _combine_call(h, route_t, row(norm_final_g),
                            y_slots.reshape(TOP_K, SUBROWS, tp, LANES), nb, seq,
                            q, COMBINE_PARTS, out)
    return out
```

```python
import functools
import math

import jax
import jax.numpy as jnp
from jax import lax
from jax.experimental import pallas as pl
from jax.experimental.pallas import tpu as pltpu
from jax.experimental.pallas import tpu_sc as plsc

F32 = jnp.float32
BF16 = jnp.bfloat16
F8 = jnp.float8_e4m3fn
F8_TARGET = 240.0
TINY = 1e-30

NORM_EPS = 1e-5
D_MODEL = 1024
SSM_WIDTH = 512
SSM_GROUP = 16
SSM_STATE = 64
SGU_WIDTH = 512
SGU_BLOCK = 128
SGU_HEADS = 4
SGU_HEAD_DIM = 128
CHUNK = 64
N_EXPERTS = 32
TOP_K = 4
SWIGLU_ALPHA = 1.702
SWIGLU_LIMIT = 7.0
GATE_FLOOR = -128.0
LOG2_E = 1.4426950408889634
HID_SCALE = 4.0

LANES = 128
SUBLANES = 8
TIME_TILE = 128
SCAN_BLOCK = 64
ROW_BLOCK = 1024
VMEM_LIMIT = 56 * 1024 * 1024
NEG_BIG = -1e30
SC_CORES = 2
SC_SUBCORES = 16
SC_WORKERS = SC_CORES * SC_SUBCORES
SUBROWS = 4
COMBINE_PARTS = 8
SLOT_TILE = 4096
ROUTE_ROWS = 16
PACKED = D_MODEL // 2


def _pack_rows(x, out_ref):
    lo = lax.bitcast_convert_type(x[:, :PACKED].astype(BF16).astype(F32), jnp.int32)
    hi = lax.bitcast_convert_type(x[:, PACKED:].astype(BF16).astype(F32), jnp.int32)
    words = lax.shift_right_logical(lo, 16) | (hi & jnp.int32(-65536))
    for s in range(SUBROWS):
        out_ref[s] = words[:, s * LANES:(s + 1) * LANES]


def _unpack_rows(ref):
    lo, hi = [], []
    for s in range(SUBROWS):
        w = ref[s]
        lo.append(lax.bitcast_convert_type(lax.shift_left(w, 16), F32))
        hi.append(lax.bitcast_convert_type(w & jnp.int32(-65536), F32))
    return lo, hi


def _dot(a, b):
    return jnp.dot(a, b, preferred_element_type=F32)


def _sigmoid(x):
    return 1.0 / (1.0 + jnp.exp(-x))


def _gelu(x):
    return 0.5 * x * (1.0 + jnp.tanh(0.7978845608028654 * (x + 0.044715 * (x * x * x))))


def _rmsnorm(x, g):
    return x * lax.rsqrt(jnp.mean(x * x, axis=-1, keepdims=True) + NORM_EPS) * g


def _proj_xa(x, g_ref, w_ref):
    xn = _rmsnorm(x, g_ref[...]).astype(BF16)
    return xn, _dot(xn, w_ref[:, 0:SSM_WIDTH]).astype(BF16)


def _proj_gate(xn, w_ref, lo):
    return _sigmoid(_dot(xn, w_ref[:, lo:lo + D_MODEL]))


def _sgu(z, lng_ref, lnb_ref, ws_ref, bs_ref, yb_scr):
    m = z.shape[0]
    u = z[:, :SGU_WIDTH]
    v = z[:, SGU_WIDTH:]
    mu = jnp.mean(v, axis=-1, keepdims=True)
    vc = v - mu
    v = vc * lax.rsqrt(jnp.mean(vc * vc, axis=-1, keepdims=True) + NORM_EPS)
    v = (v * lng_ref[...] + lnb_ref[...]).astype(BF16)

    ri = lax.broadcasted_iota(jnp.int32, (SGU_BLOCK, SGU_BLOCK), 0) // CHUNK
    ci = lax.broadcasted_iota(jnp.int32, (SGU_BLOCK, SGU_BLOCK), 1) // CHUNK
    causal = ri >= ci
    for h in range(SGU_HEADS):
        wm = jnp.where(causal, ws_ref[h], 0.0).astype(BF16)
        bias = bs_ref[:, h:h + 1]
        lo = h * SGU_HEAD_DIM
        for blk in range(m // SGU_BLOCK):
            r0 = blk * SGU_BLOCK
            s = _dot(wm, v[r0:r0 + SGU_BLOCK, lo:lo + SGU_HEAD_DIM]) + bias
            yb_scr[r0:r0 + SGU_BLOCK, lo:lo + SGU_HEAD_DIM] = (
                u[r0:r0 + SGU_BLOCK, lo:lo + SGU_HEAD_DIM] * s).astype(BF16)


def _mix_kernel(x_ref, g1_ref, win_ref, lng_ref, lnb_ref, ws_ref, bs_ref, wb_ref,
                upper_ref, bd_ref, cd_ref,
                are_ref, aim_ref, dskip_ref, wglu_ref, bglu_ref, wa_ref, wout_ref,
                g2_ref, xs_ref, wr_ref, br_ref,
                h_ref, xn2_ref, routet_ref, cnt_ref,
                bu0_scr, bu1_scr, x8_scr, y8_scr, yb_scr, state_scr, cnt_scr):
    nb, tl, d = x_ref.shape
    m = nb * tl
    x = x_ref[...].reshape(m, d)
    xn, xa_bf = _proj_xa(x, g1_ref, win_ref)

    @pl.when(pl.program_id(0) == 0)
    def _():
        state_scr[...] = jnp.zeros_like(state_scr)
        cnt_scr[...] = jnp.zeros_like(cnt_scr)
        x8_scr[...] = jnp.zeros_like(x8_scr)

    xa = xa_bf.astype(F32)

    def chunk_lo(p, hf):
        return hf * (SSM_WIDTH // 2) + p * LANES

    scans = (bu0_scr, bu1_scr)
    for p in range(2):
        for hf in range(2):
            lo = chunk_lo(p, hf)
            for b in range(nb):
                x8_scr[2 * p + hf, pl.ds(hf * nb + b, tl, stride=SUBLANES), :] = (
                    xa[b * tl:(b + 1) * tl, lo:lo + LANES])
        x8 = jnp.concatenate([x8_scr[2 * p], x8_scr[2 * p + 1]], axis=1).astype(BF16)
        scans[p][...] = _dot(x8, bd_ref[p])

    a_re = [are_ref[p] for p in range(2)]
    a_im = [aim_ref[p] for p in range(2)]
    s_re = [state_scr[p][:, :SSM_WIDTH] for p in range(2)]
    s_im = [state_scr[p][:, SSM_WIDTH:] for p in range(2)]
    s_b = SSM_WIDTH + 2 * SGU_WIDTH
    z = _gelu(_dot(xn, win_ref[:, SSM_WIDTH:s_b]))
    gates = []
    for t0 in range(0, tl, SCAN_BLOCK):
        rows_p = ([], [])
        for t in range(t0, t0 + SCAN_BLOCK):
            r = t * SUBLANES
            for p in range(2):
                bu_scr = scans[p]
                n_re = a_re[p] * s_re[p] - a_im[p] * s_im[p] + bu_scr[r:r + SUBLANES, 0:SSM_WIDTH]
                n_im = (a_re[p] * s_im[p] + a_im[p] * s_re[p]
                        + bu_scr[r:r + SUBLANES, SSM_WIDTH:2 * SSM_WIDTH])
                s_re[p], s_im[p] = n_re, n_im
                rows_p[p].append(jnp.concatenate([n_re, n_im], axis=1))
        r0 = t0 * SUBLANES
        for p in range(2):
            y8 = _dot(jnp.concatenate(rows_p[p], axis=0).astype(BF16), cd_ref[p])
            for hf in range(2):
                y8_scr[2 * p + hf, r0:r0 + SCAN_BLOCK * SUBLANES, :] = (
                    y8[:, hf * LANES:(hf + 1) * LANES])
        gates.append(_proj_gate(xn, win_ref, s_b + len(gates) * D_MODEL))

    sga = gates[0]

    y_chunks = {}
    for p in range(2):
        state_scr[p] = jnp.concatenate([s_re[p], s_im[p]], axis=1)
        for hf in range(2):
            y_chunks[(hf, p)] = jnp.concatenate(
                [y8_scr[2 * p + hf, pl.ds(hf * nb + b, tl, stride=SUBLANES), :]
                 for b in range(nb)], axis=0)
    y = jnp.concatenate([y_chunks[(hf, p)] for hf in range(2) for p in range(2)], axis=1)
    y = y + dskip_ref[...] * xa
    zs = _gelu(y)
    ya = zs * _sigmoid(_dot(zs.astype(BF16), wglu_ref[...]) + bglu_ref[...])
    _sgu(z, lng_ref, lnb_ref, ws_ref, bs_ref, yb_scr)
    pb = (gates[1] * _dot(yb_scr[...], wb_ref[...])).astype(BF16)
    pa = sga * _dot(ya.astype(BF16), wa_ref[...])
    merged = pa + pb.astype(F32)
    h = x + _dot(merged.astype(BF16), wout_ref[...])
    _pack_rows(h, h_ref)
    xn2 = _rmsnorm(h, g2_ref[...])
    _pack_rows(xn2 * xs_ref[0:1, 0:1], xn2_ref)

    logits = _dot(xn2.astype(BF16), wr_ref[...]) + br_ref[...]
    work = jnp.transpose(logits)[:N_EXPERTS]
    expert = lax.broadcasted_iota(jnp.int32, (N_EXPERTS, m), 0).astype(F32)
    vals, idxs = [], []
    member = jnp.zeros((N_EXPERTS, m), F32)
    for _ in range(TOP_K):
        mx = jnp.max(work, axis=0, keepdims=True)
        ix = jnp.min(jnp.where(work == mx, expert, float(N_EXPERTS)), axis=0, keepdims=True)
        hit = expert == ix
        member = jnp.where(hit, 1.0, member)
        work = jnp.where(hit, NEG_BIG, work)
        vals.append(mx)
        idxs.append(ix)
    exps = [jnp.exp(v - vals[0]) for v in vals]
    denom = exps[0] + exps[1] + exps[2] + exps[3]
    gates = [e / denom for e in exps]

    before = _dot(member.astype(BF16), upper_ref[...]) + cnt_scr[:, 0:1]
    new_cnt = cnt_scr[:, 0:1] + jnp.sum(member, axis=1, keepdims=True)
    cnt_scr[...] = jnp.broadcast_to(new_cnt, cnt_scr.shape)
    cnt_ref[...] = jnp.broadcast_to(new_cnt, cnt_ref.shape)
    ranks = [jnp.sum(jnp.where(expert == ix, before, 0.0), axis=0, keepdims=True) for ix in idxs]
    pad = jnp.zeros((ROUTE_ROWS - 3 * TOP_K, m), F32)
    routet_ref[...] = jnp.concatenate(idxs + gates + ranks + [pad], axis=0)


def _mix_call(x, *params):
    nb, seq, d = x.shape
    tl = TIME_TILE
    m = nb * tl
    n_tiles = seq // tl
    const = lambda shape: pl.BlockSpec(shape, lambda i: (0,) * len(shape))
    resident = lambda shape: pl.BlockSpec(shape, lambda i: (0,) * len(shape),
                                          pipeline_mode=pl.Buffered(1))
    tile = lambda width: pl.BlockSpec((nb, tl, width), lambda i: (0, i, 0))
    packed_rows = pl.BlockSpec((SUBROWS, m, LANES), lambda i: (0, i, 0))
    operands = (x,) + params
    in_specs = [tile(d)] + [resident(o.shape) for o in params]
    return pl.pallas_call(
        _mix_kernel,
        grid=(n_tiles,),
        in_specs=in_specs,
        out_specs=[packed_rows, packed_rows,
                   pl.BlockSpec((ROUTE_ROWS, m), lambda i: (0, i)), const((N_EXPERTS, LANES))],
        out_shape=[jax.ShapeDtypeStruct((SUBROWS, n_tiles * m, LANES), jnp.int32),
                   jax.ShapeDtypeStruct((SUBROWS, n_tiles * m, LANES), jnp.int32),
                   jax.ShapeDtypeStruct((ROUTE_ROWS, n_tiles * m), F32),
                   jax.ShapeDtypeStruct((N_EXPERTS, LANES), F32)],
        scratch_shapes=[pltpu.VMEM((SUBLANES * tl, 2 * SSM_WIDTH), F32),
                        pltpu.VMEM((SUBLANES * tl, 2 * SSM_WIDTH), F32),
                        pltpu.VMEM((4, SUBLANES * tl, LANES), F32),
                        pltpu.VMEM((4, SUBLANES * tl, LANES), F32),
                        pltpu.VMEM((m, SGU_WIDTH), BF16),
                        pltpu.VMEM((2, SUBLANES, 2 * SSM_WIDTH), F32),
                        pltpu.VMEM((N_EXPERTS, LANES), F32)],
        compiler_params=pltpu.CompilerParams(
            dimension_semantics=("arbitrary",), vmem_limit_bytes=VMEM_LIMIT),
        name="mix",
    )(*operands)


def _sc_mesh():
    return plsc.VectorSubcoreMesh(core_axis_name="c", subcore_axis_name="s")


def _worker_id():
    return lax.axis_index("s") * SC_CORES + lax.axis_index("c")


def _dispatch_body(src_hbm, idx_hbm, out_hbm, idx_v, buf0, buf1, sem_r0, sem_r1, sem_w):
    n_chunks = idx_v.shape[0] // TOP_K
    wid = _worker_id()
    base = wid * (n_chunks * LANES)
    pltpu.sync_copy(idx_hbm.at[wid], idx_v)
    bufs = ((buf0, sem_r0), (buf1, sem_r1))

    def read(j, b):
        return pltpu.make_async_copy(src_hbm.at[pl.ds(base + j * LANES, LANES)], bufs[b][0],
                                     bufs[b][1])

    read(0, 0).start()

    @pl.loop(0, n_chunks, step=2)
    def _(j0):
        for b in range(2):
            j = j0 + b
            read(j, b).wait()

            @pl.when(j + 1 < n_chunks)
            def _():
                read(j + 1, 1 - b).start()

            copies = [pltpu.async_copy(bufs[b][0], out_hbm.at[idx_v.at[j * TOP_K + k]], sem_w)
                      for k in range(TOP_K)]
            for c in copies:
                c.wait()


def _sc_dispatch(src, idx, n_out):
    n_chunks = src.shape[0] // (SC_WORKERS * LANES)
    assert n_chunks % 2 == 0
    return pl.kernel(
        _dispatch_body,
        out_type=jax.ShapeDtypeStruct((n_out, LANES), jnp.int32),
        mesh=_sc_mesh(),
        scratch_types=[pltpu.VMEM((n_chunks * TOP_K, LANES), jnp.int32),
                       pltpu.VMEM((LANES, LANES), jnp.int32),
                       pltpu.VMEM((LANES, LANES), jnp.int32),
                       pltpu.SemaphoreType.DMA, pltpu.SemaphoreType.DMA,
                       pltpu.SemaphoreType.DMA],
        name="sc_dispatch",
    )(src, idx)


def _gather_body(tab_hbm, idx_hbm, out_hbm, idx_v, buf0, buf1, sem0, sem1):
    n_chunks = idx_v.shape[0]
    wid = _worker_id()
    base = wid * (n_chunks * LANES)
    pltpu.sync_copy(idx_hbm.at[wid], idx_v)

    @pl.loop(0, n_chunks, step=2)
    def _(j):
        c0 = pltpu.async_copy(tab_hbm.at[idx_v.at[j]], buf0, sem0)
        c1 = pltpu.async_copy(tab_hbm.at[idx_v.at[j + 1]], buf1, sem1)
        c0.wait()
        pltpu.sync_copy(buf0, out_hbm.at[pl.ds(base + j * LANES, LANES)])
        c1.wait()
        pltpu.sync_copy(buf1, out_hbm.at[pl.ds(base + (j + 1) * LANES, LANES)])


def _sc_gather(tab, idx):
    n_chunks = idx.shape[1]
    return pl.kernel(
        _gather_body,
        out_type=jax.ShapeDtypeStruct((SC_WORKERS * n_chunks * LANES, LANES), jnp.int32),
        mesh=_sc_mesh(),
        scratch_types=[pltpu.VMEM((n_chunks, LANES), jnp.int32),
                       pltpu.VMEM((LANES, LANES), jnp.int32),
                       pltpu.VMEM((LANES, LANES), jnp.int32),
                       pltpu.SemaphoreType.DMA, pltpu.SemaphoreType.DMA],
        name="sc_gather",
    )(tab, idx)


def _slots_kernel(pstart_ref, rt_ref, gat_ref, dis_ref, *, n_rows):
    tb = rt_ref.shape[1]
    idx = rt_ref[0:TOP_K, :]
    dest = rt_ref[2 * TOP_K:3 * TOP_K, :]
    for e in range(N_EXPERTS):
        dest = dest + jnp.where(idx == float(e), pstart_ref[e].astype(F32), 0.0)
    dest = dest.astype(jnp.int32)
    for k in range(TOP_K):
        for s in range(SUBROWS):
            row = dest[k:k + 1, :] + s * n_rows
            gat_ref[k, s:s + 1, :] = row
            for c in range(tb // LANES):
                dis_ref[s, c, k:k + 1, :] = row[:, c * LANES:(c + 1) * LANES]


def _slots_call(pstart, route_t, n_rows, n_parts):
    t = route_t.shape[1]
    tp = t // n_parts
    tb = min(tp, SLOT_TILE)
    per_part = tp // tb
    grid_spec = pltpu.PrefetchScalarGridSpec(
        num_scalar_prefetch=1,
        grid=(t // tb,),
        in_specs=[pl.BlockSpec((ROUTE_ROWS, tb), lambda i, ps: (0, i))],
        out_specs=[pl.BlockSpec((None, TOP_K, SUBROWS, tb),
                                lambda i, ps: (i // per_part, 0, 0, i % per_part)),
                   pl.BlockSpec((SUBROWS, tb // LANES, TOP_K, LANES), lambda i, ps: (0, i, 0, 0))],
    )
    return pl.pallas_call(
        functools.partial(_slots_kernel, n_rows=n_rows),
        grid_spec=grid_spec,
        out_shape=[jax.ShapeDtypeStruct((n_parts, TOP_K, SUBROWS, tp), jnp.int32),
                   jax.ShapeDtypeStruct((SUBROWS, t // LANES, TOP_K, LANES), jnp.int32)],
        compiler_params=pltpu.CompilerParams(dimension_semantics=("arbitrary",)),
        name="slots",
    )(pstart, route_t)


def _expert_kernel(be_ref, bv_ref, nx_ref, sl_ref, x_ref, xs_ref, bg_ref, bu_ref, bd_ref,
                   wg_hbm, wu_hbm, wd_hbm, y_ref, stage, wg_scr, wu_scr, wd_scr, inv_scr, sems):
    i = pl.program_id(0)
    valid = bv_ref[i]
    first = jnp.logical_and(
        valid > 0, jnp.logical_or(i == 0, be_ref[i] != be_ref[jnp.maximum(i - 1, 0)]))

    def weight_copies(expert, slot):
        return [pltpu.make_async_copy(w.at[expert], stage.at[slot, j], sems.at[slot, j])
                for j, w in enumerate((wg_hbm, wu_hbm, wd_hbm))]

    @pl.when(i == 0)
    def _():
        for c in weight_copies(be_ref[0], sl_ref[0]):
            c.start()

    @pl.when(first)
    def _():
        slot = sl_ref[i]
        for c in weight_copies(be_ref[i], slot):
            c.wait()
        for j, scr in enumerate((wg_scr, wu_scr, wd_scr)):
            w = stage[slot, j].astype(BF16)
            amax = jnp.max(jnp.max(jnp.abs(w), axis=0, keepdims=True), axis=1, keepdims=True)
            scale = jnp.exp2(jnp.floor(jnp.log2(
                F8_TARGET / jnp.maximum(amax.astype(F32), TINY))))
            scr[...] = (w * scale.astype(BF16)).astype(F8)
            inv_scr[j] = jnp.broadcast_to(1.0 / scale, inv_scr.shape[1:])

        @pl.when(nx_ref[i] >= 0)
        def _():
            for c in weight_copies(nx_ref[i], 1 - slot):
                c.start()

    @pl.when(valid > 0)
    def _():
        lo, hi = _unpack_rows(x_ref)
        live = lax.broadcasted_iota(jnp.int32, (ROW_BLOCK, 1), 0) < valid
        x = jnp.where(live, jnp.concatenate(lo + hi, axis=1), 0.0).astype(BF16).astype(F8)
        inv_x = 1.0 / xs_ref[0:1, 0:1]
        cg = (inv_scr[0, 0:1, 0:1] * inv_x).astype(BF16)
        cl = (inv_scr[1, 0:1, 0:1] * inv_x * HID_SCALE).astype(BF16)
        g = _dot(x, wg_scr[...]).astype(BF16) * cg + bg_ref[...].astype(BF16)
        g = jnp.clip(g, GATE_FLOOR, SWIGLU_LIMIT)
        l = _dot(x, wu_scr[...]).astype(BF16) * cl + (bu_ref[...] * HID_SCALE).astype(BF16)
        l = jnp.clip(l, -SWIGLU_LIMIT * HID_SCALE, SWIGLU_LIMIT * HID_SCALE) + HID_SCALE
        hid = g * l / (1.0 + jnp.exp2(g * (-SWIGLU_ALPHA * LOG2_E)))
        y = _dot(hid.astype(F8), wd_scr[...])
        _pack_rows(y * (inv_scr[2, 0:1, 0:1] * (1.0 / HID_SCALE)) + bd_ref[...], y_ref)

    @pl.when(valid <= 0)
    def _():
        y_ref[...] = jnp.zeros_like(y_ref)


def _expert_call(block_expert, block_valid, next_expert, slot, x_tab, x_scale,
                 w_gate, b_gate, w_up, b_up, w_down, b_down):
    d, f = w_gate.shape[-2:]
    assert d == f
    n_blocks = block_expert.shape[0]
    bspec = lambda width: pl.BlockSpec((None, 1, width), lambda i, be, bv, nx, sl: (be[i], 0, 0))
    rows = pl.BlockSpec((SUBROWS, ROW_BLOCK, LANES), lambda i, be, bv, nx, sl: (0, i, 0))
    hbm = pl.BlockSpec(memory_space=pl.ANY)
    grid_spec = pltpu.PrefetchScalarGridSpec(
        num_scalar_prefetch=4,
        grid=(n_blocks,),
        in_specs=[rows, pl.BlockSpec((SUBLANES, LANES), lambda i, be, bv, nx, sl: (0, 0)),
                  bspec(f), bspec(f), bspec(d), hbm, hbm, hbm],
        out_specs=rows,
        scratch_shapes=[pltpu.VMEM((2, 3, d, f), F32),
                        pltpu.VMEM((d, f), F8), pltpu.VMEM((d, f), F8), pltpu.VMEM((f, d), F8),
                        pltpu.VMEM((3, SUBLANES, LANES), F32), pltpu.SemaphoreType.DMA((2, 3))],
    )
    return pl.pallas_call(
        _expert_kernel,
        grid_spec=grid_spec,
        out_shape=jax.ShapeDtypeStruct(x_tab.shape, jnp.int32),
        compiler_params=pltpu.CompilerParams(
            dimension_semantics=("arbitrary",), vmem_limit_bytes=VMEM_LIMIT),
        name="expert",
    )(block_expert, block_valid, next_expert, slot, x_tab, x_scale, b_gate, b_up, b_down,
      w_gate, w_up, w_down)


def _combine_kernel(h_ref, routet_ref, gf_ref, ys_ref, *rest):
    out_ref = rest[-1]
    nb, tl, d = out_ref.shape
    m = nb * tl
    route = jnp.transpose(jnp.concatenate(
        [routet_ref[...], jnp.zeros((LANES - ROUTE_ROWS, m), F32)], axis=0))
    lo_acc, hi_acc = _unpack_rows(h_ref)
    for k in range(TOP_K):
        gate = route[:, TOP_K + k:TOP_K + k + 1]
        lo, hi = _unpack_rows(ys_ref.at[k])
        lo_acc = [a + gate * v for a, v in zip(lo_acc, lo)]
        hi_acc = [a + gate * v for a, v in zip(hi_acc, hi)]
    acc = jnp.concatenate(lo_acc + hi_acc, axis=1)
    out_ref[...] = _rmsnorm(acc, gf_ref[...]).reshape(nb, tl, d)


def _combine_call(h, route_t, g_final, y_slots, nb, seq, part, n_parts, prev):
    t, d = h.shape[1], D_MODEL
    tl = TIME_TILE
    m = nb * tl
    n = t // m // n_parts
    first = part * n
    in_specs = [pl.BlockSpec((SUBROWS, m, LANES), lambda i: (0, first + i, 0)),
                pl.BlockSpec((ROUTE_ROWS, m), lambda i: (0, first + i)),
                pl.BlockSpec((1, d), lambda i: (0, 0)),
                pl.BlockSpec((TOP_K, SUBROWS, m, LANES), lambda i: (0, 0, i, 0))]
    operands = [h, route_t, g_final, y_slots]
    aliases = {}
    if prev is not None:
        in_specs.append(pl.BlockSpec(memory_space=pl.ANY))
        operands.append(prev)
        aliases = {4: 0}
    return pl.pallas_call(
        _combine_kernel,
        grid=(n,),
        in_specs=in_specs,
        out_specs=pl.BlockSpec((nb, tl, d), lambda i: (0, first + i, 0)),
        out_shape=jax.ShapeDtypeStruct((nb, seq, d), F32),
        input_output_aliases=aliases,
        compiler_params=pltpu.CompilerParams(
            dimension_semantics=("arbitrary",), vmem_limit_bytes=VMEM_LIMIT),
        name="combine",
    )(*operands)


def _s5_operands(lam_re, lam_im, log_dt, b_re, b_im, c_re, c_im, nb):
    ns, pg = SSM_STATE, SSM_GROUP
    lam = lax.complex(lam_re.astype(F32), lam_im.astype(F32))
    dt = jnp.exp(log_dt.astype(F32))[:, None]
    lam_bar = jnp.exp(lam * dt)
    b_bar = ((lam_bar - 1.0) / lam)[..., None] * lax.complex(b_re.astype(F32), b_im.astype(F32))
    eye8 = jnp.eye(8, dtype=F32)
    split = lambda a: a.reshape((2, 2, 8) + a.shape[1:])

    def b_blocks(bb):
        return jnp.einsum('hpgnq,gk->phgqkn', split(bb), eye8).reshape(2, 2 * 8 * pg, 8 * ns)

    def c_blocks(cc):
        return jnp.einsum('hpgqn,gk->pgnhkq', split(cc), eye8).reshape(2, 8 * ns, 2 * 8 * pg)

    bd = jnp.concatenate([b_blocks(b_bar.real), b_blocks(b_bar.imag)], axis=2).astype(BF16)
    cd = jnp.concatenate([c_blocks(c_re.astype(F32)), -c_blocks(c_im.astype(F32))],
                         axis=1).astype(BF16)
    lam_rows = split(lam_bar).transpose(1, 0, 2, 3).reshape(2, 2, 8 * ns)
    lam_rows = jnp.repeat(lam_rows, nb, axis=1)
    return bd, cd, lam_rows.real, lam_rows.imag


def kernel(x, norm_mix_g, w_in, lam_re, lam_im, log_dt, b_re, b_im, c_re, c_im, d_skip, w_glu, b_glu, sgu_ln_g, sgu_ln_b, w_s, b_s, w_branch_a, w_branch_b, w_out, norm_moe_g, w_router, b_router, w_gate, b_gate, w_up, b_up, w_down, b_down, norm_final_g):
    nb, seq, d = x.shape
    assert d == D_MODEL and SUBLANES % nb == 0 and SUBLANES // nb == 2
    assert seq % TIME_TILE == 0 and norm_mix_g.shape[0] == 1
    assert (nb * seq * SUBROWS) % (SC_WORKERS * LANES) == 0
    assert (nb * seq * SUBROWS * TOP_K) % (SC_WORKERS * LANES * 2 * COMBINE_PARTS) == 0
    assert (seq // TIME_TILE) % COMBINE_PARTS == 0
    tl = TIME_TILE
    t = nb * seq
    row = lambda v: v.reshape(1, -1).astype(F32)

    bd, cd, a_re, a_im = _s5_operands(
        lam_re[0], lam_im[0], log_dt[0], b_re[0], b_im[0], c_re[0], c_im[0], nb)
    w_r = jnp.zeros((d, LANES), F32).at[:, :N_EXPERTS].set(w_router[0].astype(F32)).astype(BF16)
    b_r = jnp.full((1, LANES), NEG_BIG, F32).at[0, :N_EXPERTS].set(b_router[0].astype(F32))
    x_bound = math.sqrt(d) * jnp.max(jnp.abs(norm_moe_g[0].astype(F32)))
    x_scale = jnp.exp2(jnp.floor(jnp.log2(F8_TARGET / jnp.maximum(x_bound, TINY))))
    x_scale = jnp.full((SUBLANES, LANES), 1.0, F32) * x_scale
    m = nb * tl
    upper = (jnp.arange(m)[:, None] < jnp.arange(m)[None, :]).astype(BF16)
    h, xn2p, route_t, cnt = _mix_call(
        x, row(norm_mix_g[0]), w_in[0].astype(BF16), row(sgu_ln_g[0]), row(sgu_ln_b[0]),
        w_s[0].astype(F32), b_s[0].T.astype(F32), w_branch_b[0].astype(BF16),
        upper, bd, cd, a_re, a_im, row(d_skip[0]),
        w_glu[0].astype(BF16), row(b_glu[0]), w_branch_a[0].astype(BF16), w_out[0].astype(BF16),
        row(norm_moe_g[0]), x_scale, w_r, b_r)

    counts = cnt[:, 0].astype(jnp.int32)
    padded = (counts + ROW_BLOCK - 1) // ROW_BLOCK * ROW_BLOCK
    experts = jnp.arange(N_EXPERTS, dtype=jnp.int32)
    upto = experts[None, :] <= experts[:, None]
    cum = jnp.sum(jnp.where(upto, padded[None, :], 0), axis=1)
    pstart = cum - padded
    n_blocks = (t * TOP_K) // ROW_BLOCK + N_EXPERTS
    n_rows = n_blocks * ROW_BLOCK
    block_row0 = jnp.arange(n_blocks, dtype=jnp.int32) * ROW_BLOCK
    block_expert = jnp.minimum(
        jnp.sum((cum[None, :] <= block_row0[:, None]).astype(jnp.int32), axis=1), N_EXPERTS - 1)
    of_block = block_expert[:, None] == experts[None, :]
    pick = lambda table: jnp.sum(jnp.where(of_block, table[None, :], 0), axis=1)
    block_valid = jnp.clip(pick(counts) - (block_row0 - pick(pstart)), 0, ROW_BLOCK)
    present = counts > 0
    slot_e = (jnp.sum(jnp.where(upto, present[None, :].astype(jnp.int32), 0), axis=1) - 1) % 2
    later = jnp.logical_and(experts[None, :] > experts[:, None], present[None, :])
    next_e = jnp.min(jnp.where(later, experts[None, :], N_EXPERTS), axis=1)
    next_e = jnp.where(next_e == N_EXPERTS, -1, next_e)

    idx_parts, idx_dispatch = _slots_call(pstart, route_t, n_rows, COMBINE_PARTS)
    idx_dispatch = idx_dispatch.reshape(SC_WORKERS, -1, LANES)

    x_tab = _sc_dispatch(xn2p.reshape(SUBROWS * t, LANES), idx_dispatch, SUBROWS * n_rows)
    y_tab = _expert_call(
        block_expert, block_valid, pick(next_e), pick(slot_e),
        x_tab.reshape(SUBROWS, n_rows, LANES), x_scale,
        w_gate[0], b_gate[0][:, None, :], w_up[0], b_up[0][:, None, :],
        w_down[0], b_down[0][:, None, :])
    y_flat = y_tab.reshape(SUBROWS * n_rows, LANES)
    tp = t // COMBINE_PARTS
    out = None
    for q in range(COMBINE_PARTS):
        y_slots = _sc_gather(y_flat, idx_parts[q].reshape(SC_WORKERS, -1, LANES))
        out = _combine_call(h, route_t, row(norm_final_g),
                            y_slots.reshape(TOP_K, SUBROWS, tp, LANES), nb, seq,
                            q, COMBINE_PARTS, out)
    return out
```

```python
import functools
import math

import jax
import jax.numpy as jnp
from jax import lax
from jax.experimental import pallas as pl
from jax.experimental.pallas import tpu as pltpu
from jax.experimental.pallas import tpu_sc as plsc

F32 = jnp.float32
BF16 = jnp.bfloat16
F8 = jnp.float8_e4m3fn
F8_TARGET = 240.0
TINY = 1e-30

NORM_EPS = 1e-5
D_MODEL = 1024
SSM_WIDTH = 512
SSM_GROUP = 16
SSM_STATE = 64
SGU_WIDTH = 512
SGU_BLOCK = 128
SGU_HEADS = 4
SGU_HEAD_DIM = 128
CHUNK = 64
N_EXPERTS = 32
TOP_K = 4
SWIGLU_ALPHA = 1.702
SWIGLU_LIMIT = 7.0
GATE_FLOOR = -128.0
LOG2_E = 1.4426950408889634
HID_SCALE = 4.0

LANES = 128
SUBLANES = 8
TIME_TILE = 128
SCAN_BLOCK = 64
ROW_BLOCK = 1024
WEIGHT_DMA_PRIORITY = 1
VMEM_LIMIT = 56 * 1024 * 1024
NEG_BIG = -1e30
SC_CORES = 2
SC_SUBCORES = 16
SC_WORKERS = SC_CORES * SC_SUBCORES
SUBROWS = 4
COMBINE_PARTS = 4
SLOT_TILE = 4096
ROUTE_ROWS = 16
PACKED = D_MODEL // 2


def _pack_rows(x, out_ref):
    lo = lax.bitcast_convert_type(x[:, :PACKED].astype(BF16).astype(F32), jnp.int32)
    hi = lax.bitcast_convert_type(x[:, PACKED:].astype(BF16).astype(F32), jnp.int32)
    words = lax.shift_right_logical(lo, 16) | (hi & jnp.int32(-65536))
    for s in range(SUBROWS):
        out_ref[s] = words[:, s * LANES:(s + 1) * LANES]


def _unpack_rows(ref):
    lo, hi = [], []
    for s in range(SUBROWS):
        w = ref[s]
        lo.append(lax.bitcast_convert_type(lax.shift_left(w, 16), F32))
        hi.append(lax.bitcast_convert_type(w & jnp.int32(-65536), F32))
    return lo, hi


def _dot(a, b):
    return jnp.dot(a, b, preferred_element_type=F32)


def _sigmoid(x):
    return 1.0 / (1.0 + jnp.exp(-x))


def _gelu(x):
    return 0.5 * x * (1.0 + jnp.tanh(0.7978845608028654 * (x + 0.044715 * (x * x * x))))


def _rmsnorm(x, g):
    return x * lax.rsqrt(jnp.mean(x * x, axis=-1, keepdims=True) + NORM_EPS) * g


def _proj_xa(x, g_ref, w_ref):
    xn = _rmsnorm(x, g_ref[...]).astype(BF16)
    return xn, _dot(xn, w_ref[:, 0:SSM_WIDTH]).astype(BF16)


def _proj_gate(xn, w_ref, lo):
    return _sigmoid(_dot(xn, w_ref[:, lo:lo + D_MODEL]))


def _sgu(z, lng_ref, lnb_ref, ws_ref, bs_ref, yb_scr):
    m = z.shape[0]
    u = z[:, :SGU_WIDTH]
    v = z[:, SGU_WIDTH:]
    mu = jnp.mean(v, axis=-1, keepdims=True)
    vc = v - mu
    v = vc * lax.rsqrt(jnp.mean(vc * vc, axis=-1, keepdims=True) + NORM_EPS)
    v = (v * lng_ref[...] + lnb_ref[...]).astype(BF16)

    ri = lax.broadcasted_iota(jnp.int32, (SGU_BLOCK, SGU_BLOCK), 0) // CHUNK
    ci = lax.broadcasted_iota(jnp.int32, (SGU_BLOCK, SGU_BLOCK), 1) // CHUNK
    causal = ri >= ci
    for h in range(SGU_HEADS):
        wm = jnp.where(causal, ws_ref[h], 0.0).astype(BF16)
        bias = bs_ref[:, h:h + 1]
        lo = h * SGU_HEAD_DIM
        for blk in range(m // SGU_BLOCK):
            r0 = blk * SGU_BLOCK
            s = _dot(wm, v[r0:r0 + SGU_BLOCK, lo:lo + SGU_HEAD_DIM]) + bias
            yb_scr[r0:r0 + SGU_BLOCK, lo:lo + SGU_HEAD_DIM] = (
                u[r0:r0 + SGU_BLOCK, lo:lo + SGU_HEAD_DIM] * s).astype(BF16)


def _mix_kernel(x_ref, g1_ref, win_ref, lng_ref, lnb_ref, ws_ref, bs_ref, wb_ref,
                upper_ref, bd_ref, cd_ref,
                are_ref, aim_ref, dskip_ref, wglu_ref, bglu_ref, wa_ref, wout_ref,
                g2_ref, xs_ref, wr_ref, br_ref,
                h_ref, xn2_ref, routet_ref, cnt_ref,
                bu0_scr, bu1_scr, x8_scr, y8_scr, yb_scr, state_scr, cnt_scr):
    nb, tl, d = x_ref.shape
    m = nb * tl
    x = x_ref[...].reshape(m, d)
    xn, xa_bf = _proj_xa(x, g1_ref, win_ref)

    @pl.when(pl.program_id(0) == 0)
    def _():
        state_scr[...] = jnp.zeros_like(state_scr)
        cnt_scr[...] = jnp.zeros_like(cnt_scr)
        x8_scr[...] = jnp.zeros_like(x8_scr)

    xa = xa_bf.astype(F32)

    def chunk_lo(p, hf):
        return hf * (SSM_WIDTH // 2) + p * LANES

    scans = (bu0_scr, bu1_scr)
    for p in range(2):
        for hf in range(2):
            lo = chunk_lo(p, hf)
            for b in range(nb):
                x8_scr[2 * p + hf, pl.ds(hf * nb + b, tl, stride=SUBLANES), :] = (
                    xa[b * tl:(b + 1) * tl, lo:lo + LANES])
        x8 = jnp.concatenate([x8_scr[2 * p], x8_scr[2 * p + 1]], axis=1).astype(BF16)
        scans[p][...] = _dot(x8, bd_ref[p])

    a_re = [are_ref[p] for p in range(2)]
    a_im = [aim_ref[p] for p in range(2)]
    s_re = [state_scr[p][:, :SSM_WIDTH] for p in range(2)]
    s_im = [state_scr[p][:, SSM_WIDTH:] for p in range(2)]
    s_b = SSM_WIDTH + 2 * SGU_WIDTH
    z = _gelu(_dot(xn, win_ref[:, SSM_WIDTH:s_b]))
    gates = []
    for t0 in range(0, tl, SCAN_BLOCK):
        rows_p = ([], [])
        for t in range(t0, t0 + SCAN_BLOCK):
            r = t * SUBLANES
            for p in range(2):
                bu_scr = scans[p]
                n_re = a_re[p] * s_re[p] - a_im[p] * s_im[p] + bu_scr[r:r + SUBLANES, 0:SSM_WIDTH]
                n_im = (a_re[p] * s_im[p] + a_im[p] * s_re[p]
                        + bu_scr[r:r + SUBLANES, SSM_WIDTH:2 * SSM_WIDTH])
                s_re[p], s_im[p] = n_re, n_im
                rows_p[p].append(jnp.concatenate([n_re, n_im], axis=1))
        r0 = t0 * SUBLANES
        for p in range(2):
            y8 = _dot(jnp.concatenate(rows_p[p], axis=0).astype(BF16), cd_ref[p])
            for hf in range(2):
                y8_scr[2 * p + hf, r0:r0 + SCAN_BLOCK * SUBLANES, :] = (
                    y8[:, hf * LANES:(hf + 1) * LANES])
        gates.append(_proj_gate(xn, win_ref, s_b + len(gates) * D_MODEL))

    sga = gates[0]

    y_chunks = {}
    for p in range(2):
        state_scr[p] = jnp.concatenate([s_re[p], s_im[p]], axis=1)
        for hf in range(2):
            y_chunks[(hf, p)] = jnp.concatenate(
                [y8_scr[2 * p + hf, pl.ds(hf * nb + b, tl, stride=SUBLANES), :]
                 for b in range(nb)], axis=0)
    y = jnp.concatenate([y_chunks[(hf, p)] for hf in range(2) for p in range(2)], axis=1)
    y = y + dskip_ref[...] * xa
    zs = _gelu(y)
    ya = zs * _sigmoid(_dot(zs.astype(BF16), wglu_ref[...]) + bglu_ref[...])
    _sgu(z, lng_ref, lnb_ref, ws_ref, bs_ref, yb_scr)
    pb = (gates[1] * _dot(yb_scr[...], wb_ref[...])).astype(BF16)
    pa = sga * _dot(ya.astype(BF16), wa_ref[...])
    merged = pa + pb.astype(F32)
    h = x + _dot(merged.astype(BF16), wout_ref[...])
    _pack_rows(h, h_ref)
    xn2 = _rmsnorm(h, g2_ref[...])
    _pack_rows(xn2 * xs_ref[0:1, 0:1], xn2_ref)

    logits = _dot(xn2.astype(BF16), wr_ref[...]) + br_ref[...]
    work = jnp.transpose(logits)[:N_EXPERTS]
    expert = lax.broadcasted_iota(jnp.int32, (N_EXPERTS, m), 0).astype(F32)
    vals, idxs = [], []
    member = jnp.zeros((N_EXPERTS, m), F32)
    for _ in range(TOP_K):
        mx = jnp.max(work, axis=0, keepdims=True)
        ix = jnp.min(jnp.where(work == mx, expert, float(N_EXPERTS)), axis=0, keepdims=True)
        hit = expert == ix
        member = jnp.where(hit, 1.0, member)
        work = jnp.where(hit, NEG_BIG, work)
        vals.append(mx)
        idxs.append(ix)
    exps = [jnp.exp(v - vals[0]) for v in vals]
    denom = exps[0] + exps[1] + exps[2] + exps[3]
    gates = [e / denom for e in exps]

    before = _dot(member.astype(BF16), upper_ref[...]) + cnt_scr[:, 0:1]
    new_cnt = cnt_scr[:, 0:1] + jnp.sum(member, axis=1, keepdims=True)
    cnt_scr[...] = jnp.broadcast_to(new_cnt, cnt_scr.shape)
    cnt_ref[...] = jnp.broadcast_to(new_cnt, cnt_ref.shape)
    ranks = [jnp.sum(jnp.where(expert == ix, before, 0.0), axis=0, keepdims=True) for ix in idxs]
    pad = jnp.zeros((ROUTE_ROWS - 3 * TOP_K, m), F32)
    routet_ref[...] = jnp.concatenate(idxs + gates + ranks + [pad], axis=0)


def _mix_call(x, *params):
    nb, seq, d = x.shape
    tl = TIME_TILE
    m = nb * tl
    n_tiles = seq // tl
    const = lambda shape: pl.BlockSpec(shape, lambda i: (0,) * len(shape))
    resident = lambda shape: pl.BlockSpec(shape, lambda i: (0,) * len(shape),
                                          pipeline_mode=pl.Buffered(1))
    tile = lambda width: pl.BlockSpec((nb, tl, width), lambda i: (0, i, 0))
    packed_rows = pl.BlockSpec((SUBROWS, m, LANES), lambda i: (0, i, 0))
    operands = (x,) + params
    in_specs = [tile(d)] + [resident(o.shape) for o in params]
    return pl.pallas_call(
        _mix_kernel,
        grid=(n_tiles,),
        in_specs=in_specs,
        out_specs=[packed_rows, packed_rows,
                   pl.BlockSpec((ROUTE_ROWS, m), lambda i: (0, i)), const((N_EXPERTS, LANES))],
        out_shape=[jax.ShapeDtypeStruct((SUBROWS, n_tiles * m, LANES), jnp.int32),
                   jax.ShapeDtypeStruct((SUBROWS, n_tiles * m, LANES), jnp.int32),
                   jax.ShapeDtypeStruct((ROUTE_ROWS, n_tiles * m), F32),
                   jax.ShapeDtypeStruct((N_EXPERTS, LANES), F32)],
        scratch_shapes=[pltpu.VMEM((SUBLANES * tl, 2 * SSM_WIDTH), F32),
                        pltpu.VMEM((SUBLANES * tl, 2 * SSM_WIDTH), F32),
                        pltpu.VMEM((4, SUBLANES * tl, LANES), F32),
                        pltpu.VMEM((4, SUBLANES * tl, LANES), F32),
                        pltpu.VMEM((m, SGU_WIDTH), BF16),
                        pltpu.VMEM((2, SUBLANES, 2 * SSM_WIDTH), F32),
                        pltpu.VMEM((N_EXPERTS, LANES), F32)],
        compiler_params=pltpu.CompilerParams(
            dimension_semantics=("arbitrary",), vmem_limit_bytes=VMEM_LIMIT),
        name="mix",
    )(*operands)


def _sc_mesh():
    return plsc.VectorSubcoreMesh(core_axis_name="c", subcore_axis_name="s")


def _worker_id():
    return lax.axis_index("s") * SC_CORES + lax.axis_index("c")


def _dispatch_body(src_hbm, idx_hbm, out_hbm, idx_v, buf0, buf1, sem_r0, sem_r1, sem_w):
    n_chunks = idx_v.shape[0] // TOP_K
    wid = _worker_id()
    base = wid * (n_chunks * LANES)
    pltpu.sync_copy(idx_hbm.at[wid], idx_v)
    bufs = ((buf0, sem_r0), (buf1, sem_r1))

    def read(j, b):
        return pltpu.make_async_copy(src_hbm.at[pl.ds(base + j * LANES, LANES)], bufs[b][0],
                                     bufs[b][1])

    read(0, 0).start()

    @pl.loop(0, n_chunks, step=2)
    def _(j0):
        for b in range(2):
            j = j0 + b
            read(j, b).wait()

            @pl.when(j + 1 < n_chunks)
            def _():
                read(j + 1, 1 - b).start()

            copies = [pltpu.async_copy(bufs[b][0], out_hbm.at[idx_v.at[j * TOP_K + k]], sem_w)
                      for k in range(TOP_K)]
            for c in copies:
                c.wait()


def _sc_dispatch(src, idx, n_out):
    n_chunks = src.shape[0] // (SC_WORKERS * LANES)
    assert n_chunks % 2 == 0
    return pl.kernel(
        _dispatch_body,
        out_type=jax.ShapeDtypeStruct((n_out, LANES), jnp.int32),
        mesh=_sc_mesh(),
        scratch_types=[pltpu.VMEM((n_chunks * TOP_K, LANES), jnp.int32),
                       pltpu.VMEM((LANES, LANES), jnp.int32),
                       pltpu.VMEM((LANES, LANES), jnp.int32),
                       pltpu.SemaphoreType.DMA, pltpu.SemaphoreType.DMA,
                       pltpu.SemaphoreType.DMA],
        name="sc_dispatch",
    )(src, idx)


def _gather_body(tab_hbm, idx_hbm, out_hbm, idx_v, buf0, buf1, sem0, sem1):
    n_chunks = idx_v.shape[0]
    wid = _worker_id()
    base = wid * (n_chunks * LANES)
    pltpu.sync_copy(idx_hbm.at[wid], idx_v)

    @pl.loop(0, n_chunks, step=2)
    def _(j):
        c0 = pltpu.async_copy(tab_hbm.at[idx_v.at[j]], buf0, sem0)
        c1 = pltpu.async_copy(tab_hbm.at[idx_v.at[j + 1]], buf1, sem1)
        c0.wait()
        pltpu.sync_copy(buf0, out_hbm.at[pl.ds(base + j * LANES, LANES)])
        c1.wait()
        pltpu.sync_copy(buf1, out_hbm.at[pl.ds(base + (j + 1) * LANES, LANES)])


def _sc_gather(tab, idx):
    n_chunks = idx.shape[1]
    return pl.kernel(
        _gather_body,
        out_type=jax.ShapeDtypeStruct((SC_WORKERS * n_chunks * LANES, LANES), jnp.int32),
        mesh=_sc_mesh(),
        scratch_types=[pltpu.VMEM((n_chunks, LANES), jnp.int32),
                       pltpu.VMEM((LANES, LANES), jnp.int32),
                       pltpu.VMEM((LANES, LANES), jnp.int32),
                       pltpu.SemaphoreType.DMA, pltpu.SemaphoreType.DMA],
        name="sc_gather",
    )(tab, idx)


def _slots_kernel(pstart_ref, rt_ref, gat_ref, dis_ref, *, n_rows):
    tb = rt_ref.shape[1]
    idx = rt_ref[0:TOP_K, :]
    dest = rt_ref[2 * TOP_K:3 * TOP_K, :]
    for e in range(N_EXPERTS):
        dest = dest + jnp.where(idx == float(e), pstart_ref[e].astype(F32), 0.0)
    dest = dest.astype(jnp.int32)
    for k in range(TOP_K):
        for s in range(SUBROWS):
            row = dest[k:k + 1, :] + s * n_rows
            gat_ref[k, s:s + 1, :] = row
            for c in range(tb // LANES):
                dis_ref[s, c, k:k + 1, :] = row[:, c * LANES:(c + 1) * LANES]


def _slots_call(pstart, route_t, n_rows, n_parts):
    t = route_t.shape[1]
    tp = t // n_parts
    tb = min(tp, SLOT_TILE)
    per_part = tp // tb
    grid_spec = pltpu.PrefetchScalarGridSpec(
        num_scalar_prefetch=1,
        grid=(t // tb,),
        in_specs=[pl.BlockSpec((ROUTE_ROWS, tb), lambda i, ps: (0, i))],
        out_specs=[pl.BlockSpec((None, TOP_K, SUBROWS, tb),
                                lambda i, ps: (i // per_part, 0, 0, i % per_part)),
                   pl.BlockSpec((SUBROWS, tb // LANES, TOP_K, LANES), lambda i, ps: (0, i, 0, 0))],
    )
    return pl.pallas_call(
        functools.partial(_slots_kernel, n_rows=n_rows),
        grid_spec=grid_spec,
        out_shape=[jax.ShapeDtypeStruct((n_parts, TOP_K, SUBROWS, tp), jnp.int32),
                   jax.ShapeDtypeStruct((SUBROWS, t // LANES, TOP_K, LANES), jnp.int32)],
        compiler_params=pltpu.CompilerParams(dimension_semantics=("arbitrary",)),
        name="slots",
    )(pstart, route_t)


def _expert_kernel(be_ref, bv_ref, nx_ref, sl_ref, x_ref, xs_ref, bg_ref, bu_ref, bd_ref,
                   wg_hbm, wu_hbm, wd_hbm, y_ref, stage, wg_scr, wu_scr, wd_scr, inv_scr, sems):
    i = pl.program_id(0)
    valid = bv_ref[i]
    first = jnp.logical_and(
        valid > 0, jnp.logical_or(i == 0, be_ref[i] != be_ref[jnp.maximum(i - 1, 0)]))

    def weight_copies(expert, slot):
        return [pltpu.make_async_copy(w.at[expert], stage.at[slot, j], sems.at[slot, j])
                for j, w in enumerate((wg_hbm, wu_hbm, wd_hbm))]

    @pl.when(i == 0)
    def _():
        for c in weight_copies(be_ref[0], sl_ref[0]):
            c.start(priority=WEIGHT_DMA_PRIORITY)

    @pl.when(first)
    def _():
        slot = sl_ref[i]
        for c in weight_copies(be_ref[i], slot):
            c.wait()
        for j, scr in enumerate((wg_scr, wu_scr, wd_scr)):
            w = stage[slot, j].astype(BF16)
            amax = jnp.max(jnp.max(jnp.abs(w), axis=0, keepdims=True), axis=1, keepdims=True)
            scale = jnp.exp2(jnp.floor(jnp.log2(
                F8_TARGET / jnp.maximum(amax.astype(F32), TINY))))
            scr[...] = (w * scale.astype(BF16)).astype(F8)
            inv_scr[j] = jnp.broadcast_to(1.0 / scale, inv_scr.shape[1:])

        @pl.when(nx_ref[i] >= 0)
        def _():
            for c in weight_copies(nx_ref[i], 1 - slot):
                c.start(priority=WEIGHT_DMA_PRIORITY)

    @pl.when(valid > 0)
    def _():
        lo, hi = _unpack_rows(x_ref)
        live = lax.broadcasted_iota(jnp.int32, (ROW_BLOCK, 1), 0) < valid
        x = jnp.where(live, jnp.concatenate(lo + hi, axis=1), 0.0).astype(BF16).astype(F8)
        inv_x = 1.0 / xs_ref[0:1, 0:1]
        cg = (inv_scr[0, 0:1, 0:1] * inv_x).astype(BF16)
        cl = (inv_scr[1, 0:1, 0:1] * inv_x * HID_SCALE).astype(BF16)
        g = _dot(x, wg_scr[...]).astype(BF16) * cg + bg_ref[...].astype(BF16)
        g = jnp.clip(g, GATE_FLOOR, SWIGLU_LIMIT)
        l = _dot(x, wu_scr[...]).astype(BF16) * cl + (bu_ref[...] * HID_SCALE).astype(BF16)
        l = jnp.clip(l, -SWIGLU_LIMIT * HID_SCALE, SWIGLU_LIMIT * HID_SCALE) + HID_SCALE
        hid = g * l / (1.0 + jnp.exp2(g * (-SWIGLU_ALPHA * LOG2_E)))
        y = _dot(hid.astype(F8), wd_scr[...])
        _pack_rows(y * (inv_scr[2, 0:1, 0:1] * (1.0 / HID_SCALE)) + bd_ref[...], y_ref)

    @pl.when(valid <= 0)
    def _():
        y_ref[...] = jnp.zeros_like(y_ref)


def _expert_call(block_expert, block_valid, next_expert, slot, x_tab, x_scale,
                 w_gate, b_gate, w_up, b_up, w_down, b_down):
    d, f = w_gate.shape[-2:]
    assert d == f
    n_blocks = block_expert.shape[0]
    bspec = lambda width: pl.BlockSpec((None, 1, width), lambda i, be, bv, nx, sl: (be[i], 0, 0))
    rows = pl.BlockSpec((SUBROWS, ROW_BLOCK, LANES), lambda i, be, bv, nx, sl: (0, i, 0))
    hbm = pl.BlockSpec(memory_space=pl.ANY)
    grid_spec = pltpu.PrefetchScalarGridSpec(
        num_scalar_prefetch=4,
        grid=(n_blocks,),
        in_specs=[rows, pl.BlockSpec((SUBLANES, LANES), lambda i, be, bv, nx, sl: (0, 0)),
                  bspec(f), bspec(f), bspec(d), hbm, hbm, hbm],
        out_specs=rows,
        scratch_shapes=[pltpu.VMEM((2, 3, d, f), F32),
                        pltpu.VMEM((d, f), F8), pltpu.VMEM((d, f), F8), pltpu.VMEM((f, d), F8),
                        pltpu.VMEM((3, SUBLANES, LANES), F32), pltpu.SemaphoreType.DMA((2, 3))],
    )
    return pl.pallas_call(
        _expert_kernel,
        grid_spec=grid_spec,
        out_shape=jax.ShapeDtypeStruct(x_tab.shape, jnp.int32),
        compiler_params=pltpu.CompilerParams(
            dimension_semantics=("arbitrary",), vmem_limit_bytes=VMEM_LIMIT),
        name="expert",
    )(block_expert, block_valid, next_expert, slot, x_tab, x_scale, b_gate, b_up, b_down,
      w_gate, w_up, w_down)


def _combine_kernel(h_ref, routet_ref, gf_ref, ys_ref, *rest):
    out_ref = rest[-1]
    nb, tl, d = out_ref.shape
    m = nb * tl
    route = jnp.transpose(jnp.concatenate(
        [routet_ref[...], jnp.zeros((LANES - ROUTE_ROWS, m), F32)], axis=0))
    lo_acc, hi_acc = _unpack_rows(h_ref)
    for k in range(TOP_K):
        gate = route[:, TOP_K + k:TOP_K + k + 1]
        lo, hi = _unpack_rows(ys_ref.at[k])
        lo_acc = [a + gate * v for a, v in zip(lo_acc, lo)]
        hi_acc = [a + gate * v for a, v in zip(hi_acc, hi)]
    acc = jnp.concatenate(lo_acc + hi_acc, axis=1)
    out_ref[...] = _rmsnorm(acc, gf_ref[...]).reshape(nb, tl, d)


def _combine_call(h, route_t, g_final, y_slots, nb, seq, part, n_parts, prev):
    t, d = h.shape[1], D_MODEL
    tl = TIME_TILE
    m = nb * tl
    n = t // m // n_parts
    first = part * n
    in_specs = [pl.BlockSpec((SUBROWS, m, LANES), lambda i: (0, first + i, 0)),
                pl.BlockSpec((ROUTE_ROWS, m), lambda i: (0, first + i)),
                pl.BlockSpec((1, d), lambda i: (0, 0)),
                pl.BlockSpec((TOP_K, SUBROWS, m, LANES), lambda i: (0, 0, i, 0))]
    operands = [h, route_t, g_final, y_slots]
    aliases = {}
    if prev is not None:
        in_specs.append(pl.BlockSpec(memory_space=pl.ANY))
        operands.append(prev)
        aliases = {4: 0}
    return pl.pallas_call(
        _combine_kernel,
        grid=(n,),
        in_specs=in_specs,
        out_specs=pl.BlockSpec((nb, tl, d), lambda i: (0, first + i, 0)),
        out_shape=jax.ShapeDtypeStruct((nb, seq, d), F32),
        input_output_aliases=aliases,
        compiler_params=pltpu.CompilerParams(
            dimension_semantics=("arbitrary",), vmem_limit_bytes=VMEM_LIMIT),
        name="combine",
    )(*operands)


def _s5_operands(lam_re, lam_im, log_dt, b_re, b_im, c_re, c_im, nb):
    ns, pg = SSM_STATE, SSM_GROUP
    lam = lax.complex(lam_re.astype(F32), lam_im.astype(F32))
    dt = jnp.exp(log_dt.astype(F32))[:, None]
    lam_bar = jnp.exp(lam * dt)
    b_bar = ((lam_bar - 1.0) / lam)[..., None] * lax.complex(b_re.astype(F32), b_im.astype(F32))
    eye8 = jnp.eye(8, dtype=F32)
    split = lambda a: a.reshape((2, 2, 8) + a.shape[1:])

    def b_blocks(bb):
        return jnp.einsum('hpgnq,gk->phgqkn', split(bb), eye8).reshape(2, 2 * 8 * pg, 8 * ns)

    def c_blocks(cc):
        return jnp.einsum('hpgqn,gk->pgnhkq', split(cc), eye8).reshape(2, 8 * ns, 2 * 8 * pg)

    bd = jnp.concatenate([b_blocks(b_bar.real), b_blocks(b_bar.imag)], axis=2).astype(BF16)
    cd = jnp.concatenate([c_blocks(c_re.astype(F32)), -c_blocks(c_im.astype(F32))],
                         axis=1).astype(BF16)
    lam_rows = split(lam_bar).transpose(1, 0, 2, 3).reshape(2, 2, 8 * ns)
    lam_rows = jnp.repeat(lam_rows, nb, axis=1)
    return bd, cd, lam_rows.real, lam_rows.imag


def kernel(x, norm_mix_g, w_in, lam_re, lam_im, log_dt, b_re, b_im, c_re, c_im, d_skip, w_glu, b_glu, sgu_ln_g, sgu_ln_b, w_s, b_s, w_branch_a, w_branch_b, w_out, norm_moe_g, w_router, b_router, w_gate, b_gate, w_up, b_up, w_down, b_down, norm_final_g):
    nb, seq, d = x.shape
    assert d == D_MODEL and SUBLANES % nb == 0 and SUBLANES // nb == 2
    assert seq % TIME_TILE == 0 and norm_mix_g.shape[0] == 1
    assert (nb * seq * SUBROWS) % (SC_WORKERS * LANES) == 0
    assert (nb * seq * SUBROWS * TOP_K) % (SC_WORKERS * LANES * 2 * COMBINE_PARTS) == 0
    assert (seq // TIME_TILE) % COMBINE_PARTS == 0
    tl = TIME_TILE
    t = nb * seq
    row = lambda v: v.reshape(1, -1).astype(F32)

    bd, cd, a_re, a_im = _s5_operands(
        lam_re[0], lam_im[0], log_dt[0], b_re[0], b_im[0], c_re[0], c_im[0], nb)
    w_r = jnp.zeros((d, LANES), F32).at[:, :N_EXPERTS].set(w_router[0].astype(F32)).astype(BF16)
    b_r = jnp.full((1, LANES), NEG_BIG, F32).at[0, :N_EXPERTS].set(b_router[0].astype(F32))
    x_bound = math.sqrt(d) * jnp.max(jnp.abs(norm_moe_g[0].astype(F32)))
    x_scale = jnp.exp2(jnp.floor(jnp.log2(F8_TARGET / jnp.maximum(x_bound, TINY))))
    x_scale = jnp.full((SUBLANES, LANES), 1.0, F32) * x_scale
    m = nb * tl
    upper = (jnp.arange(m)[:, None] < jnp.arange(m)[None, :]).astype(BF16)
    h, xn2p, route_t, cnt = _mix_call(
        x, row(norm_mix_g[0]), w_in[0].astype(BF16), row(sgu_ln_g[0]), row(sgu_ln_b[0]),
        w_s[0].astype(F32), b_s[0].T.astype(F32), w_branch_b[0].astype(BF16),
        upper, bd, cd, a_re, a_im, row(d_skip[0]),
        w_glu[0].astype(BF16), row(b_glu[0]), w_branch_a[0].astype(BF16), w_out[0].astype(BF16),
        row(norm_moe_g[0]), x_scale, w_r, b_r)

    counts = cnt[:, 0].astype(jnp.int32)
    padded = (counts + ROW_BLOCK - 1) // ROW_BLOCK * ROW_BLOCK
    experts = jnp.arange(N_EXPERTS, dtype=jnp.int32)
    upto = experts[None, :] <= experts[:, None]
    cum = jnp.sum(jnp.where(upto, padded[None, :], 0), axis=1)
    pstart = cum - padded
    n_blocks = (t * TOP_K) // ROW_BLOCK + N_EXPERTS
    n_rows = n_blocks * ROW_BLOCK
    block_row0 = jnp.arange(n_blocks, dtype=jnp.int32) * ROW_BLOCK
    block_expert = jnp.minimum(
        jnp.sum((cum[None, :] <= block_row0[:, None]).astype(jnp.int32), axis=1), N_EXPERTS - 1)
    of_block = block_expert[:, None] == experts[None, :]
    pick = lambda table: jnp.sum(jnp.where(of_block, table[None, :], 0), axis=1)
    block_valid = jnp.clip(pick(counts) - (block_row0 - pick(pstart)), 0, ROW_BLOCK)
    present = counts > 0
    slot_e = (jnp.sum(jnp.where(upto, present[None, :].astype(jnp.int32), 0), axis=1) - 1) % 2
    later = jnp.logical_and(experts[None, :] > experts[:, None], present[None, :])
    next_e = jnp.min(jnp.where(later, experts[None, :], N_EXPERTS), axis=1)
    next_e = jnp.where(next_e == N_EXPERTS, -1, next_e)

    idx_parts, idx_dispatch = _slots_call(pstart, route_t, n_rows, COMBINE_PARTS)
    idx_dispatch = idx_dispatch.reshape(SC_WORKERS, -1, LANES)

    x_tab = _sc_dispatch(xn2p.reshape(SUBROWS * t, LANES), idx_dispatch, SUBROWS * n_rows)
    y_tab = _expert_call(
        block_expert, block_valid, pick(next_e), pick(slot_e),
        x_tab.reshape(SUBROWS, n_rows, LANES), x_scale,
        w_gate[0], b_gate[0][:, None, :], w_up[0], b_up[0][:, None, :],
        w_down[0], b_down[0][:, None, :])
    y_flat = y_tab.reshape(SUBROWS * n_rows, LANES)
    tp = t // COMBINE_PARTS
    out = None
    for q in range(COMBINE_PARTS):
        y_slots = _sc_gather(y_flat, idx_parts[q].reshape(SC_WORKERS, -1, LANES))
        out = _combine_call(h, route_t, row(norm_final_g),
                            y_slots.reshape(TOP_K, SUBROWS, tp, LANES), nb, seq,
                            q, COMBINE_PARTS, out)
    return out
```
